```python
import math, functools
import jax, jax.numpy as jnp
from jax import lax
import numpy as np

D_MODEL = 1024
BATCH = 16
SEQ = 2048
DEPTH = 1
DEC_BATCH = 128
DEC_SEQ = 8
PAST_LEN = 8192
PAGE_SIZE = 128

D_INNER = 2 * D_MODEL
SSD_HEAD_DIM = 64
SSD_HEADS = D_INNER // SSD_HEAD_DIM
SSD_GROUPS = 8
SSD_STATE = 128
SSD_CONV = 4
SSD_CHUNK = 128
CONV_DIM = D_INNER + 2 * SSD_GROUPS * SSD_STATE
ATT_HEADS = 16
ATT_HEAD_DIM = 64
KV_HEADS = 4
GQA_REP = ATT_HEADS // KV_HEADS
ATT_WIDTH = ATT_HEADS * ATT_HEAD_DIM
KV_WIDTH = KV_HEADS * ATT_HEAD_DIM
MOBA_BLOCK = 256
MOBA_TOPK = 3
Q_BLOCK = 128
ROT_DIM = ATT_HEAD_DIM // 4
ROPE_THETA = 500000.0
MEM_TOKENS = 256
MEM_HEADS = 4
MEM_HEAD_DIM = 256
MEM_WIDTH = MEM_HEADS * MEM_HEAD_DIM
D_FF = 2816
FFN_CONV = 3
N_BRANCH = 3
RMS_EPS = 1e-6
NEG_INF = -1e30
IN_SPLITS = (D_INNER, CONV_DIM, SSD_HEADS, ATT_WIDTH, KV_WIDTH, KV_WIDTH, MEM_WIDTH, N_BRANCH * D_MODEL)
D_IN_PROJ = D_INNER + CONV_DIM + SSD_HEADS + ATT_WIDTH + 2 * KV_WIDTH + MEM_WIDTH + N_BRANCH * D_MODEL

kernel_name = "hybrid_ssd_moba_mem_convffn_step"


def _rmsnorm(x, w):
    xf = x.astype(jnp.float32)
    y = xf * lax.rsqrt(jnp.mean(xf * xf, axis=-1, keepdims=True) + RMS_EPS)
    return (y * w.astype(jnp.float32)).astype(x.dtype)


def _gated_group_rmsnorm(y, z, w, out_dtype):
    g = y.astype(jnp.float32) * jax.nn.silu(z.astype(jnp.float32))
    shp = g.shape
    g = g.reshape(*shp[:-1], SSD_GROUPS, D_INNER // SSD_GROUPS)
    g = g * lax.rsqrt(jnp.mean(g * g, axis=-1, keepdims=True) + RMS_EPS)
    return (g.reshape(shp) * w.astype(jnp.float32)).astype(out_dtype)


def _split_cols(x, sizes):
    out, start = [], 0
    for s in sizes:
        out.append(x[..., start:start + s])
        start += s
    return out


def _causal_dwconv(x, buf, w, b):
    k = w.shape[0]
    l = x.shape[1]
    xp = jnp.concatenate([buf.astype(x.dtype), x], axis=1)
    y = xp[:, 0:l] * w[0]
    for i in range(1, k):
        y = y + xp[:, i:i + l] * w[i]
    return y + b, xp[:, xp.shape[1] - (k - 1):]


def _partial_rope(x, pos):
    half = ROT_DIM // 2
    inv = ROPE_THETA ** (-jnp.arange(half, dtype=jnp.float32) * 2.0 / ROT_DIM)
    ang = pos.astype(jnp.float32)[:, None] * inv[None, :]
    cos = jnp.cos(ang)[None, :, None, :]
    sin = jnp.sin(ang)[None, :, None, :]
    xf = x.astype(jnp.float32)
    x1, x2 = xf[..., :half], xf[..., half:ROT_DIM]
    out = jnp.concatenate([x1 * cos - x2 * sin, x2 * cos + x1 * sin, xf[..., ROT_DIM:]], axis=-1)
    return out.astype(x.dtype)


def _ssd_chunked(x, dt, a, bmat, cmat, init_state, chunk):
    f32 = jnp.float32
    b, l, h, p = x.shape
    g, n = bmat.shape[2], bmat.shape[3]
    r = h // g
    c = l // chunk
    cs = jnp.cumsum((dt * a).reshape(b, c, chunk, h), axis=2)
    xdt = (x.astype(f32) * dt[..., None]).reshape(b, c, chunk, g, r, p)
    bc = bmat.astype(f32).reshape(b, c, chunk, g, n)
    cc = cmat.astype(f32).reshape(b, c, chunk, g, n)
    causal = jnp.tril(jnp.ones((chunk, chunk), dtype=bool))
    seg = cs[:, :, :, None, :] - cs[:, :, None, :, :]
    lmat = jnp.exp(jnp.where(causal[None, None, :, :, None], seg, -jnp.inf)).reshape(b, c, chunk, chunk, g, r)
    cb = jnp.einsum("bcign,bcjgn->bcijg", cc, bc)
    y_diag = jnp.einsum("bcijgr,bcjgrp->bcigrp", cb[..., None] * lmat, xdt)
    decay_to_end = jnp.exp(cs[:, :, -1:, :] - cs).reshape(b, c, chunk, g, r)
    states = jnp.einsum("bcjgn,bcjgrp->bcgrpn", bc, decay_to_end[..., None] * xdt)
    chunk_decay = jnp.exp(cs[:, :, -1, :]).reshape(b, c, g, r)

    def step(s, inp):
        st, dec = inp
        return s * dec[..., None, None] + st, s

    s_final, s_start = lax.scan(step, init_state.astype(f32).reshape(b, g, r, p, n),
                                (jnp.moveaxis(states, 1, 0), jnp.moveaxis(chunk_decay, 1, 0)))
    s_start = jnp.moveaxis(s_start, 0, 1)
    y_off = jnp.einsum("bcign,bcgrpn->bcigrp", cc, s_start) * jnp.exp(cs).reshape(b, c, chunk, g, r)[..., None]
    y = (y_diag + y_off).reshape(b, l, h, p)
    return y, s_final.reshape(b, h, p, n)


def _moba_core(q, k_sel, v_sel, sel_ok, k_own, v_own, own_ok):
    f32 = jnp.float32
    h, nq, hd = q.shape
    qs = q.astype(f32) * (hd ** -0.5)
    s_own = jnp.einsum("grqd,gkd->grqk", qs.reshape(KV_HEADS, GQA_REP, nq, hd), k_own.astype(f32)).reshape(h, nq, -1)
    s_own = jnp.where(own_ok[None], s_own, NEG_INF)
    n_own = s_own.shape[-1]
    if k_sel is None:
        p = jax.nn.softmax(s_own, axis=-1)
        out = 0.0
    else:
        s_sel = jnp.where(sel_ok, jnp.einsum("hqd,hqkd->hqk", qs, k_sel.astype(f32)), NEG_INF)
        n_sel = s_sel.shape[-1]
        p = jax.nn.softmax(jnp.concatenate([s_sel, s_own], axis=-1), axis=-1)
        out = jnp.einsum("hqk,hqkd->hqd", p[..., :n_sel], v_sel.astype(f32))
        p = p[..., n_sel:]
    out = out + jnp.einsum("grqk,gkd->grqd", p.reshape(KV_HEADS, GQA_REP, nq, n_own), v_own.astype(f32)).reshape(h, nq, hd)
    return out.transpose(1, 0, 2).astype(q.dtype)


def _moba_prompt(q, k, v):
    b, s, h, hd = q.shape
    nb = -(-s // MOBA_BLOCK)
    pad = nb * MOBA_BLOCK - s
    nqb = s // Q_BLOCK
    topk = min(MOBA_TOPK, nb - 1)
    padw = ((0, 0), (0, pad), (0, 0), (0, 0))
    k_blk = jnp.pad(k, padw).reshape(b, nb, MOBA_BLOCK, KV_HEADS, hd).transpose(0, 3, 1, 2, 4)
    v_blk = jnp.pad(v, padw).reshape(b, nb, MOBA_BLOCK, KV_HEADS, hd).transpose(0, 3, 1, 2, 4)
    own_blk = jnp.arange(s) // MOBA_BLOCK
    q_b = q.reshape(b, nqb, Q_BLOCK, h, hd).transpose(0, 1, 3, 2, 4).reshape(b * nqb, h, Q_BLOCK, hd)
    sel_b, ok_b = None, None
    if topk > 0:
        k_mean = jnp.mean(k_blk.astype(jnp.float32), axis=3)
        gate = jnp.einsum("bsgrd,bgnd->bgrsn", q.astype(jnp.float32).reshape(b, s, KV_HEADS, GQA_REP, hd), k_mean).reshape(b, h, s, nb)
        gate = jnp.where(jnp.arange(nb)[None, :] < own_blk[:, None], gate, NEG_INF)
        _, sel = lax.top_k(gate, topk)
        ok = sel < own_blk[None, None, :, None]
        sel_b = sel.reshape(b, h, nqb, Q_BLOCK, topk).transpose(0, 2, 1, 3, 4).reshape(b * nqb, h, Q_BLOCK, topk)
        ok_b = ok.reshape(b, h, nqb, Q_BLOCK, topk).transpose(0, 2, 1, 3, 4).reshape(b * nqb, h, Q_BLOCK, topk)
    bidx = jnp.arange(b * nqb) // nqb
    qidx = jnp.arange(b * nqb) % nqb
    head_kv = jnp.arange(h) // GQA_REP

    def body(inp):
        qb, selq, okq, bi, qi = inp
        kb, vb = k_blk[bi], v_blk[bi]
        ob = qi * Q_BLOCK // MOBA_BLOCK
        k_own = lax.dynamic_index_in_dim(kb, ob, axis=1, keepdims=False)
        v_own = lax.dynamic_index_in_dim(vb, ob, axis=1, keepdims=False)
        qpos = qi * Q_BLOCK + jnp.arange(Q_BLOCK)
        kpos = ob * MOBA_BLOCK + jnp.arange(MOBA_BLOCK)
        own_ok = kpos[None, :] <= qpos[:, None]
        if selq is None:
            return _moba_core(qb, None, None, None, k_own, v_own, own_ok)
        k_sel = kb[head_kv[:, None, None], selq].reshape(h, Q_BLOCK, topk * MOBA_BLOCK, hd)
        v_sel = vb[head_kv[:, None, None], selq].reshape(h, Q_BLOCK, topk * MOBA_BLOCK, hd)
        sel_ok = jnp.repeat(okq, MOBA_BLOCK, axis=-1)
        return _moba_core(qb, k_sel, v_sel, sel_ok, k_own, v_own, own_ok)

    out = lax.map(body, (q_b, sel_b, ok_b, bidx, qidx))
    return out.reshape(b, s, h, hd)


def _moba_sample(q, k, v, cache_k, cache_v, page_table, layer):
    db, t, h, hd = q.shape
    n_pages = page_table.shape[1]
    past = n_pages * PAGE_SIZE
    n_full = past // MOBA_BLOCK
    own_start = n_full * MOBA_BLOCK
    topk = min(MOBA_TOPK, n_full)
    ppb = MOBA_BLOCK // PAGE_SIZE
    head_kv = jnp.arange(h) // GQA_REP
    row = jnp.arange(PAGE_SIZE)
    qpos = past + jnp.arange(t)
    kpos = own_start + jnp.arange(past - own_start + t)
    own_ok = kpos[None, :] <= qpos[:, None]

    def body(inp):
        qb, kb, vb, pt = inp
        k_past = cache_k[layer, pt].reshape(past, KV_HEADS, hd)
        v_tail = cache_v[layer, pt[own_start // PAGE_SIZE:]].reshape(-1, KV_HEADS, hd)
        k_own = jnp.concatenate([k_past[own_start:], kb.astype(k_past.dtype)], axis=0).transpose(1, 0, 2)
        v_own = jnp.concatenate([v_tail, vb.astype(v_tail.dtype)], axis=0).transpose(1, 0, 2)
        qh = qb.transpose(1, 0, 2)
        if topk == 0:
            return _moba_core(qh, None, None, None, k_own, v_own, own_ok)
        k_mean = jnp.mean(k_past[:own_start].astype(jnp.float32).reshape(n_full, MOBA_BLOCK, KV_HEADS, hd), axis=1)
        gate = jnp.einsum("grtd,ngd->grtn", qh.astype(jnp.float32).reshape(KV_HEADS, GQA_REP, t, hd), k_mean).reshape(h, t, n_full)
        _, sel = lax.top_k(gate, topk)
        phys = pt[sel[..., None] * ppb + jnp.arange(ppb)]
        gi = head_kv[:, None, None, None, None]
        k_sel = cache_k[layer, phys[..., None], row, gi].reshape(h, t, topk * MOBA_BLOCK, hd)
        v_sel = cache_v[layer, phys[..., None], row, gi].reshape(h, t, topk * MOBA_BLOCK, hd)
        sel_ok = jnp.ones((h, t, topk * MOBA_BLOCK), dtype=bool)
        return _moba_core(qh, k_sel, v_sel, sel_ok, k_own, v_own, own_ok)

    return lax.map(body, (q, k, v, page_table))


def _mem_attend(qm, mem_k, mem_v):
    b, l, _ = qm.shape
    q = qm.astype(jnp.float32).reshape(b, l, MEM_HEADS, MEM_HEAD_DIM) * (MEM_HEAD_DIM ** -0.5)
    s = jnp.einsum("blhd,bmhd->bhlm", q, mem_k.astype(jnp.float32))
    p = jax.nn.softmax(s, axis=-1)
    out = jnp.einsum("bhlm,bmhd->blhd", p, mem_v.astype(jnp.float32))
    return out.reshape(b, l, MEM_WIDTH).astype(qm.dtype)


def _layer(x, pos, mem_k, mem_v, ssm_init, conv_buf, ffn_buf, attend, lw):
    b, l, _ = x.shape
    xn = _rmsnorm(x, lw["norm1_w"])
    z, xbc, dt_raw, q, k, v, qm, gates = _split_cols(xn @ lw["w_in"], IN_SPLITS)
    xbc, conv_new = _causal_dwconv(xbc, conv_buf, lw["ssd_conv_w"], lw["ssd_conv_b"])
    xbc = jax.nn.silu(xbc)
    xs, bm, cm = _split_cols(xbc, (D_INNER, SSD_GROUPS * SSD_STATE, SSD_GROUPS * SSD_STATE))
    xs = xs.reshape(b, l, SSD_HEADS, SSD_HEAD_DIM)
    bm = bm.reshape(b, l, SSD_GROUPS, SSD_STATE)
    cm = cm.reshape(b, l, SSD_GROUPS, SSD_STATE)
    dt = jax.nn.softplus(dt_raw.astype(jnp.float32) + lw["dt_bias"].astype(jnp.float32))
    a = -jnp.exp(lw["a_log"].astype(jnp.float32))
    y, ssm_final = _ssd_chunked(xs, dt, a, bm, cm, ssm_init, math.gcd(l, SSD_CHUNK))
    y = y + lw["d_skip"].astype(jnp.float32)[:, None] * xs.astype(jnp.float32)
    y_ssd = _gated_group_rmsnorm(y.reshape(b, l, D_INNER), z, lw["ssd_norm_w"], x.dtype)
    q = _partial_rope(q.reshape(b, l, ATT_HEADS, ATT_HEAD_DIM), pos)
    k = _partial_rope(k.reshape(b, l, KV_HEADS, ATT_HEAD_DIM), pos)
    v = v.reshape(b, l, KV_HEADS, ATT_HEAD_DIM)
    y_att = attend(q, k, v).reshape(b, l, ATT_WIDTH)
    y_mem = _mem_attend(qm, mem_k, mem_v)
    gs = jax.nn.sigmoid(gates.astype(jnp.float32).reshape(b, l, N_BRANCH, D_MODEL)).astype(x.dtype)
    merged = (gs[:, :, 0] * (y_ssd @ lw["w_ssd_out"]) + gs[:, :, 1] * (y_att @ lw["w_attn_out"])
              + gs[:, :, 2] * (y_mem @ lw["w_mem_out"]))
    h = x + merged @ lw["w_o"]
    u = _rmsnorm(h, lw["norm2_w"]) @ lw["w_up"]
    uc, ffn_new = _causal_dwconv(u, ffn_buf, lw["ffn_conv_w"], lw["ffn_conv_b"])
    ug, uv = uc[..., :D_FF], uc[..., D_FF:]
    h = h + (jax.nn.silu(ug) * uv) @ lw["w_down"]
    return h, k, v, ssm_final.astype(x.dtype), conv_new, ffn_new


def setup_inputs(seed: int = 0) -> dict:
    key = jax.random.key(seed)
    keys = list(jax.random.split(key, 40))
    f32 = jnp.float32

    def nrm(shape, scale=1.0):
        return scale * jax.random.normal(keys.pop(), shape, f32)

    n_pages = PAST_LEN // PAGE_SIZE
    n_used = DEC_BATCH * n_pages
    n_phys = n_used + n_used // 4
    page_table = jax.random.permutation(keys.pop(), n_phys)[:n_used].reshape(DEC_BATCH, n_pages).astype(jnp.int32)
    dt0 = jnp.exp(jax.random.uniform(keys.pop(), (DEPTH, SSD_HEADS), f32, math.log(1e-3), math.log(1e-1)))
    dt_bias = dt0 + jnp.log(-jnp.expm1(-dt0))
    a_log = jnp.log(jax.random.uniform(keys.pop(), (DEPTH, SSD_HEADS), f32, 1.0, 16.0))
    return {
        "x_prompt": nrm((BATCH, SEQ, D_MODEL)),
        "x_sample": nrm((DEC_BATCH, DEC_SEQ, D_MODEL)),
        "cache_k": nrm((DEPTH, n_phys, PAGE_SIZE, KV_HEADS, ATT_HEAD_DIM)),
        "cache_v": nrm((DEPTH, n_phys, PAGE_SIZE, KV_HEADS, ATT_HEAD_DIM)),
        "cache_mem_k": nrm((DEPTH, DEC_BATCH, MEM_TOKENS, MEM_HEADS, MEM_HEAD_DIM)),
        "cache_mem_v": nrm((DEPTH, DEC_BATCH, MEM_TOKENS, MEM_HEADS, MEM_HEAD_DIM)),
        "state_ssm": nrm((DEPTH, DEC_BATCH, SSD_HEADS, SSD_HEAD_DIM, SSD_STATE), 0.3),
        "state_conv": nrm((DEPTH, DEC_BATCH, SSD_CONV - 1, CONV_DIM)),
        "state_ffn_conv": nrm((DEPTH, DEC_BATCH, FFN_CONV - 1, 2 * D_FF)),
        "page_table": page_table,
        "mem_prompt": nrm((BATCH, MEM_TOKENS, D_MODEL)),
        "norm1_w": 1.0 + nrm((DEPTH, D_MODEL), 0.02),
        "w_in": nrm((DEPTH, D_MODEL, D_IN_PROJ), D_MODEL ** -0.5),
        "ssd_conv_w": nrm((DEPTH, SSD_CONV, CONV_DIM), SSD_CONV ** -0.5),
        "ssd_conv_b": nrm((DEPTH, CONV_DIM), 0.01),
        "dt_bias": dt_bias,
        "a_log": a_log,
        "d_skip": 1.0 + nrm((DEPTH, SSD_HEADS), 0.1),
        "ssd_norm_w": 1.0 + nrm((DEPTH, D_INNER), 0.02),
        "mem_norm_w": 1.0 + nrm((DEPTH, D_MODEL), 0.02),
        "w_mem_kv": nrm((DEPTH, D_MODEL, 2 * MEM_WIDTH), D_MODEL ** -0.5),
        "w_ssd_out": nrm((DEPTH, D_INNER, D_MODEL), D_INNER ** -0.5),
        "w_attn_out": nrm((DEPTH, ATT_WIDTH, D_MODEL), ATT_WIDTH ** -0.5),
        "w_mem_out": nrm((DEPTH, MEM_WIDTH, D_MODEL), MEM_WIDTH ** -0.5),
        "w_o": nrm((DEPTH, D_MODEL, D_MODEL), D_MODEL ** -0.5),
        "norm2_w": 1.0 + nrm((DEPTH, D_MODEL), 0.02),
        "w_up": nrm((DEPTH, D_MODEL, 2 * D_FF), D_MODEL ** -0.5),
        "ffn_conv_w": nrm((DEPTH, FFN_CONV, 2 * D_FF), FFN_CONV ** -0.5),
        "ffn_conv_b": nrm((DEPTH, 2 * D_FF), 0.01),
        "w_down": nrm((DEPTH, D_FF, D_MODEL), D_FF ** -0.5),
        "final_norm_w": 1.0 + nrm((D_MODEL,), 0.02),
    }


def reference(x_prompt, x_sample, cache_k, cache_v, cache_mem_k, cache_mem_v, state_ssm, state_conv,
              state_ffn_conv, page_table, mem_prompt, norm1_w, w_in, ssd_conv_w, ssd_conv_b, dt_bias, a_log,
              d_skip, ssd_norm_w, mem_norm_w, w_mem_kv, w_ssd_out, w_attn_out, w_mem_out, w_o, norm2_w, w_up,
              ffn_conv_w, ffn_conv_b, w_down, final_norm_w):
    b_p, s_p, _ = x_prompt.shape
    n_mem = mem_prompt.shape[1]
    pos_p = jnp.arange(s_p, dtype=jnp.int32)
    pos_s = page_table.shape[1] * PAGE_SIZE + jnp.arange(x_sample.shape[1], dtype=jnp.int32)
    hp, hs = x_prompt, x_sample
    kp_l, vp_l, mkp_l, mvp_l, sp_l, cp_l, fp_l = [], [], [], [], [], [], []
    ks_l, vs_l, ss_l, cs_l, fs_l = [], [], [], [], []
    for l in range(DEPTH):
        lw = {"norm1_w": norm1_w[l], "w_in": w_in[l], "ssd_conv_w": ssd_conv_w[l], "ssd_conv_b": ssd_conv_b[l],
              "dt_bias": dt_bias[l], "a_log": a_log[l], "d_skip": d_skip[l], "ssd_norm_w": ssd_norm_w[l],
              "w_ssd_out": w_ssd_out[l], "w_attn_out": w_attn_out[l], "w_mem_out": w_mem_out[l], "w_o": w_o[l],
              "norm2_w": norm2_w[l], "w_up": w_up[l], "ffn_conv_w": ffn_conv_w[l], "ffn_conv_b": ffn_conv_b[l],
              "w_down": w_down[l]}
        mkv = (_rmsnorm(mem_prompt, mem_norm_w[l]) @ w_mem_kv[l]).reshape(b_p, n_mem, 2, MEM_HEADS, MEM_HEAD_DIM)
        mk_p, mv_p = mkv[:, :, 0], mkv[:, :, 1]
        ssm0 = jnp.zeros((b_p, SSD_HEADS, SSD_HEAD_DIM, SSD_STATE), x_prompt.dtype)
        conv0 = jnp.zeros((b_p, SSD_CONV - 1, CONV_DIM), x_prompt.dtype)
        ffn0 = jnp.zeros((b_p, FFN_CONV - 1, 2 * D_FF), x_prompt.dtype)
        hp, kp, vp, sp, cp, fp = _layer(hp, pos_p, mk_p, mv_p, ssm0, conv0, ffn0, _moba_prompt, lw)
        attend_s = functools.partial(_moba_sample, cache_k=cache_k, cache_v=cache_v, page_table=page_table, layer=l)
        hs, ks, vs, ss, cs, fs = _layer(hs, pos_s, cache_mem_k[l], cache_mem_v[l], state_ssm[l], state_conv[l],
                                        state_ffn_conv[l], attend_s, lw)
        kp_l.append(kp); vp_l.append(vp); mkp_l.append(mk_p); mvp_l.append(mv_p)
        sp_l.append(sp); cp_l.append(cp); fp_l.append(fp)
        ks_l.append(ks); vs_l.append(vs); ss_l.append(ss); cs_l.append(cs); fs_l.append(fs)
    y_prompt = _rmsnorm(hp, final_norm_w)
    y_sample = _rmsnorm(hs, final_norm_w)
    return (y_prompt, y_sample, jnp.stack(kp_l), jnp.stack(vp_l), jnp.stack(mkp_l), jnp.stack(mvp_l),
            jnp.stack(sp_l), jnp.stack(cp_l), jnp.stack(fp_l), jnp.stack(ks_l), jnp.stack(vs_l),
            jnp.stack(ss_l), jnp.stack(cs_l), jnp.stack(fs_l))
```

```python
import functools
import math

import jax
import jax.numpy as jnp
from jax import lax
from jax.experimental import pallas as pl
from jax.experimental.pallas import tpu as pltpu

F32 = jnp.float32
BF16 = jnp.bfloat16

D_MODEL = 1024
D_INNER = 2048
SSD_HEAD_DIM = 64
SSD_HEADS = 32
SSD_GROUPS = 8
SSD_STATE = 128
SSD_CONV = 4
SSD_CHUNK = 128
CONV_DIM = 4096
ATT_HEADS = 16
ATT_HEAD_DIM = 64
KV_HEADS = 4
GQA_REP = 4
ATT_WIDTH = 1024
KV_WIDTH = 256
MOBA_BLOCK = 256
MOBA_TOPK = 3
ROT_DIM = 16
ROPE_THETA = 500000.0
MEM_HEADS = 4
MEM_HEAD_DIM = 256
MEM_WIDTH = 1024
D_FF = 2816
FFN_CONV = 3
RMS_EPS = 1e-6
NEG_INF = -1e30
PAGE_SIZE = 128

LANES = 128
SUBLANES = 8
VMEM_LIMIT = 56 * 1024 * 1024

QKVD_WIDTH = ATT_WIDTH + 2 * KV_WIDTH + LANES
DT_COL_BLOCK = (ATT_WIDTH + 2 * KV_WIDTH) // LANES


def _cparams(sem):
    return pltpu.CompilerParams(dimension_semantics=sem, vmem_limit_bytes=VMEM_LIMIT)


def _dot(a, b):
    return jnp.dot(a, b, preferred_element_type=F32)


def _dot_nt(a, b):
    return lax.dot_general(a, b, (((1,), (1,)), ((), ())), preferred_element_type=F32)


def _dot_f32(a, b):
    return jnp.dot(a, b, preferred_element_type=F32, precision=lax.Precision.HIGHEST)


def _silu(x):
    return x * jax.nn.sigmoid(x)


def _norm_matmul_kernel(x_ref, nw_ref, w_ref, o_ref, xn_ref, *, act):
    @pl.when(pl.program_id(1) == 0)
    def _():
        x = x_ref[...]
        ms = jnp.mean(x * x, axis=-1, keepdims=True)
        xn_ref[...] = (x * lax.rsqrt(ms + RMS_EPS) * nw_ref[...]).astype(BF16)

    y = _dot(xn_ref[...], w_ref[...])
    if act == "sigmoid":
        y = jax.nn.sigmoid(y)
    o_ref[...] = y.astype(o_ref.dtype)


def _norm_matmul(x, nw, w, *, tn, out_dtype=F32, act=None, name="norm_matmul"):
    t, d = x.shape
    n = w.shape[1]
    tm = min(t, 1024)
    assert t % tm == 0 and n % tn == 0
    return pl.pallas_call(
        functools.partial(_norm_matmul_kernel, act=act),
        grid=(t // tm, n // tn),
        in_specs=[pl.BlockSpec((tm, d), lambda i, j: (i, 0)),
                  pl.BlockSpec((1, d), lambda i, j: (0, 0)),
                  pl.BlockSpec((d, tn), lambda i, j: (0, j))],
        out_specs=pl.BlockSpec((tm, tn), lambda i, j: (i, j)),
        out_shape=jax.ShapeDtypeStruct((t, n), out_dtype),
        scratch_shapes=[pltpu.VMEM((tm, d), BF16)],
        compiler_params=_cparams(("parallel", "arbitrary")),
        name=name,
    )(x, nw.reshape(1, d), w)


def _shift_rows(x, k, halo):
    r = pltpu.roll(x, k, axis=0)
    row = lax.broadcasted_iota(jnp.int32, x.shape, 0)
    for j in range(k):
        src = SUBLANES - k + j
        r = jnp.where(row == j, halo[src:src + 1, :], r)
    return r


def _ssd_kernel(z_ref, xa_ref, xb_ref, dt_ref, cst_ref, s0_ref, cw_ref, cb_ref, dtb_ref, alog_ref,
                dsk_ref, nw_ref, y_ref, sf_ref, s_ref, carry_ref, ybuf_ref, *, qb, q, nchunks):
    c = pl.program_id(1)

    @pl.when(c == 0)
    def _():
        s_ref[...] = s0_ref[0]
        carry_ref[...] = cst_ref[0]

    pre = jnp.concatenate([xa_ref[0], xb_ref[0]], axis=1)
    halo = carry_ref[...]
    acc = _shift_rows(pre, 3, halo) * cw_ref[0:1, :]
    acc = acc + _shift_rows(pre, 2, halo) * cw_ref[1:2, :]
    acc = acc + _shift_rows(pre, 1, halo) * cw_ref[2:3, :]
    acc = acc + pre * cw_ref[3:4, :]
    xbc = _silu(acc + cb_ref[...])
    if nchunks > 1:
        carry_ref[...] = pre[qb - SUBLANES:qb, :]

    z = z_ref[0]
    dt_raw = dt_ref[0]
    if qb < q:
        xbc = jnp.concatenate([xbc, jnp.zeros((q - qb, CONV_DIM), F32)], axis=0)
        z = jnp.concatenate([z, jnp.zeros((q - qb, D_INNER), F32)], axis=0)
        dt_raw = jnp.concatenate([dt_raw, jnp.zeros((q - qb, LANES), F32)], axis=0)

    xs = xbc[:, :D_INNER]
    xs_bf = xs.astype(BF16)
    xs_t = xs.T

    v = dt_raw + dtb_ref[...]
    dt = jnp.maximum(v, 0.0) + jnp.log1p(jnp.exp(-jnp.abs(v)))
    row_q = lax.broadcasted_iota(jnp.int32, (q, LANES), 0)
    if qb < q:
        dt = jnp.where(row_q < qb, dt, 0.0)
    a = -jnp.exp(alog_ref[...])
    ri = lax.broadcasted_iota(jnp.int32, (q, q), 0)
    ci = lax.broadcasted_iota(jnp.int32, (q, q), 1)
    causal = ci <= ri
    cs = _dot_f32(causal.astype(F32), dt * a)
    cs_t = cs.T
    dt_t = dt.T
    ecs = jnp.exp(cs)

    for g in range(SSD_GROUPS):
        b_g = xbc[:, D_INNER + g * SSD_STATE:D_INNER + (g + 1) * SSD_STATE].astype(BF16)
        c_g = xbc[:, D_INNER + SSD_GROUPS * SSD_STATE + g * SSD_STATE:
                  D_INNER + SSD_GROUPS * SSD_STATE + (g + 1) * SSD_STATE].astype(BF16)
        cb = _dot_nt(c_g, b_g)
        for r in range(SSD_HEADS // SSD_GROUPS):
            h = g * (SSD_HEADS // SSD_GROUPS) + r
            lo = h * SSD_HEAD_DIM
            cs_col = cs[:, h:h + 1]
            cs_row = cs_t[h:h + 1, :]
            dt_row = dt_t[h:h + 1, :]
            seg = jnp.where(causal, cs_col - cs_row, -jnp.inf)
            m_h = (cb * jnp.exp(seg) * dt_row).astype(BF16)
            s_h = s_ref[h]
            y_h = _dot(m_h, xs_bf[:, lo:lo + SSD_HEAD_DIM])
            y_h = y_h + _dot_nt(c_g, s_h.astype(BF16)) * ecs[:, h:h + 1]
            ybuf_ref[:, lo:lo + SSD_HEAD_DIM] = y_h
            cs_end = cs_row[:, q - 1:q]
            w_row = dt_row * jnp.exp(cs_end - cs_row)
            xw = (xs_t[lo:lo + SSD_HEAD_DIM, :] * w_row).astype(BF16)
            s_ref[h] = s_h * jnp.exp(cs_end) + _dot(xw, b_g)

    y = ybuf_ref[...] + dsk_ref[...] * xs
    gt = y * _silu(z)
    gw = D_INNER // SSD_GROUPS
    parts = []
    for g in range(SSD_GROUPS):
        gg = gt[:, g * gw:(g + 1) * gw]
        parts.append(gg * lax.rsqrt(jnp.mean(gg * gg, axis=-1, keepdims=True) + RMS_EPS))
    yn = jnp.concatenate(parts, axis=1) * nw_ref[...]
    y_ref[0] = yn[:qb].astype(y_ref.dtype)

    @pl.when(c == nchunks - 1)
    def _():
        sf_ref[0] = s_ref[...]


def _ssd(zx3, qkvd3, conv_state8, ssm0, cw, cb, dtb, alog, dsk, nw):
    nb, l, _ = zx3.shape
    q = SSD_CHUNK
    qb = min(l, q)
    assert l % qb == 0 and qb % SUBLANES == 0
    nchunks = l // qb
    half = CONV_DIM // 2
    row_blk = lambda col: (lambda b, c: (b, c, col))
    full2 = lambda b, c: (0, 0)
    return pl.pallas_call(
        functools.partial(_ssd_kernel, qb=qb, q=q, nchunks=nchunks),
        grid=(nb, nchunks),
        in_specs=[pl.BlockSpec((1, qb, D_INNER), row_blk(0)),
                  pl.BlockSpec((1, qb, half), row_blk(1)),
                  pl.BlockSpec((1, qb, half), row_blk(2)),
                  pl.BlockSpec((1, qb, LANES), row_blk(DT_COL_BLOCK)),
                  pl.BlockSpec((1, SUBLANES, CONV_DIM), lambda b, c: (b, 0, 0)),
                  pl.BlockSpec((1, SSD_HEADS, SSD_HEAD_DIM, SSD_STATE), lambda b, c: (b, 0, 0, 0)),
                  pl.BlockSpec((SSD_CONV, CONV_DIM), full2),
                  pl.BlockSpec((1, CONV_DIM), full2),
                  pl.BlockSpec((1, LANES), full2),
                  pl.BlockSpec((1, LANES), full2),
                  pl.BlockSpec((1, D_INNER), full2),
                  pl.BlockSpec((1, D_INNER), full2)],
        out_specs=[pl.BlockSpec((1, qb, D_INNER), lambda b, c: (b, c, 0)),
                   pl.BlockSpec((1, SSD_HEADS, SSD_HEAD_DIM, SSD_STATE), lambda b, c: (b, 0, 0, 0))],
        out_shape=[jax.ShapeDtypeStruct((nb, l, D_INNER), BF16),
                   jax.ShapeDtypeStruct((nb, SSD_HEADS, SSD_HEAD_DIM, SSD_STATE), F32)],
        scratch_shapes=[pltpu.VMEM((SSD_HEADS, SSD_HEAD_DIM, SSD_STATE), F32),
                        pltpu.VMEM((SUBLANES, CONV_DIM), F32),
                        pltpu.VMEM((q, D_INNER), F32)],
        compiler_params=_cparams(("parallel", "arbitrary")),
        name="ssd",
    )(zx3, zx3, zx3, qkvd3, conv_state8, ssm0, cw, cb, dtb, alog, dsk, nw)


def _rope_tables(pos):
    half = ROT_DIM // 2
    inv = ROPE_THETA ** (-jnp.arange(half, dtype=F32) * 2.0 / ROT_DIM)
    ang = pos.astype(F32)[:, None] * inv[None, :]
    cos, sin = jnp.cos(ang), jnp.sin(ang)
    n = pos.shape[0]
    pad = jnp.zeros((n, ATT_HEAD_DIM - ROT_DIM), F32)
    zero = jnp.zeros((n, half), F32)
    c_head = jnp.concatenate([cos, cos, pad + 1.0], axis=1)
    s1_head = jnp.concatenate([-sin, zero, pad], axis=1)
    s2_head = jnp.concatenate([zero, sin, pad], axis=1)
    rep = LANES // ATT_HEAD_DIM
    return jnp.tile(c_head, (1, rep)), jnp.tile(s1_head, (1, rep)), jnp.tile(s2_head, (1, rep))


def _rope_group(xg, c, s1, s2):
    half = ROT_DIM // 2
    return xg * c + pltpu.roll(xg, LANES - half, axis=1) * s1 + pltpu.roll(xg, half, axis=1) * s2


def _head_from_group(xg, odd, lane):
    if odd:
        xg = pltpu.roll(xg, ATT_HEAD_DIM, axis=1)
    return jnp.where(lane < ATT_HEAD_DIM, xg, 0.0)


def _prompt_prep_kernel(q_ref, k_ref, v_ref, c_ref, s1_ref, s2_ref,
                        krot_ref, qhm_ref, kaug_ref, vhm_ref, kmean_ref):
    i = pl.program_id(1)
    rows = q_ref.shape[1]
    c, s1, s2 = c_ref[...], s1_ref[...], s2_ref[...]
    lane = lax.broadcasted_iota(jnp.int32, (rows, LANES), 1)
    scale = ATT_HEAD_DIM ** -0.5

    @pl.when(i == 0)
    def _():
        kmean_ref[...] = jnp.zeros_like(kmean_ref)

    for cg in range(ATT_WIDTH // LANES):
        qg = _rope_group(q_ref[0, :, cg * LANES:(cg + 1) * LANES], c, s1, s2) * scale
        for odd in range(2):
            qhm_ref[0, 2 * cg + odd] = _head_from_group(qg, odd, lane).astype(BF16)

    onehot = jnp.where(lane == ATT_HEAD_DIM + i, 1.0, 0.0)
    for cg in range(KV_WIDTH // LANES):
        kg = _rope_group(k_ref[0, :, cg * LANES:(cg + 1) * LANES], c, s1, s2)
        krot_ref[0, :, cg * LANES:(cg + 1) * LANES] = kg
        vg = v_ref[0, :, cg * LANES:(cg + 1) * LANES]
        for odd in range(2):
            g = 2 * cg + odd
            kh = _head_from_group(kg, odd, lane)
            kaug_ref[0, g] = (kh + onehot).astype(BF16)
            vhm_ref[0, g] = _head_from_group(vg, odd, lane).astype(BF16)
            kmean_ref[0, g, pl.ds(i, 1), :] = jnp.mean(kh, axis=0, keepdims=True)


def _prompt_prep(qkvd3, tables):
    b, s, _ = qkvd3.shape
    nblk = s // MOBA_BLOCK
    assert s % MOBA_BLOCK == 0 and nblk <= SUBLANES
    blk = MOBA_BLOCK
    tab = pl.BlockSpec((blk, LANES), lambda bi, i: (i, 0))
    return pl.pallas_call(
        _prompt_prep_kernel,
        grid=(b, nblk),
        in_specs=[pl.BlockSpec((1, blk, ATT_WIDTH), lambda bi, i: (bi, i, 0)),
                  pl.BlockSpec((1, blk, KV_WIDTH), lambda bi, i: (bi, i, ATT_WIDTH // KV_WIDTH)),
                  pl.BlockSpec((1, blk, KV_WIDTH), lambda bi, i: (bi, i, ATT_WIDTH // KV_WIDTH + 1)),
                  tab, tab, tab],
        out_specs=[pl.BlockSpec((1, blk, KV_WIDTH), lambda bi, i: (bi, i, 0)),
                   pl.BlockSpec((1, ATT_HEADS, blk, LANES), lambda bi, i: (bi, 0, i, 0)),
                   pl.BlockSpec((1, KV_HEADS, blk, LANES), lambda bi, i: (bi, 0, i, 0)),
                   pl.BlockSpec((1, KV_HEADS, blk, LANES), lambda bi, i: (bi, 0, i, 0)),
                   pl.BlockSpec((1, KV_HEADS, SUBLANES, LANES), lambda bi, i: (bi, 0, 0, 0))],
        out_shape=[jax.ShapeDtypeStruct((b, s, KV_WIDTH), F32),
                   jax.ShapeDtypeStruct((b, ATT_HEADS, s, LANES), BF16),
                   jax.ShapeDtypeStruct((b, KV_HEADS, s, LANES), BF16),
                   jax.ShapeDtypeStruct((b, KV_HEADS, s, LANES), BF16),
                   jax.ShapeDtypeStruct((b, KV_HEADS, SUBLANES, LANES), F32)],
        compiler_params=_cparams(("parallel", "arbitrary")),
        name="prompt_prep",
    )(qkvd3, qkvd3, qkvd3, *tables)


def _topk_lanes(gate, valid, lane_f):
    g0 = jnp.where(valid, gate, -jnp.inf)
    sel = jnp.zeros(gate.shape, dtype=jnp.bool_)
    for _ in range(MOBA_TOPK):
        m = jnp.max(g0, axis=1, keepdims=True)
        idx = jnp.min(jnp.where(g0 == m, lane_f, 1e9), axis=1, keepdims=True)
        pick = jnp.logical_and(lane_f == idx, m > -jnp.inf)
        sel = jnp.logical_or(sel, pick)
        g0 = jnp.where(pick, -jnp.inf, g0)
    return sel


def _moba_prompt_kernel(q_ref, k_ref, v_ref, km_ref, y_ref):
    i = pl.program_id(2)
    blk = MOBA_BLOCK
    k_all = k_ref[0, 0]
    v_all = v_ref[0, 0]
    off = pl.multiple_of(i * blk, blk)
    k_own = k_ref[0, 0, pl.ds(off, blk), :]
    v_own = v_ref[0, 0, pl.ds(off, blk), :]
    kmt = jnp.concatenate([jnp.zeros((ATT_HEAD_DIM, LANES), F32), km_ref[0, 0],
                           jnp.zeros((LANES - ATT_HEAD_DIM - SUBLANES, LANES), F32)], axis=0).astype(BF16)
    lane = lax.broadcasted_iota(jnp.int32, (blk, LANES), 1)
    lane_f = lane.astype(F32)
    slot = jnp.logical_and(lane >= ATT_HEAD_DIM, lane < ATT_HEAD_DIM + SUBLANES)
    valid = jnp.logical_and(lane >= ATT_HEAD_DIM, lane < ATT_HEAD_DIM + i)
    ri = lax.broadcasted_iota(jnp.int32, (blk, blk), 0)
    ci = lax.broadcasted_iota(jnp.int32, (blk, blk), 1)
    own_ok = ci <= ri

    for r in range(GQA_REP):
        qh = q_ref[0, r]
        gate = _dot_nt(qh, kmt)
        sel = _topk_lanes(gate, valid, lane_f)
        bias = jnp.where(slot, jnp.where(sel, 0.0, NEG_INF), 0.0)
        q_aug = (qh.astype(F32) + bias).astype(BF16)
        s_past = _dot_nt(q_aug, k_all)
        s_own = jnp.where(own_ok, _dot_nt(qh, k_own), NEG_INF)
        m = jnp.maximum(jnp.max(s_past, axis=1, keepdims=True), jnp.max(s_own, axis=1, keepdims=True))
        p_past = jnp.exp(s_past - m)
        p_own = jnp.exp(s_own - m)
        den = jnp.sum(p_past, axis=1, keepdims=True) + jnp.sum(p_own, axis=1, keepdims=True)
        o = _dot(p_past.astype(BF16), v_all) + _dot(p_own.astype(BF16), v_own)
        o = o / den
        y_ref[:, r * ATT_HEAD_DIM:(r + 1) * ATT_HEAD_DIM] = o[:, :ATT_HEAD_DIM].astype(y_ref.dtype)


def _moba_prompt(q_hm, k_aug, v_hm, kmean):
    b, _, s, _ = q_hm.shape
    nblk = s // MOBA_BLOCK
    blk = MOBA_BLOCK
    return pl.pallas_call(
        _moba_prompt_kernel,
        grid=(b, KV_HEADS, nblk),
        in_specs=[pl.BlockSpec((1, GQA_REP, blk, LANES), lambda bi, g, i: (bi, g, i, 0)),
                  pl.BlockSpec((1, 1, s, LANES), lambda bi, g, i: (bi, g, 0, 0)),
                  pl.BlockSpec((1, 1, s, LANES), lambda bi, g, i: (bi, g, 0, 0)),
                  pl.BlockSpec((1, 1, SUBLANES, LANES), lambda bi, g, i: (bi, g, 0, 0))],
        out_specs=pl.BlockSpec((blk, GQA_REP * ATT_HEAD_DIM), lambda bi, g, i: (bi * nblk + i, g)),
        out_shape=jax.ShapeDtypeStruct((b * s, ATT_WIDTH), BF16),
        compiler_params=_cparams(("parallel", "parallel", "arbitrary")),
        name="moba_prompt",
    )(q_hm, k_aug, v_hm, kmean)


def _rope_kernel(x_ref, c_ref, s1_ref, s2_ref, o_ref, *, q_groups):
    c, s1, s2 = c_ref[...], s1_ref[...], s2_ref[...]
    scale = ATT_HEAD_DIM ** -0.5
    for cg in range(x_ref.shape[1] // LANES):
        xg = _rope_group(x_ref[:, cg * LANES:(cg + 1) * LANES], c, s1, s2)
        if cg < q_groups:
            xg = xg * scale
        o_ref[:, cg * LANES:(cg + 1) * LANES] = xg


def _rope_sample(qkvd, tables):
    t = qkvd.shape[0]
    w = ATT_WIDTH + KV_WIDTH
    tab = pl.BlockSpec((t, LANES), lambda i: (0, 0))
    return pl.pallas_call(
        functools.partial(_rope_kernel, q_groups=ATT_WIDTH // LANES),
        grid=(1,),
        in_specs=[pl.BlockSpec((t, w), lambda i: (0, 0)), tab, tab, tab],
        out_specs=pl.BlockSpec((t, w), lambda i: (0, 0)),
        out_shape=jax.ShapeDtypeStruct((t, w), F32),
        compiler_params=_cparams(("arbitrary",)),
        name="rope_sample",
    )(qkvd, *tables)


def _moba_sample_kernel(pt_ref, qx_ref, kn_ref, vn_ref, e_ref, ck_ref, cv_ref, y_ref,
                        kbuf, vbuf, sem, *, npages, t):
    s = pl.program_id(0)
    ns = pl.num_programs(0)
    slot = s % 2

    def k_copy(seq, p, sl):
        return pltpu.make_async_copy(ck_ref.at[pt_ref[seq, p]], kbuf.at[sl, p], sem.at[0, sl])

    def v_copy(seq, p, sl):
        return pltpu.make_async_copy(cv_ref.at[pt_ref[seq, p]], vbuf.at[sl, p], sem.at[1, sl])

    def start_all(seq, sl):
        def body(p, carry):
            k_copy(seq, p, sl).start()
            v_copy(seq, p, sl).start()
            return carry
        lax.fori_loop(0, npages, body, 0)

    @pl.when(s == 0)
    def _():
        start_all(0, 0)

    @pl.when(s + 1 < ns)
    def _():
        start_all(s + 1, 1 - slot)

    def wait_body(p, carry):
        k_copy(s, p, slot).wait()
        v_copy(s, p, slot).wait()
        return carry
    lax.fori_loop(0, npages, wait_body, 0)

    past = npages * PAGE_SIZE
    nblk = past // MOBA_BLOCK
    rows = ATT_HEADS * t
    kf = kbuf[slot].reshape(past, KV_WIDTH)
    vf = vbuf[slot].reshape(past, KV_WIDTH)
    qx = qx_ref[0]

    kmean = jnp.mean(kf.reshape(nblk, MOBA_BLOCK, KV_WIDTH), axis=1)
    kmean = jnp.concatenate([kmean, jnp.zeros((LANES - nblk, KV_WIDTH), F32)], axis=0).astype(BF16)
    gate = _dot_nt(qx, kmean)
    lane = lax.broadcasted_iota(jnp.int32, (rows, LANES), 1)
    sel = _topk_lanes(gate, lane < nblk, lane.astype(F32))
    selb = jnp.where(sel, 0.0, NEG_INF).astype(BF16)
    s_past = _dot_nt(qx, kf.astype(BF16)) + _dot(selb, e_ref[...])

    kn = jnp.concatenate([kn_ref[0], jnp.zeros((LANES - t, KV_WIDTH), F32)], axis=0).astype(BF16)
    vn = jnp.concatenate([vn_ref[0], jnp.zeros((LANES - t, KV_WIDTH), F32)], axis=0).astype(BF16)
    row = lax.broadcasted_iota(jnp.int32, (rows, LANES), 0)
    own_ok = lane <= (row % t)
    s_own = jnp.where(own_ok, _dot_nt(qx, kn), NEG_INF)

    m = jnp.maximum(jnp.max(s_past, axis=1, keepdims=True), jnp.max(s_own, axis=1, keepdims=True))
    p_past = jnp.exp(s_past - m)
    p_own = jnp.exp(s_own - m)
    den = jnp.sum(p_past, axis=1, keepdims=True) + jnp.sum(p_own, axis=1, keepdims=True)
    o = _dot(p_past.astype(BF16), vf.astype(BF16)) + _dot(p_own.astype(BF16), vn)
    o = o / den
    for h in range(ATT_HEADS):
        g = h // GQA_REP
        y_ref[0, :, h * ATT_HEAD_DIM:(h + 1) * ATT_HEAD_DIM] = (
            o[h * t:(h + 1) * t, g * ATT_HEAD_DIM:(g + 1) * ATT_HEAD_DIM].astype(y_ref.dtype))


def _moba_sample(page_table, q_exp, k_new, v_new, e_mat, cache_k, cache_v):
    ns, npages = page_table.shape
    t = k_new.shape[1]
    past = npages * PAGE_SIZE
    assert past % MOBA_BLOCK == 0 and past // MOBA_BLOCK <= LANES and t == SUBLANES
    rows = ATT_HEADS * t
    grid_spec = pltpu.PrefetchScalarGridSpec(
        num_scalar_prefetch=1,
        grid=(ns,),
        in_specs=[pl.BlockSpec((1, rows, KV_WIDTH), lambda s, pt: (s, 0, 0)),
                  pl.BlockSpec((1, t, KV_WIDTH), lambda s, pt: (s, 0, 0)),
                  pl.BlockSpec((1, t, KV_WIDTH), lambda s, pt: (s, 0, 0)),
                  pl.BlockSpec((LANES, past), lambda s, pt: (0, 0)),
                  pl.BlockSpec(memory_space=pl.ANY),
                  pl.BlockSpec(memory_space=pl.ANY)],
        out_specs=pl.BlockSpec((1, t, ATT_WIDTH), lambda s, pt: (s, 0, 0)),
        scratch_shapes=[pltpu.VMEM((2, npages, PAGE_SIZE, KV_WIDTH), F32),
                        pltpu.VMEM((2, npages, PAGE_SIZE, KV_WIDTH), F32),
                        pltpu.SemaphoreType.DMA((2, 2))],
    )
    return pl.pallas_call(
        functools.partial(_moba_sample_kernel, npages=npages, t=t),
        grid_spec=grid_spec,
        out_shape=jax.ShapeDtypeStruct((ns, t, ATT_WIDTH), BF16),
        compiler_params=_cparams(("arbitrary",)),
        name="moba_sample",
    )(page_table, q_exp, k_new, v_new, e_mat, cache_k, cache_v)


def _mem_attn_kernel(q_ref, mk_ref, mv_ref, y_ref):
    scale = MEM_HEAD_DIM ** -0.5
    for h in range(MEM_HEADS):
        lo, hi = h * MEM_HEAD_DIM, (h + 1) * MEM_HEAD_DIM
        q = (q_ref[0, :, lo:hi].astype(F32) * scale).astype(BF16)
        s = _dot_nt(q, mk_ref[0, :, lo:hi].astype(BF16))
        m = jnp.max(s, axis=1, keepdims=True)
        p = jnp.exp(s - m)
        den = jnp.sum(p, axis=1, keepdims=True)
        o = _dot(p.astype(BF16), mv_ref[0, :, lo:hi].astype(BF16)) / den
        y_ref[0, :, lo:hi] = o.astype(y_ref.dtype)


def _mem_attn(qm3, mk3, mv3):
    nb, l, _ = qm3.shape
    m = mk3.shape[1]
    tl = min(l, 512)
    assert l % tl == 0
    return pl.pallas_call(
        _mem_attn_kernel,
        grid=(nb, l // tl),
        in_specs=[pl.BlockSpec((1, tl, MEM_WIDTH), lambda b, i: (b, i, 0)),
                  pl.BlockSpec((1, m, MEM_WIDTH), lambda b, i: (b, 0, 0)),
                  pl.BlockSpec((1, m, MEM_WIDTH), lambda b, i: (b, 0, 0))],
        out_specs=pl.BlockSpec((1, tl, MEM_WIDTH), lambda b, i: (b, i, 0)),
        out_shape=jax.ShapeDtypeStruct((nb, l, MEM_WIDTH), BF16),
        compiler_params=_cparams(("parallel", "arbitrary")),
        name="mem_attn",
    )(qm3, mk3, mv3)


def _merge_kernel(x_ref, gs_ref, ys_ref, ya_ref, ym_ref, ws_ref, wa_ref, wm_ref, wo_ref, h_ref):
    d = D_MODEL
    merged = gs_ref[:, 0:d] * _dot(ys_ref[...], ws_ref[...])
    merged = merged + gs_ref[:, d:2 * d] * _dot(ya_ref[...], wa_ref[...])
    merged = merged + gs_ref[:, 2 * d:3 * d] * _dot(ym_ref[...], wm_ref[...])
    h_ref[...] = x_ref[...] + _dot(merged.astype(BF16), wo_ref[...])


def _merge(x, gs, y_ssd, y_att, y_mem, ws, wa, wm, wo):
    t = x.shape[0]
    tm = min(t, 512)
    assert t % tm == 0
    rows = lambda w: pl.BlockSpec((tm, w), lambda i: (i, 0))
    full = lambda a: pl.BlockSpec(a.shape, lambda i: (0, 0))
    return pl.pallas_call(
        _merge_kernel,
        grid=(t // tm,),
        in_specs=[rows(D_MODEL), rows(3 * D_MODEL), rows(D_INNER), rows(ATT_WIDTH), rows(MEM_WIDTH),
                  full(ws), full(wa), full(wm), full(wo)],
        out_specs=rows(D_MODEL),
        out_shape=jax.ShapeDtypeStruct((t, D_MODEL), F32),
        compiler_params=_cparams(("parallel",)),
        name="merge",
    )(x, gs, y_ssd, y_att, y_mem, ws, wa, wm, wo)


def _ffn_gate_kernel(u_ref, st_ref, w_ref, b_ref, a_ref, carry_ref, *, tl, ntiles):
    i = pl.program_id(1)

    @pl.when(i == 0)
    def _():
        carry_ref[...] = st_ref[0]

    u = u_ref[0]
    halo = carry_ref[...]
    acc = _shift_rows(u, 2, halo) * w_ref[0:1, :]
    acc = acc + _shift_rows(u, 1, halo) * w_ref[1:2, :]
    acc = acc + u * w_ref[2:3, :]
    uc = acc + b_ref[...]
    if ntiles > 1:
        carry_ref[...] = u[tl - SUBLANES:tl, :]
    a_ref[0] = (_silu(uc[:, :D_FF]) * uc[:, D_FF:]).astype(a_ref.dtype)


def _ffn_gate(u3, state8, w, b):
    nb, l, c = u3.shape
    tl = min(l, 256)
    assert l % tl == 0 and tl % SUBLANES == 0
    ntiles = l // tl
    return pl.pallas_call(
        functools.partial(_ffn_gate_kernel, tl=tl, ntiles=ntiles),
        grid=(nb, ntiles),
        in_specs=[pl.BlockSpec((1, tl, c), lambda bi, i: (bi, i, 0)),
                  pl.BlockSpec((1, SUBLANES, c), lambda bi, i: (bi, 0, 0)),
                  pl.BlockSpec((FFN_CONV, c), lambda bi, i: (0, 0)),
                  pl.BlockSpec((1, c), lambda bi, i: (0, 0))],
        out_specs=pl.BlockSpec((1, tl, D_FF), lambda bi, i: (bi, i, 0)),
        out_shape=jax.ShapeDtypeStruct((nb, l, D_FF), BF16),
        scratch_shapes=[pltpu.VMEM((SUBLANES, c), F32)],
        compiler_params=_cparams(("parallel", "arbitrary")),
        name="ffn_gate",
    )(u3, state8, w, b)


def _down_kernel(a_ref, w_ref, h_ref, nw_ref, y_ref):
    h = h_ref[...] + _dot(a_ref[...], w_ref[...])
    ms = jnp.mean(h * h, axis=-1, keepdims=True)
    y_ref[...] = h * lax.rsqrt(ms + RMS_EPS) * nw_ref[...]


def _down(act, w, h, nw):
    t = h.shape[0]
    tm = min(t, 512)
    assert t % tm == 0
    return pl.pallas_call(
        _down_kernel,
        grid=(t // tm,),
        in_specs=[pl.BlockSpec((tm, D_FF), lambda i: (i, 0)),
                  pl.BlockSpec((D_FF, D_MODEL), lambda i: (0, 0)),
                  pl.BlockSpec((tm, D_MODEL), lambda i: (i, 0)),
                  pl.BlockSpec((1, D_MODEL), lambda i: (0, 0))],
        out_specs=pl.BlockSpec((tm, D_MODEL), lambda i: (i, 0)),
        out_shape=jax.ShapeDtypeStruct((t, D_MODEL), F32),
        compiler_params=_cparams(("parallel",)),
        name="ffn_down",
    )(act, w, h, nw.reshape(1, D_MODEL))


def _pad_state_rows(state):
    nb, k, c = state.shape
    return jnp.concatenate([jnp.zeros((nb, SUBLANES - k, c), state.dtype), state], axis=1)


def _layer(x3, mem_k3, mem_v3, ssm0, conv_state, ffn_state, attend, p):
    nb, l, d = x3.shape
    t = nb * l
    x = x3.reshape(t, d)
    nw1 = p["norm1_w"]
    zx = _norm_matmul(x, nw1, p["w_zx"], tn=2048, name="proj_zx")
    qkvd = _norm_matmul(x, nw1, p["w_qkvd"], tn=QKVD_WIDTH, name="proj_qkvd")
    qm = _norm_matmul(x, nw1, p["w_qm"], tn=MEM_WIDTH, out_dtype=BF16, name="proj_qm")
    gs = _norm_matmul(x, nw1, p["w_gates"], tn=D_MODEL, act="sigmoid", name="proj_gates")

    zx3 = zx.reshape(nb, l, D_INNER + CONV_DIM)
    qkvd3 = qkvd.reshape(nb, l, QKVD_WIDTH)
    y_ssd, ssm_new = _ssd(zx3, qkvd3, _pad_state_rows(conv_state), ssm0, p["ssd_conv_w"], p["ssd_conv_b"],
                          p["dt_bias"], p["a_log"], p["d_skip"], p["ssd_norm_w"])
    conv_new = zx3[:, l - (SSD_CONV - 1):, D_INNER:]

    y_att, k_new, v_new = attend(qkvd3)
    y_mem = _mem_attn(qm.reshape(nb, l, MEM_WIDTH), mem_k3, mem_v3)

    h = _merge(x, gs, y_ssd.reshape(t, D_INNER), y_att.reshape(t, ATT_WIDTH), y_mem.reshape(t, MEM_WIDTH),
               p["w_ssd_out"], p["w_attn_out"], p["w_mem_out"], p["w_o"])

    u = _norm_matmul(h, p["norm2_w"], p["w_up"], tn=2 * D_FF // 11, name="ffn_up")
    u3 = u.reshape(nb, l, 2 * D_FF)
    ffn_new = u3[:, l - (FFN_CONV - 1):]
    act = _ffn_gate(u3, _pad_state_rows(ffn_state), p["ffn_conv_w"], p["ffn_conv_b"])
    y = _down(act.reshape(t, D_FF), p["w_down"], h, p["final_norm_w"])
    return y.reshape(nb, l, d), k_new, v_new, ssm_new, conv_new, ffn_new


def _attend_prompt(qkvd3):
    b, s, _ = qkvd3.shape
    tables = _rope_tables(jnp.arange(s, dtype=jnp.int32))
    k_rot, q_hm, k_aug, v_hm, kmean = _prompt_prep(qkvd3, tables)
    y_att = _moba_prompt(q_hm, k_aug, v_hm, kmean)
    v = qkvd3[:, :, ATT_WIDTH + KV_WIDTH:ATT_WIDTH + 2 * KV_WIDTH]
    return y_att, k_rot.reshape(b, s, KV_HEADS, ATT_HEAD_DIM), v.reshape(b, s, KV_HEADS, ATT_HEAD_DIM)


def _attend_sample(qkvd3, cache_k, cache_v, page_table):
    ns, t, _ = qkvd3.shape
    npages = page_table.shape[1]
    past = npages * PAGE_SIZE
    pos = past + jnp.arange(t, dtype=jnp.int32)
    tables = tuple(jnp.tile(tb, (ns, 1)) for tb in _rope_tables(pos))
    qk = _rope_sample(qkvd3.reshape(ns * t, QKVD_WIDTH), tables)
    q_rot = qk[:, :ATT_WIDTH].reshape(ns, t, KV_HEADS, GQA_REP, ATT_HEAD_DIM)
    k_rot = qk[:, ATT_WIDTH:].reshape(ns, t, KV_WIDTH)
    v = qkvd3[:, :, ATT_WIDTH + KV_WIDTH:ATT_WIDTH + 2 * KV_WIDTH]
    q_ht = q_rot.transpose(0, 2, 3, 1, 4)
    eye = jnp.eye(KV_HEADS, dtype=F32)
    q_exp = (q_ht[:, :, :, :, None, :] * eye[None, :, None, None, :, None]).reshape(ns, ATT_HEADS * t, KV_WIDTH)
    blk_of_key = jnp.arange(past, dtype=jnp.int32) // MOBA_BLOCK
    e_mat = (jnp.arange(LANES, dtype=jnp.int32)[:, None] == blk_of_key[None, :]).astype(BF16)
    n_phys = cache_k.shape[0]
    y_att = _moba_sample(page_table, q_exp.astype(BF16), k_rot, v, e_mat,
                         cache_k.reshape(n_phys, PAGE_SIZE, KV_WIDTH), cache_v.reshape(n_phys, PAGE_SIZE, KV_WIDTH))
    return y_att, k_rot.reshape(ns, t, KV_HEADS, ATT_HEAD_DIM), v.reshape(ns, t, KV_HEADS, ATT_HEAD_DIM)


def _layer_params(l, norm1_w, w_in, ssd_conv_w, ssd_conv_b, dt_bias, a_log, d_skip, ssd_norm_w, w_ssd_out,
                  w_attn_out, w_mem_out, w_o, norm2_w, w_up, ffn_conv_w, ffn_conv_b, w_down, final_norm_w):
    w = w_in[l]
    o_z, o_x, o_dt = 0, D_INNER, D_INNER + CONV_DIM
    o_q = o_dt + SSD_HEADS
    o_k, o_v = o_q + ATT_WIDTH, o_q + ATT_WIDTH + KV_WIDTH
    o_qm = o_v + KV_WIDTH
    o_g = o_qm + MEM_WIDTH
    pad_lanes = lambda a: jnp.pad(a, (0, LANES - a.shape[0])).reshape(1, LANES)
    w_dt = jnp.pad(w[:, o_dt:o_q], ((0, 0), (0, LANES - SSD_HEADS)))
    return {
        "norm1_w": norm1_w[l],
        "w_zx": w[:, o_z:o_dt].astype(BF16),
        "w_qkvd": jnp.concatenate([w[:, o_q:o_qm], w_dt], axis=1).astype(BF16),
        "w_qm": w[:, o_qm:o_g].astype(BF16),
        "w_gates": w[:, o_g:].astype(BF16),
        "ssd_conv_w": ssd_conv_w[l],
        "ssd_conv_b": ssd_conv_b[l].reshape(1, CONV_DIM),
        "dt_bias": pad_lanes(dt_bias[l]),
        "a_log": pad_lanes(a_log[l]),
        "d_skip": jnp.repeat(d_skip[l], SSD_HEAD_DIM).reshape(1, D_INNER),
        "ssd_norm_w": ssd_norm_w[l].reshape(1, D_INNER),
        "w_ssd_out": w_ssd_out[l].astype(BF16),
        "w_attn_out": w_attn_out[l].astype(BF16),
        "w_mem_out": w_mem_out[l].astype(BF16),
        "w_o": w_o[l].astype(BF16),
        "norm2_w": norm2_w[l],
        "w_up": w_up[l].astype(BF16),
        "ffn_conv_w": ffn_conv_w[l],
        "ffn_conv_b": ffn_conv_b[l].reshape(1, 2 * D_FF),
        "w_down": w_down[l].astype(BF16),
        "final_norm_w": final_norm_w,
    }


def kernel(x_prompt, x_sample, cache_k, cache_v, cache_mem_k, cache_mem_v, state_ssm, state_conv,
           state_ffn_conv, page_table, mem_prompt, norm1_w, w_in, ssd_conv_w, ssd_conv_b, dt_bias, a_log,
           d_skip, ssd_norm_w, mem_norm_w, w_mem_kv, w_ssd_out, w_attn_out, w_mem_out, w_o, norm2_w, w_up,
           ffn_conv_w, ffn_conv_b, w_down, final_norm_w):
    depth = w_in.shape[0]
    assert depth == 1, "the final RMSNorm is fused into the single layer's last kernel"
    b_p, s_p, _ = x_prompt.shape
    n_mem = mem_prompt.shape[1]
    ns = x_sample.shape[0]
    l = 0
    p = _layer_params(l, norm1_w, w_in, ssd_conv_w, ssd_conv_b, dt_bias, a_log, d_skip, ssd_norm_w, w_ssd_out,
                      w_attn_out, w_mem_out, w_o, norm2_w, w_up, ffn_conv_w, ffn_conv_b, w_down, final_norm_w)

    memx = mem_prompt.reshape(b_p * n_mem, D_MODEL)
    wkv = w_mem_kv[l].astype(BF16)
    mk_p = _norm_matmul(memx, mem_norm_w[l], wkv[:, :MEM_WIDTH], tn=MEM_WIDTH, name="mem_k")
    mv_p = _norm_matmul(memx, mem_norm_w[l], wkv[:, MEM_WIDTH:], tn=MEM_WIDTH, name="mem_v")
    mk_p3 = mk_p.reshape(b_p, n_mem, MEM_WIDTH)
    mv_p3 = mv_p.reshape(b_p, n_mem, MEM_WIDTH)
    ssm0 = jnp.zeros((b_p, SSD_HEADS, SSD_HEAD_DIM, SSD_STATE), F32)
    conv0 = jnp.zeros((b_p, SSD_CONV - 1, CONV_DIM), F32)
    ffn0 = jnp.zeros((b_p, FFN_CONV - 1, 2 * D_FF), F32)
    y_p, k_p, v_p, s_p_new, c_p, f_p = _layer(x_prompt, mk_p3, mv_p3, ssm0, conv0, ffn0, _attend_prompt, p)

    attend_s = functools.partial(_attend_sample, cache_k=cache_k[l], cache_v=cache_v[l], page_table=page_table)
    mk_s3 = cache_mem_k[l].reshape(ns, -1, MEM_WIDTH)
    mv_s3 = cache_mem_v[l].reshape(ns, -1, MEM_WIDTH)
    y_s, k_s, v_s, s_s_new, c_s, f_s = _layer(x_sample, mk_s3, mv_s3, state_ssm[l], state_conv[l],
                                              state_ffn_conv[l], attend_s, p)

    mem_shape = (1, b_p, n_mem, MEM_HEADS, MEM_HEAD_DIM)
    return (y_p, y_s, k_p[None], v_p[None], mk_p.reshape(mem_shape), mv_p.reshape(mem_shape),
            s_p_new[None], c_p[None], f_p[None], k_s[None], v_s[None], s_s_new[None], c_s[None], f_s[None])
```

```python
import functools
import math

import jax
import jax.numpy as jnp
from jax import lax
from jax.experimental import pallas as pl
from jax.experimental.pallas import tpu as pltpu

F32 = jnp.float32
BF16 = jnp.bfloat16

D_MODEL = 1024
D_INNER = 2048
SSD_HEAD_DIM = 64
SSD_HEADS = 32
SSD_GROUPS = 8
SSD_STATE = 128
SSD_CONV = 4
SSD_CHUNK = 128
CONV_DIM = 4096
ATT_HEADS = 16
ATT_HEAD_DIM = 64
KV_HEADS = 4
GQA_REP = 4
ATT_WIDTH = 1024
KV_WIDTH = 256
MOBA_BLOCK = 256
MOBA_TOPK = 3
ROT_DIM = 16
ROPE_THETA = 500000.0
MEM_HEADS = 4
MEM_HEAD_DIM = 256
MEM_WIDTH = 1024
D_FF = 2816
FFN_CONV = 3
RMS_EPS = 1e-6
NEG_INF = -1e30
PAGE_SIZE = 128

LANES = 128
SUBLANES = 8
VMEM_LIMIT = 56 * 1024 * 1024

QKVD_WIDTH = ATT_WIDTH + 2 * KV_WIDTH + LANES
DT_COL_BLOCK = (ATT_WIDTH + 2 * KV_WIDTH) // LANES


def _cparams(sem):
    return pltpu.CompilerParams(dimension_semantics=sem, vmem_limit_bytes=VMEM_LIMIT)


def _dot(a, b):
    return jnp.dot(a, b, preferred_element_type=F32)


def _dot_nt(a, b):
    return lax.dot_general(a, b, (((1,), (1,)), ((), ())), preferred_element_type=F32)


def _dot_f32(a, b):
    return jnp.dot(a, b, preferred_element_type=F32, precision=lax.Precision.HIGHEST)


def _silu(x):
    return x * jax.nn.sigmoid(x)


def _norm_matmul_kernel(x_ref, nw_ref, w_ref, o_ref, xn_ref, *, act):
    @pl.when(pl.program_id(1) == 0)
    def _():
        x = x_ref[...]
        ms = jnp.mean(x * x, axis=-1, keepdims=True)
        xn_ref[...] = (x * lax.rsqrt(ms + RMS_EPS) * nw_ref[...]).astype(BF16)

    y = _dot(xn_ref[...], w_ref[...])
    if act == "sigmoid":
        y = jax.nn.sigmoid(y)
    o_ref[...] = y.astype(o_ref.dtype)


def _norm_matmul(x, nw, w, *, tn, out_dtype=F32, act=None, name="norm_matmul"):
    t, d = x.shape
    n = w.shape[1]
    tm = min(t, 1024)
    assert t % tm == 0 and n % tn == 0
    return pl.pallas_call(
        functools.partial(_norm_matmul_kernel, act=act),
        grid=(t // tm, n // tn),
        in_specs=[pl.BlockSpec((tm, d), lambda i, j: (i, 0)),
                  pl.BlockSpec((1, d), lambda i, j: (0, 0)),
                  pl.BlockSpec((d, tn), lambda i, j: (0, j))],
        out_specs=pl.BlockSpec((tm, tn), lambda i, j: (i, j)),
        out_shape=jax.ShapeDtypeStruct((t, n), out_dtype),
        scratch_shapes=[pltpu.VMEM((tm, d), BF16)],
        compiler_params=_cparams(("parallel", "arbitrary")),
        name=name,
    )(x, nw.reshape(1, d), w)


def _shift_rows(x, k, halo):
    r = pltpu.roll(x, k, axis=0)
    row = lax.broadcasted_iota(jnp.int32, halo.shape, 0)
    top = jnp.where(row < k, pltpu.roll(halo, k, axis=0), r[:SUBLANES])
    if x.shape[0] == SUBLANES:
        return top
    return jnp.concatenate([top, r[SUBLANES:]], axis=0)


def _ssd_kernel(z_ref, xa_ref, xb_ref, dt_ref, cst_ref, s0_ref, cw_ref, cb_ref, dtb_ref, alog_ref,
                dsk_ref, nw_ref, y_ref, sf_ref, s_ref, carry_ref, ybuf_ref, *, qb, q, nchunks):
    c = pl.program_id(1)

    @pl.when(c == 0)
    def _():
        s_ref[...] = s0_ref[0]
        carry_ref[...] = cst_ref[0]

    pre = jnp.concatenate([xa_ref[0], xb_ref[0]], axis=1)
    halo = carry_ref[...]
    acc = _shift_rows(pre, 3, halo) * cw_ref[0:1, :]
    acc = acc + _shift_rows(pre, 2, halo) * cw_ref[1:2, :]
    acc = acc + _shift_rows(pre, 1, halo) * cw_ref[2:3, :]
    acc = acc + pre * cw_ref[3:4, :]
    xbc = _silu(acc + cb_ref[...])
    if nchunks > 1:
        carry_ref[...] = pre[qb - SUBLANES:qb, :]

    z = z_ref[0]
    dt_raw = dt_ref[0]
    if qb < q:
        xbc = jnp.concatenate([xbc, jnp.zeros((q - qb, CONV_DIM), F32)], axis=0)
        z = jnp.concatenate([z, jnp.zeros((q - qb, D_INNER), F32)], axis=0)
        dt_raw = jnp.concatenate([dt_raw, jnp.zeros((q - qb, LANES), F32)], axis=0)

    xs = xbc[:, :D_INNER]
    xs_bf = xs.astype(BF16)
    xs_t = xs.T

    v = dt_raw + dtb_ref[...]
    dt = jnp.maximum(v, 0.0) + jnp.log1p(jnp.exp(-jnp.abs(v)))
    row_q = lax.broadcasted_iota(jnp.int32, (q, LANES), 0)
    if qb < q:
        dt = jnp.where(row_q < qb, dt, 0.0)
    a = -jnp.exp(alog_ref[...])
    ri = lax.broadcasted_iota(jnp.int32, (q, q), 0)
    ci = lax.broadcasted_iota(jnp.int32, (q, q), 1)
    causal = ci <= ri
    cs = _dot_f32(causal.astype(F32), dt * a)
    cs_t = cs.T
    dt_t = dt.T
    ecs = jnp.exp(cs)

    for g in range(SSD_GROUPS):
        b_g = xbc[:, D_INNER + g * SSD_STATE:D_INNER + (g + 1) * SSD_STATE].astype(BF16)
        c_g = xbc[:, D_INNER + SSD_GROUPS * SSD_STATE + g * SSD_STATE:
                  D_INNER + SSD_GROUPS * SSD_STATE + (g + 1) * SSD_STATE].astype(BF16)
        cb = _dot_nt(c_g, b_g)
        for r in range(SSD_HEADS // SSD_GROUPS):
            h = g * (SSD_HEADS // SSD_GROUPS) + r
            lo = h * SSD_HEAD_DIM
            cs_col = cs[:, h:h + 1]
            cs_row = cs_t[h:h + 1, :]
            dt_row = dt_t[h:h + 1, :]
            seg = jnp.where(causal, cs_col - cs_row, -jnp.inf)
            m_h = (cb * jnp.exp(seg) * dt_row).astype(BF16)
            s_h = s_ref[h]
            y_h = _dot(m_h, xs_bf[:, lo:lo + SSD_HEAD_DIM])
            y_h = y_h + _dot_nt(c_g, s_h.astype(BF16)) * ecs[:, h:h + 1]
            ybuf_ref[:, lo:lo + SSD_HEAD_DIM] = y_h
            cs_end = cs_row[:, q - 1:q]
            w_row = dt_row * jnp.exp(cs_end - cs_row)
            xw = (xs_t[lo:lo + SSD_HEAD_DIM, :] * w_row).astype(BF16)
            s_ref[h] = s_h * jnp.exp(cs_end) + _dot(xw, b_g)

    y = ybuf_ref[...] + dsk_ref[...] * xs
    gt = y * _silu(z)
    gw = D_INNER // SSD_GROUPS
    parts = []
    for g in range(SSD_GROUPS):
        gg = gt[:, g * gw:(g + 1) * gw]
        parts.append(gg * lax.rsqrt(jnp.mean(gg * gg, axis=-1, keepdims=True) + RMS_EPS))
    yn = jnp.concatenate(parts, axis=1) * nw_ref[...]
    y_ref[0] = yn[:qb].astype(y_ref.dtype)

    @pl.when(c == nchunks - 1)
    def _():
        sf_ref[0] = s_ref[...]


def _ssd(zx3, qkvd3, conv_state8, ssm0, cw, cb, dtb, alog, dsk, nw):
    nb, l, _ = zx3.shape
    q = SSD_CHUNK
    qb = min(l, q)
    assert l % qb == 0 and qb % SUBLANES == 0
    nchunks = l // qb
    half = CONV_DIM // 2
    row_blk = lambda col: (lambda b, c: (b, c, col))
    full2 = lambda b, c: (0, 0)
    return pl.pallas_call(
        functools.partial(_ssd_kernel, qb=qb, q=q, nchunks=nchunks),
        grid=(nb, nchunks),
        in_specs=[pl.BlockSpec((1, qb, D_INNER), row_blk(0)),
                  pl.BlockSpec((1, qb, half), row_blk(1)),
                  pl.BlockSpec((1, qb, half), row_blk(2)),
                  pl.BlockSpec((1, qb, LANES), row_blk(DT_COL_BLOCK)),
                  pl.BlockSpec((1, SUBLANES, CONV_DIM), lambda b, c: (b, 0, 0)),
                  pl.BlockSpec((1, SSD_HEADS, SSD_HEAD_DIM, SSD_STATE), lambda b, c: (b, 0, 0, 0)),
                  pl.BlockSpec((SSD_CONV, CONV_DIM), full2),
                  pl.BlockSpec((1, CONV_DIM), full2),
                  pl.BlockSpec((1, LANES), full2),
                  pl.BlockSpec((1, LANES), full2),
                  pl.BlockSpec((1, D_INNER), full2),
                  pl.BlockSpec((1, D_INNER), full2)],
        out_specs=[pl.BlockSpec((1, qb, D_INNER), lambda b, c: (b, c, 0)),
                   pl.BlockSpec((1, SSD_HEADS, SSD_HEAD_DIM, SSD_STATE), lambda b, c: (b, 0, 0, 0))],
        out_shape=[jax.ShapeDtypeStruct((nb, l, D_INNER), BF16),
                   jax.ShapeDtypeStruct((nb, SSD_HEADS, SSD_HEAD_DIM, SSD_STATE), F32)],
        scratch_shapes=[pltpu.VMEM((SSD_HEADS, SSD_HEAD_DIM, SSD_STATE), F32),
                        pltpu.VMEM((SUBLANES, CONV_DIM), F32),
                        pltpu.VMEM((q, D_INNER), F32)],
        compiler_params=_cparams(("parallel", "arbitrary")),
        name="ssd",
    )(zx3, zx3, zx3, qkvd3, conv_state8, ssm0, cw, cb, dtb, alog, dsk, nw)


def _rope_tables(pos):
    half = ROT_DIM // 2
    inv = ROPE_THETA ** (-jnp.arange(half, dtype=F32) * 2.0 / ROT_DIM)
    ang = pos.astype(F32)[:, None] * inv[None, :]
    cos, sin = jnp.cos(ang), jnp.sin(ang)
    n = pos.shape[0]
    pad = jnp.zeros((n, ATT_HEAD_DIM - ROT_DIM), F32)
    zero = jnp.zeros((n, half), F32)
    c_head = jnp.concatenate([cos, cos, pad + 1.0], axis=1)
    s1_head = jnp.concatenate([-sin, zero, pad], axis=1)
    s2_head = jnp.concatenate([zero, sin, pad], axis=1)
    rep = LANES // ATT_HEAD_DIM
    return jnp.tile(c_head, (1, rep)), jnp.tile(s1_head, (1, rep)), jnp.tile(s2_head, (1, rep))


def _rope_group(xg, c, s1, s2):
    half = ROT_DIM // 2
    return xg * c + pltpu.roll(xg, LANES - half, axis=1) * s1 + pltpu.roll(xg, half, axis=1) * s2


def _head_from_group(xg, odd, lane):
    if odd:
        xg = pltpu.roll(xg, ATT_HEAD_DIM, axis=1)
    return jnp.where(lane < ATT_HEAD_DIM, xg, 0.0)


def _prompt_prep_kernel(q_ref, k_ref, v_ref, c_ref, s1_ref, s2_ref,
                        krot_ref, qhm_ref, kaug_ref, vhm_ref, kmean_ref):
    i = pl.program_id(1)
    rows = q_ref.shape[1]
    c, s1, s2 = c_ref[...], s1_ref[...], s2_ref[...]
    lane = lax.broadcasted_iota(jnp.int32, (rows, LANES), 1)
    scale = ATT_HEAD_DIM ** -0.5

    @pl.when(i == 0)
    def _():
        kmean_ref[...] = jnp.zeros_like(kmean_ref)

    for cg in range(ATT_WIDTH // LANES):
        qg = _rope_group(q_ref[0, :, cg * LANES:(cg + 1) * LANES], c, s1, s2) * scale
        for odd in range(2):
            qhm_ref[0, 2 * cg + odd] = _head_from_group(qg, odd, lane).astype(BF16)

    onehot = jnp.where(lane == ATT_HEAD_DIM + i, 1.0, 0.0)
    for cg in range(KV_WIDTH // LANES):
        kg = _rope_group(k_ref[0, :, cg * LANES:(cg + 1) * LANES], c, s1, s2)
        krot_ref[0, :, cg * LANES:(cg + 1) * LANES] = kg
        vg = v_ref[0, :, cg * LANES:(cg + 1) * LANES]
        for odd in range(2):
            g = 2 * cg + odd
            kh = _head_from_group(kg, odd, lane)
            kaug_ref[0, g] = (kh + onehot).astype(BF16)
            vhm_ref[0, g] = _head_from_group(vg, odd, lane).astype(BF16)
            kmean_ref[0, g, pl.ds(i, 1), :] = jnp.mean(kh, axis=0, keepdims=True)


def _prompt_prep(qkvd3, tables):
    b, s, _ = qkvd3.shape
    nblk = s // MOBA_BLOCK
    assert s % MOBA_BLOCK == 0 and nblk <= SUBLANES
    blk = MOBA_BLOCK
    tab = pl.BlockSpec((blk, LANES), lambda bi, i: (i, 0))
    return pl.pallas_call(
        _prompt_prep_kernel,
        grid=(b, nblk),
        in_specs=[pl.BlockSpec((1, blk, ATT_WIDTH), lambda bi, i: (bi, i, 0)),
                  pl.BlockSpec((1, blk, KV_WIDTH), lambda bi, i: (bi, i, ATT_WIDTH // KV_WIDTH)),
                  pl.BlockSpec((1, blk, KV_WIDTH), lambda bi, i: (bi, i, ATT_WIDTH // KV_WIDTH + 1)),
                  tab, tab, tab],
        out_specs=[pl.BlockSpec((1, blk, KV_WIDTH), lambda bi, i: (bi, i, 0)),
                   pl.BlockSpec((1, ATT_HEADS, blk, LANES), lambda bi, i: (bi, 0, i, 0)),
                   pl.BlockSpec((1, KV_HEADS, blk, LANES), lambda bi, i: (bi, 0, i, 0)),
                   pl.BlockSpec((1, KV_HEADS, blk, LANES), lambda bi, i: (bi, 0, i, 0)),
                   pl.BlockSpec((1, KV_HEADS, SUBLANES, LANES), lambda bi, i: (bi, 0, 0, 0))],
        out_shape=[jax.ShapeDtypeStruct((b, s, KV_WIDTH), F32),
                   jax.ShapeDtypeStruct((b, ATT_HEADS, s, LANES), BF16),
                   jax.ShapeDtypeStruct((b, KV_HEADS, s, LANES), BF16),
                   jax.ShapeDtypeStruct((b, KV_HEADS, s, LANES), BF16),
                   jax.ShapeDtypeStruct((b, KV_HEADS, SUBLANES, LANES), F32)],
        compiler_params=_cparams(("parallel", "arbitrary")),
        name="prompt_prep",
    )(qkvd3, qkvd3, qkvd3, *tables)


def _topk_lanes(gate, valid, lane_f, rounds=MOBA_TOPK):
    g0 = jnp.where(valid, gate, -jnp.inf)
    sel = jnp.zeros(gate.shape, dtype=jnp.bool_)
    for _ in range(rounds):
        m = jnp.max(g0, axis=1, keepdims=True)
        idx = jnp.min(jnp.where(g0 == m, lane_f, 1e9), axis=1, keepdims=True)
        pick = jnp.logical_and(lane_f == idx, m > -jnp.inf)
        sel = jnp.logical_or(sel, pick)
        g0 = jnp.where(pick, -jnp.inf, g0)
    return sel


def _moba_prompt_block(n_past, q_ref, k_ref, v_ref, km_ref, y_ref):
    blk = MOBA_BLOCK
    k_own = k_ref[0, 0, n_past * blk:(n_past + 1) * blk, :]
    v_own = v_ref[0, 0, n_past * blk:(n_past + 1) * blk, :]
    ri = lax.broadcasted_iota(jnp.int32, (blk, blk), 0)
    ci = lax.broadcasted_iota(jnp.int32, (blk, blk), 1)
    own_ok = ci <= ri
    if n_past > 0:
        k_past = k_ref[0, 0, 0:n_past * blk, :]
        v_past = v_ref[0, 0, 0:n_past * blk, :]
        kmt = jnp.concatenate([jnp.zeros((ATT_HEAD_DIM, LANES), F32), km_ref[0, 0],
                               jnp.zeros((LANES - ATT_HEAD_DIM - SUBLANES, LANES), F32)], axis=0).astype(BF16)
        lane = lax.broadcasted_iota(jnp.int32, (blk, LANES), 1)
        lane_f = lane.astype(F32)
        slot = jnp.logical_and(lane >= ATT_HEAD_DIM, lane < ATT_HEAD_DIM + SUBLANES)
        valid = jnp.logical_and(lane >= ATT_HEAD_DIM, lane < ATT_HEAD_DIM + n_past)

    for r in range(GQA_REP):
        qh = q_ref[0, r]
        s_own = jnp.where(own_ok, _dot_nt(qh, k_own), NEG_INF)
        m = jnp.max(s_own, axis=1, keepdims=True)
        if n_past > 0:
            gate = _dot_nt(qh, kmt)
            sel = _topk_lanes(gate, valid, lane_f, rounds=min(MOBA_TOPK, n_past))
            bias = jnp.where(slot, jnp.where(sel, 0.0, NEG_INF), 0.0)
            q_aug = (qh.astype(F32) + bias).astype(BF16)
            s_past = _dot_nt(q_aug, k_past)
            m = jnp.maximum(m, jnp.max(s_past, axis=1, keepdims=True))
        p_own = jnp.exp(s_own - m)
        den = jnp.sum(p_own, axis=1, keepdims=True)
        o = _dot(p_own.astype(BF16), v_own)
        if n_past > 0:
            p_past = jnp.exp(s_past - m)
            den = den + jnp.sum(p_past, axis=1, keepdims=True)
            o = o + _dot(p_past.astype(BF16), v_past)
        o = o / den
        y_ref[:, r * ATT_HEAD_DIM:(r + 1) * ATT_HEAD_DIM] = o[:, :ATT_HEAD_DIM].astype(y_ref.dtype)


def _moba_prompt_kernel(q_ref, k_ref, v_ref, km_ref, y_ref, *, nblk):
    i = pl.program_id(2)
    for n_past in range(nblk):
        pl.when(i == n_past)(functools.partial(_moba_prompt_block, n_past, q_ref, k_ref, v_ref, km_ref, y_ref))


def _moba_prompt(q_hm, k_aug, v_hm, kmean):
    b, _, s, _ = q_hm.shape
    nblk = s // MOBA_BLOCK
    blk = MOBA_BLOCK
    return pl.pallas_call(
        functools.partial(_moba_prompt_kernel, nblk=nblk),
        grid=(b, KV_HEADS, nblk),
        in_specs=[pl.BlockSpec((1, GQA_REP, blk, LANES), lambda bi, g, i: (bi, g, i, 0)),
                  pl.BlockSpec((1, 1, s, LANES), lambda bi, g, i: (bi, g, 0, 0)),
                  pl.BlockSpec((1, 1, s, LANES), lambda bi, g, i: (bi, g, 0, 0)),
                  pl.BlockSpec((1, 1, SUBLANES, LANES), lambda bi, g, i: (bi, g, 0, 0))],
        out_specs=pl.BlockSpec((blk, GQA_REP * ATT_HEAD_DIM), lambda bi, g, i: (bi * nblk + i, g)),
        out_shape=jax.ShapeDtypeStruct((b * s, ATT_WIDTH), BF16),
        compiler_params=_cparams(("parallel", "parallel", "arbitrary")),
        name="moba_prompt",
    )(q_hm, k_aug, v_hm, kmean)


def _rope_kernel(x_ref, c_ref, s1_ref, s2_ref, o_ref, *, q_groups):
    c, s1, s2 = c_ref[...], s1_ref[...], s2_ref[...]
    scale = ATT_HEAD_DIM ** -0.5
    for cg in range(x_ref.shape[1] // LANES):
        xg = _rope_group(x_ref[:, cg * LANES:(cg + 1) * LANES], c, s1, s2)
        if cg < q_groups:
            xg = xg * scale
        o_ref[:, cg * LANES:(cg + 1) * LANES] = xg


def _rope_sample(qkvd, tables):
    t = qkvd.shape[0]
    w = ATT_WIDTH + KV_WIDTH
    tab = pl.BlockSpec((t, LANES), lambda i: (0, 0))
    return pl.pallas_call(
        functools.partial(_rope_kernel, q_groups=ATT_WIDTH // LANES),
        grid=(1,),
        in_specs=[pl.BlockSpec((t, w), lambda i: (0, 0)), tab, tab, tab],
        out_specs=pl.BlockSpec((t, w), lambda i: (0, 0)),
        out_shape=jax.ShapeDtypeStruct((t, w), F32),
        compiler_params=_cparams(("arbitrary",)),
        name="rope_sample",
    )(qkvd, *tables)


def _moba_sample_kernel(pt_ref, qx_ref, kn_ref, vn_ref, e_ref, ck_ref, cv_ref, y_ref,
                        kbuf, vbuf, sem, *, npages, t):
    s = pl.program_id(0)
    ns = pl.num_programs(0)
    slot = s % 2

    def k_copy(seq, p, sl):
        return pltpu.make_async_copy(ck_ref.at[pt_ref[seq, p]], kbuf.at[sl, :, p * PAGE_SIZE:(p + 1) * PAGE_SIZE],
                                     sem.at[0, sl])

    def v_copy(seq, p, sl):
        return pltpu.make_async_copy(cv_ref.at[pt_ref[seq, p]], vbuf.at[sl, :, p * PAGE_SIZE:(p + 1) * PAGE_SIZE],
                                     sem.at[1, sl])

    def start_all(seq, sl):
        for p in range(npages):
            k_copy(seq, p, sl).start()
            v_copy(seq, p, sl).start()

    @pl.when(s == 0)
    def _():
        start_all(0, 0)

    @pl.when(s + 1 < ns)
    def _():
        start_all(s + 1, 1 - slot)

    for p in range(npages):
        k_copy(s, p, slot).wait()
        v_copy(s, p, slot).wait()

    past = npages * PAGE_SIZE
    nblk = past // MOBA_BLOCK
    rows = ATT_HEADS * t
    chunk = min(past, 2048)
    qx = qx_ref[0]

    s_raw = jnp.concatenate(
        [_dot(qx, kbuf[slot, :, c * chunk:(c + 1) * chunk].astype(BF16)) for c in range(past // chunk)], axis=1)
    lane = lax.broadcasted_iota(jnp.int32, (rows, LANES), 1)
    gate = jnp.zeros((rows, LANES), F32)
    for n in range(nblk):
        col = jnp.sum(s_raw[:, n * MOBA_BLOCK:(n + 1) * MOBA_BLOCK], axis=1, keepdims=True)
        gate = jnp.where(lane == n, col, gate)
    sel = _topk_lanes(gate, lane < nblk, lane.astype(F32))
    selb = jnp.where(sel, 0.0, NEG_INF).astype(BF16)
    s_past = s_raw + _dot(selb, e_ref[...])

    kn = jnp.concatenate([kn_ref[0], jnp.zeros((LANES - t, KV_WIDTH), F32)], axis=0).astype(BF16)
    vn = jnp.concatenate([vn_ref[0], jnp.zeros((LANES - t, KV_WIDTH), F32)], axis=0).astype(BF16)
    row = lax.broadcasted_iota(jnp.int32, (rows, LANES), 0)
    own_ok = lane <= (row % t)
    s_own = jnp.where(own_ok, _dot_nt(qx, kn), NEG_INF)

    m = jnp.maximum(jnp.max(s_past, axis=1, keepdims=True), jnp.max(s_own, axis=1, keepdims=True))
    p_past = jnp.exp(s_past - m)
    p_own = jnp.exp(s_own - m)
    den = jnp.sum(p_past, axis=1, keepdims=True) + jnp.sum(p_own, axis=1, keepdims=True)
    o = _dot(p_own.astype(BF16), vn)
    for c in range(past // chunk):
        o = o + _dot_nt(p_past[:, c * chunk:(c + 1) * chunk].astype(BF16),
                        vbuf[slot, :, c * chunk:(c + 1) * chunk].astype(BF16))
    o = o / den
    for h in range(ATT_HEADS):
        g = h // GQA_REP
        y_ref[0, :, h * ATT_HEAD_DIM:(h + 1) * ATT_HEAD_DIM] = (
            o[h * t:(h + 1) * t, g * ATT_HEAD_DIM:(g + 1) * ATT_HEAD_DIM].astype(y_ref.dtype))


def _moba_sample(page_table, q_exp, k_new, v_new, e_mat, cache_k, cache_v):
    ns, npages = page_table.shape
    t = k_new.shape[1]
    past = npages * PAGE_SIZE
    assert past % MOBA_BLOCK == 0 and past // MOBA_BLOCK <= LANES and t == SUBLANES
    rows = ATT_HEADS * t
    grid_spec = pltpu.PrefetchScalarGridSpec(
        num_scalar_prefetch=1,
        grid=(ns,),
        in_specs=[pl.BlockSpec((1, rows, KV_WIDTH), lambda s, pt: (s, 0, 0)),
                  pl.BlockSpec((1, t, KV_WIDTH), lambda s, pt: (s, 0, 0)),
                  pl.BlockSpec((1, t, KV_WIDTH), lambda s, pt: (s, 0, 0)),
                  pl.BlockSpec((LANES, past), lambda s, pt: (0, 0)),
                  pl.BlockSpec(memory_space=pl.ANY),
                  pl.BlockSpec(memory_space=pl.ANY)],
        out_specs=pl.BlockSpec((1, t, ATT_WIDTH), lambda s, pt: (s, 0, 0)),
        scratch_shapes=[pltpu.VMEM((2, KV_WIDTH, past), F32),
                        pltpu.VMEM((2, KV_WIDTH, past), F32),
                        pltpu.SemaphoreType.DMA((2, 2))],
    )
    return pl.pallas_call(
        functools.partial(_moba_sample_kernel, npages=npages, t=t),
        grid_spec=grid_spec,
        out_shape=jax.ShapeDtypeStruct((ns, t, ATT_WIDTH), BF16),
        compiler_params=_cparams(("arbitrary",)),
        name="moba_sample",
    )(page_table, q_exp, k_new, v_new, e_mat, cache_k, cache_v)


def _mem_attn_kernel(q_ref, mk_ref, mv_ref, y_ref, *, heads_split):
    scale = MEM_HEAD_DIM ** -0.5
    for h in range(MEM_HEADS):
        lo, hi = h * MEM_HEAD_DIM, (h + 1) * MEM_HEAD_DIM
        mk = mk_ref[0, :, h, :] if heads_split else mk_ref[0, :, lo:hi]
        mv = mv_ref[0, :, h, :] if heads_split else mv_ref[0, :, lo:hi]
        q = (q_ref[0, :, lo:hi].astype(F32) * scale).astype(BF16)
        s = _dot_nt(q, mk.astype(BF16))
        m = jnp.max(s, axis=1, keepdims=True)
        p = jnp.exp(s - m)
        den = jnp.sum(p, axis=1, keepdims=True)
        o = _dot(p.astype(BF16), mv.astype(BF16)) / den
        y_ref[0, :, lo:hi] = o.astype(y_ref.dtype)


def _mem_attn(qm3, mk, mv):
    nb, l, _ = qm3.shape
    m = mk.shape[1]
    tl = min(l, 512)
    assert l % tl == 0
    heads_split = mk.ndim == 4
    if heads_split:
        mem_spec = pl.BlockSpec((1, m, MEM_HEADS, MEM_HEAD_DIM), lambda b, i: (b, 0, 0, 0))
    else:
        mem_spec = pl.BlockSpec((1, m, MEM_WIDTH), lambda b, i: (b, 0, 0))
    return pl.pallas_call(
        functools.partial(_mem_attn_kernel, heads_split=heads_split),
        grid=(nb, l // tl),
        in_specs=[pl.BlockSpec((1, tl, MEM_WIDTH), lambda b, i: (b, i, 0)), mem_spec, mem_spec],
        out_specs=pl.BlockSpec((1, tl, MEM_WIDTH), lambda b, i: (b, i, 0)),
        out_shape=jax.ShapeDtypeStruct((nb, l, MEM_WIDTH), BF16),
        compiler_params=_cparams(("parallel", "arbitrary")),
        name="mem_attn",
    )(qm3, mk, mv)


def _merge_kernel(x_ref, gs_ref, ys_ref, ya_ref, ym_ref, ws_ref, wa_ref, wm_ref, wo_ref, h_ref):
    d = D_MODEL
    merged = gs_ref[:, 0:d] * _dot(ys_ref[...], ws_ref[...])
    merged = merged + gs_ref[:, d:2 * d] * _dot(ya_ref[...], wa_ref[...])
    merged = merged + gs_ref[:, 2 * d:3 * d] * _dot(ym_ref[...], wm_ref[...])
    h_ref[...] = x_ref[...] + _dot(merged.astype(BF16), wo_ref[...])


def _merge(x, gs, y_ssd, y_att, y_mem, ws, wa, wm, wo):
    t = x.shape[0]
    tm = min(t, 512)
    assert t % tm == 0
    rows = lambda w: pl.BlockSpec((tm, w), lambda i: (i, 0))
    full = lambda a: pl.BlockSpec(a.shape, lambda i: (0, 0))
    return pl.pallas_call(
        _merge_kernel,
        grid=(t // tm,),
        in_specs=[rows(D_MODEL), rows(3 * D_MODEL), rows(D_INNER), rows(ATT_WIDTH), rows(MEM_WIDTH),
                  full(ws), full(wa), full(wm), full(wo)],
        out_specs=rows(D_MODEL),
        out_shape=jax.ShapeDtypeStruct((t, D_MODEL), F32),
        compiler_params=_cparams(("parallel",)),
        name="merge",
    )(x, gs, y_ssd, y_att, y_mem, ws, wa, wm, wo)


def _ffn_gate_kernel(u_ref, st_ref, w_ref, b_ref, a_ref, carry_ref, *, tl, ntiles):
    i = pl.program_id(1)

    @pl.when(i == 0)
    def _():
        carry_ref[...] = st_ref[0]

    u = u_ref[0]
    halo = carry_ref[...]
    acc = _shift_rows(u, 2, halo) * w_ref[0:1, :]
    acc = acc + _shift_rows(u, 1, halo) * w_ref[1:2, :]
    acc = acc + u * w_ref[2:3, :]
    uc = acc + b_ref[...]
    if ntiles > 1:
        carry_ref[...] = u[tl - SUBLANES:tl, :]
    a_ref[0] = (_silu(uc[:, :D_FF]) * uc[:, D_FF:]).astype(a_ref.dtype)


def _ffn_gate(u3, state8, w, b):
    nb, l, c = u3.shape
    tl = min(l, 256)
    assert l % tl == 0 and tl % SUBLANES == 0
    ntiles = l // tl
    return pl.pallas_call(
        functools.partial(_ffn_gate_kernel, tl=tl, ntiles=ntiles),
        grid=(nb, ntiles),
        in_specs=[pl.BlockSpec((1, tl, c), lambda bi, i: (bi, i, 0)),
                  pl.BlockSpec((1, SUBLANES, c), lambda bi, i: (bi, 0, 0)),
                  pl.BlockSpec((FFN_CONV, c), lambda bi, i: (0, 0)),
                  pl.BlockSpec((1, c), lambda bi, i: (0, 0))],
        out_specs=pl.BlockSpec((1, tl, D_FF), lambda bi, i: (bi, i, 0)),
        out_shape=jax.ShapeDtypeStruct((nb, l, D_FF), BF16),
        scratch_shapes=[pltpu.VMEM((SUBLANES, c), F32)],
        compiler_params=_cparams(("parallel", "arbitrary")),
        name="ffn_gate",
    )(u3, state8, w, b)


def _ffn_fused_kernel(h_ref, st_ref, n2_ref, wu_ref, cw_ref, cb_ref, wd_ref, fw_ref, y_ref, last_ref, carry_ref,
                      *, tl, ntiles, ck):
    i = pl.program_id(1)

    @pl.when(i == 0)
    def _():
        carry_ref[...] = st_ref[0]

    h = h_ref[0]
    ms = jnp.mean(h * h, axis=-1, keepdims=True)
    hn = (h * lax.rsqrt(ms + RMS_EPS) * n2_ref[...]).astype(BF16)

    def conv(u, off):
        halo = carry_ref[:, off:off + ck]
        acc = _shift_rows(u, 2, halo) * cw_ref[0:1, off:off + ck]
        acc = acc + _shift_rows(u, 1, halo) * cw_ref[1:2, off:off + ck]
        acc = acc + u * cw_ref[2:3, off:off + ck]
        carry_ref[:, off:off + ck] = u[tl - SUBLANES:tl, :]
        return acc + cb_ref[:, off:off + ck]

    out = h
    for c in range(D_FF // ck):
        lo = c * ck
        ug = conv(_dot(hn, wu_ref[:, lo:lo + ck]), lo)
        uv = conv(_dot(hn, wu_ref[:, D_FF + lo:D_FF + lo + ck]), D_FF + lo)
        out = out + _dot((_silu(ug) * uv).astype(BF16), wd_ref[lo:lo + ck, :])
    ms = jnp.mean(out * out, axis=-1, keepdims=True)
    y_ref[0] = out * lax.rsqrt(ms + RMS_EPS) * fw_ref[...]

    @pl.when(i == ntiles - 1)
    def _():
        last_ref[0] = carry_ref[...]


def _ffn_fused(h3, state8, n2, wu, cw, cb, wd, fw):
    nb, l, d = h3.shape
    tl = 512
    ck = D_FF // 2
    assert l % tl == 0 and ck % LANES == 0
    ntiles = l // tl
    const = lambda a: pl.BlockSpec(a.shape, lambda bi, i: (0,) * a.ndim, pipeline_mode=pl.Buffered(1))
    n2, fw = n2.reshape(1, d), fw.reshape(1, d)
    return pl.pallas_call(
        functools.partial(_ffn_fused_kernel, tl=tl, ntiles=ntiles, ck=ck),
        grid=(nb, ntiles),
        in_specs=[pl.BlockSpec((1, tl, d), lambda bi, i: (bi, i, 0)),
                  pl.BlockSpec((1, SUBLANES, 2 * D_FF), lambda bi, i: (bi, 0, 0)),
                  const(n2), const(wu), const(cw), const(cb), const(wd), const(fw)],
        out_specs=[pl.BlockSpec((1, tl, d), lambda bi, i: (bi, i, 0)),
                   pl.BlockSpec((1, SUBLANES, 2 * D_FF), lambda bi, i: (bi, 0, 0))],
        out_shape=[jax.ShapeDtypeStruct((nb, l, d), F32),
                   jax.ShapeDtypeStruct((nb, SUBLANES, 2 * D_FF), F32)],
        scratch_shapes=[pltpu.VMEM((SUBLANES, 2 * D_FF), F32)],
        compiler_params=_cparams(("parallel", "arbitrary")),
        name="ffn_fused",
    )(h3, state8, n2, wu, cw, cb, wd, fw)


def _down_kernel(a_ref, w_ref, h_ref, nw_ref, y_ref):
    h = h_ref[...] + _dot(a_ref[...], w_ref[...])
    ms = jnp.mean(h * h, axis=-1, keepdims=True)
    y_ref[...] = h * lax.rsqrt(ms + RMS_EPS) * nw_ref[...]


def _down(act, w, h, nw):
    t = h.shape[0]
    tm = min(t, 512)
    assert t % tm == 0
    return pl.pallas_call(
        _down_kernel,
        grid=(t // tm,),
        in_specs=[pl.BlockSpec((tm, D_FF), lambda i: (i, 0)),
                  pl.BlockSpec((D_FF, D_MODEL), lambda i: (0, 0)),
                  pl.BlockSpec((tm, D_MODEL), lambda i: (i, 0)),
                  pl.BlockSpec((1, D_MODEL), lambda i: (0, 0))],
        out_specs=pl.BlockSpec((tm, D_MODEL), lambda i: (i, 0)),
        out_shape=jax.ShapeDtypeStruct((t, D_MODEL), F32),
        compiler_params=_cparams(("parallel",)),
        name="ffn_down",
    )(act, w, h, nw.reshape(1, D_MODEL))


def _pad_state_rows(state):
    nb, k, c = state.shape
    return jnp.concatenate([jnp.zeros((nb, SUBLANES - k, c), state.dtype), state], axis=1)


def _layer(x3, mem_k3, mem_v3, ssm0, conv_state, ffn_state, attend, p):
    nb, l, d = x3.shape
    t = nb * l
    x = x3.reshape(t, d)
    nw1 = p["norm1_w"]
    zx = _norm_matmul(x, nw1, p["w_zx"], tn=2048, name="proj_zx")
    qkvd = _norm_matmul(x, nw1, p["w_qkvd"], tn=QKVD_WIDTH, name="proj_qkvd")
    qm = _norm_matmul(x, nw1, p["w_qm"], tn=MEM_WIDTH, out_dtype=BF16, name="proj_qm")
    gs = _norm_matmul(x, nw1, p["w_gates"], tn=D_MODEL, act="sigmoid", name="proj_gates")

    zx3 = zx.reshape(nb, l, D_INNER + CONV_DIM)
    qkvd3 = qkvd.reshape(nb, l, QKVD_WIDTH)
    y_ssd, ssm_new = _ssd(zx3, qkvd3, _pad_state_rows(conv_state), ssm0, p["ssd_conv_w"], p["ssd_conv_b"],
                          p["dt_bias"], p["a_log"], p["d_skip"], p["ssd_norm_w"])
    conv_new = zx3[:, l - (SSD_CONV - 1):, D_INNER:]

    y_att, k_new, v_new = attend(qkvd3)
    y_mem = _mem_attn(qm.reshape(nb, l, MEM_WIDTH), mem_k3, mem_v3)

    h = _merge(x, gs, y_ssd.reshape(t, D_INNER), y_att.reshape(t, ATT_WIDTH), y_mem.reshape(t, MEM_WIDTH),
               p["w_ssd_out"], p["w_attn_out"], p["w_mem_out"], p["w_o"])

    if l % 512 == 0:
        y3, last8 = _ffn_fused(h.reshape(nb, l, d), _pad_state_rows(ffn_state), p["norm2_w"], p["w_up"],
                               p["ffn_conv_w"], p["ffn_conv_b"], p["w_down"], p["final_norm_w"])
        ffn_new = last8[:, SUBLANES - (FFN_CONV - 1):]
        return y3, k_new, v_new, ssm_new, conv_new, ffn_new

    u = _norm_matmul(h, p["norm2_w"], p["w_up"], tn=2 * D_FF // 11, name="ffn_up")
    u3 = u.reshape(nb, l, 2 * D_FF)
    ffn_new = u3[:, l - (FFN_CONV - 1):]
    act = _ffn_gate(u3, _pad_state_rows(ffn_state), p["ffn_conv_w"], p["ffn_conv_b"])
    y = _down(act.reshape(t, D_FF), p["w_down"], h, p["final_norm_w"])
    return y.reshape(nb, l, d), k_new, v_new, ssm_new, conv_new, ffn_new


def _attend_prompt(qkvd3):
    b, s, _ = qkvd3.shape
    tables = _rope_tables(jnp.arange(s, dtype=jnp.int32))
    k_rot, q_hm, k_aug, v_hm, kmean = _prompt_prep(qkvd3, tables)
    y_att = _moba_prompt(q_hm, k_aug, v_hm, kmean)
    v = qkvd3[:, :, ATT_WIDTH + KV_WIDTH:ATT_WIDTH + 2 * KV_WIDTH]
    return y_att, k_rot.reshape(b, s, KV_HEADS, ATT_HEAD_DIM), v.reshape(b, s, KV_HEADS, ATT_HEAD_DIM)


def _attend_sample(qkvd3, cache_k, cache_v, page_table):
    ns, t, _ = qkvd3.shape
    npages = page_table.shape[1]
    past = npages * PAGE_SIZE
    pos = past + jnp.arange(t, dtype=jnp.int32)
    tables = tuple(jnp.tile(tb, (ns, 1)) for tb in _rope_tables(pos))
    qk = _rope_sample(qkvd3.reshape(ns * t, QKVD_WIDTH), tables)
    q_rot = qk[:, :ATT_WIDTH].reshape(ns, t, KV_HEADS, GQA_REP, ATT_HEAD_DIM)
    k_rot = qk[:, ATT_WIDTH:].reshape(ns, t, KV_WIDTH)
    v = qkvd3[:, :, ATT_WIDTH + KV_WIDTH:ATT_WIDTH + 2 * KV_WIDTH]
    q_ht = q_rot.transpose(0, 2, 3, 1, 4)
    eye = jnp.eye(KV_HEADS, dtype=F32)
    q_exp = (q_ht[:, :, :, :, None, :] * eye[None, :, None, None, :, None]).reshape(ns, ATT_HEADS * t, KV_WIDTH)
    blk_of_key = jnp.arange(past, dtype=jnp.int32) // MOBA_BLOCK
    e_mat = (jnp.arange(LANES, dtype=jnp.int32)[:, None] == blk_of_key[None, :]).astype(BF16)
    n_phys = cache_k.shape[0]
    to_pages = lambda c: jnp.transpose(c, (0, 2, 3, 1)).reshape(n_phys, KV_WIDTH, PAGE_SIZE)
    y_att = _moba_sample(page_table, q_exp.astype(BF16), k_rot, v, e_mat, to_pages(cache_k), to_pages(cache_v))
    return y_att, k_rot.reshape(ns, t, KV_HEADS, ATT_HEAD_DIM), v.reshape(ns, t, KV_HEADS, ATT_HEAD_DIM)


def _layer_params(l, norm1_w, w_in, ssd_conv_w, ssd_conv_b, dt_bias, a_log, d_skip, ssd_norm_w, w_ssd_out,
                  w_attn_out, w_mem_out, w_o, norm2_w, w_up, ffn_conv_w, ffn_conv_b, w_down, final_norm_w):
    w = w_in[l]
    o_z, o_x, o_dt = 0, D_INNER, D_INNER + CONV_DIM
    o_q = o_dt + SSD_HEADS
    o_k, o_v = o_q + ATT_WIDTH, o_q + ATT_WIDTH + KV_WIDTH
    o_qm = o_v + KV_WIDTH
    o_g = o_qm + MEM_WIDTH
    pad_lanes = lambda a: jnp.pad(a, (0, LANES - a.shape[0])).reshape(1, LANES)
    w_dt = jnp.pad(w[:, o_dt:o_q], ((0, 0), (0, LANES - SSD_HEADS)))
    return {
        "norm1_w": norm1_w[l],
        "w_zx": w[:, o_z:o_dt].astype(BF16),
        "w_qkvd": jnp.concatenate([w[:, o_q:o_qm], w_dt], axis=1).astype(BF16),
        "w_qm": w[:, o_qm:o_g].astype(BF16),
        "w_gates": w[:, o_g:].astype(BF16),
        "ssd_conv_w": ssd_conv_w[l],
        "ssd_conv_b": ssd_conv_b[l].reshape(1, CONV_DIM),
        "dt_bias": pad_lanes(dt_bias[l]),
        "a_log": pad_lanes(a_log[l]),
        "d_skip": jnp.repeat(d_skip[l], SSD_HEAD_DIM).reshape(1, D_INNER),
        "ssd_norm_w": ssd_norm_w[l].reshape(1, D_INNER),
        "w_ssd_out": w_ssd_out[l].astype(BF16),
        "w_attn_out": w_attn_out[l].astype(BF16),
        "w_mem_out": w_mem_out[l].astype(BF16),
        "w_o": w_o[l].astype(BF16),
        "norm2_w": norm2_w[l],
        "w_up": w_up[l].astype(BF16),
        "ffn_conv_w": ffn_conv_w[l],
        "ffn_conv_b": ffn_conv_b[l].reshape(1, 2 * D_FF),
        "w_down": w_down[l].astype(BF16),
        "final_norm_w": final_norm_w,
    }


def kernel(x_prompt, x_sample, cache_k, cache_v, cache_mem_k, cache_mem_v, state_ssm, state_conv,
           state_ffn_conv, page_table, mem_prompt, norm1_w, w_in, ssd_conv_w, ssd_conv_b, dt_bias, a_log,
           d_skip, ssd_norm_w, mem_norm_w, w_mem_kv, w_ssd_out, w_attn_out, w_mem_out, w_o, norm2_w, w_up,
           ffn_conv_w, ffn_conv_b, w_down, final_norm_w):
    depth = w_in.shape[0]
    assert depth == 1, "the final RMSNorm is fused into the single layer's last kernel"
    b_p, s_p, _ = x_prompt.shape
    n_mem = mem_prompt.shape[1]
    ns = x_sample.shape[0]
    l = 0
    p = _layer_params(l, norm1_w, w_in, ssd_conv_w, ssd_conv_b, dt_bias, a_log, d_skip, ssd_norm_w, w_ssd_out,
                      w_attn_out, w_mem_out, w_o, norm2_w, w_up, ffn_conv_w, ffn_conv_b, w_down, final_norm_w)

    memx = mem_prompt.reshape(b_p * n_mem, D_MODEL)
    wkv = w_mem_kv[l].astype(BF16)
    mk_p = _norm_matmul(memx, mem_norm_w[l], wkv[:, :MEM_WIDTH], tn=MEM_WIDTH, name="mem_k")
    mv_p = _norm_matmul(memx, mem_norm_w[l], wkv[:, MEM_WIDTH:], tn=MEM_WIDTH, name="mem_v")
    mk_p3 = mk_p.reshape(b_p, n_mem, MEM_WIDTH)
    mv_p3 = mv_p.reshape(b_p, n_mem, MEM_WIDTH)
    ssm0 = jnp.zeros((b_p, SSD_HEADS, SSD_HEAD_DIM, SSD_STATE), F32)
    conv0 = jnp.zeros((b_p, SSD_CONV - 1, CONV_DIM), F32)
    ffn0 = jnp.zeros((b_p, FFN_CONV - 1, 2 * D_FF), F32)
    y_p, k_p, v_p, s_p_new, c_p, f_p = _layer(x_prompt, mk_p3, mv_p3, ssm0, conv0, ffn0, _attend_prompt, p)

    attend_s = functools.partial(_attend_sample, cache_k=cache_k[l], cache_v=cache_v[l], page_table=page_table)
    y_s, k_s, v_s, s_s_new, c_s, f_s = _layer(x_sample, cache_mem_k[l], cache_mem_v[l], state_ssm[l], state_conv[l],
                                              state_ffn_conv[l], attend_s, p)

    mem_shape = (1, b_p, n_mem, MEM_HEADS, MEM_HEAD_DIM)
    return (y_p, y_s, k_p[None], v_p[None], mk_p.reshape(mem_shape), mv_p.reshape(mem_shape),
            s_p_new[None], c_p[None], f_p[None], k_s[None], v_s[None], s_s_new[None], c_s[None], f_s[None])
```

```python
import functools
import math

import jax
import jax.numpy as jnp
from jax import lax
from jax.experimental import pallas as pl
from jax.experimental.pallas import tpu as pltpu

F32 = jnp.float32
BF16 = jnp.bfloat16

D_MODEL = 1024
D_INNER = 2048
SSD_HEAD_DIM = 64
SSD_HEADS = 32
SSD_GROUPS = 8
SSD_STATE = 128
SSD_CONV = 4
SSD_CHUNK = 128
CONV_DIM = 4096
ATT_HEADS = 16
ATT_HEAD_DIM = 64
KV_HEADS = 4
GQA_REP = 4
ATT_WIDTH = 1024
KV_WIDTH = 256
MOBA_BLOCK = 256
MOBA_TOPK = 3
ROT_DIM = 16
ROPE_THETA = 500000.0
MEM_HEADS = 4
MEM_HEAD_DIM = 256
MEM_WIDTH = 1024
D_FF = 2816
FFN_CONV = 3
RMS_EPS = 1e-6
NEG_INF = -1e30
PAGE_SIZE = 128

LANES = 128
SUBLANES = 8
VMEM_LIMIT = 56 * 1024 * 1024

QKVD_WIDTH = ATT_WIDTH + 2 * KV_WIDTH + LANES
DT_COL_BLOCK = (ATT_WIDTH + 2 * KV_WIDTH) // LANES


def _cparams(sem):
    return pltpu.CompilerParams(dimension_semantics=sem, vmem_limit_bytes=VMEM_LIMIT)


def _dot(a, b):
    return jnp.dot(a, b, preferred_element_type=F32)


def _dot_nt(a, b):
    return lax.dot_general(a, b, (((1,), (1,)), ((), ())), preferred_element_type=F32)


def _dot_f32(a, b):
    return jnp.dot(a, b, preferred_element_type=F32, precision=lax.Precision.HIGHEST)


def _silu(x):
    return x * jax.nn.sigmoid(x)


def _norm_matmul_kernel(x_ref, nw_ref, w_ref, o_ref, xn_ref, *, act):
    @pl.when(pl.program_id(1) == 0)
    def _():
        x = x_ref[...]
        ms = jnp.mean(x * x, axis=-1, keepdims=True)
        xn_ref[...] = (x * lax.rsqrt(ms + RMS_EPS) * nw_ref[...]).astype(BF16)

    y = _dot(xn_ref[...], w_ref[...])
    if act == "sigmoid":
        y = jax.nn.sigmoid(y)
    o_ref[...] = y.astype(o_ref.dtype)


def _norm_matmul(x, nw, w, *, tn, out_dtype=F32, act=None, name="norm_matmul"):
    t, d = x.shape
    n = w.shape[1]
    tm = min(t, 1024)
    assert t % tm == 0 and n % tn == 0
    return pl.pallas_call(
        functools.partial(_norm_matmul_kernel, act=act),
        grid=(t // tm, n // tn),
        in_specs=[pl.BlockSpec((tm, d), lambda i, j: (i, 0)),
                  pl.BlockSpec((1, d), lambda i, j: (0, 0)),
                  pl.BlockSpec((d, tn), lambda i, j: (0, j))],
        out_specs=pl.BlockSpec((tm, tn), lambda i, j: (i, j)),
        out_shape=jax.ShapeDtypeStruct((t, n), out_dtype),
        scratch_shapes=[pltpu.VMEM((tm, d), BF16)],
        compiler_params=_cparams(("parallel", "arbitrary")),
        name=name,
    )(x, nw.reshape(1, d), w)


def _shift_rows(x, k, halo):
    r = pltpu.roll(x, k, axis=0)
    row = lax.broadcasted_iota(jnp.int32, halo.shape, 0)
    top = jnp.where(row < k, pltpu.roll(halo, k, axis=0), r[:SUBLANES])
    if x.shape[0] == SUBLANES:
        return top
    return jnp.concatenate([top, r[SUBLANES:]], axis=0)


def _ssd_kernel(z_ref, xa_ref, xb_ref, dt_ref, cst_ref, s0_ref, cw_ref, cb_ref, dtb_ref, alog_ref,
                dsk_ref, nw_ref, y_ref, sf_ref, s_ref, carry_ref, ybuf_ref, *, qb, q, nchunks):
    c = pl.program_id(1)

    @pl.when(c == 0)
    def _():
        s_ref[...] = s0_ref[0]
        carry_ref[...] = cst_ref[0]

    pre = jnp.concatenate([xa_ref[0], xb_ref[0]], axis=1).astype(F32)
    halo = carry_ref[...]
    acc = _shift_rows(pre, 3, halo) * cw_ref[0:1, :]
    acc = acc + _shift_rows(pre, 2, halo) * cw_ref[1:2, :]
    acc = acc + _shift_rows(pre, 1, halo) * cw_ref[2:3, :]
    acc = acc + pre * cw_ref[3:4, :]
    xbc = _silu(acc + cb_ref[...])
    if nchunks > 1:
        carry_ref[...] = pre[qb - SUBLANES:qb, :]

    z = z_ref[0].astype(F32)
    dt_raw = dt_ref[0]
    if qb < q:
        xbc = jnp.concatenate([xbc, jnp.zeros((q - qb, CONV_DIM), F32)], axis=0)
        dt_raw = jnp.concatenate([dt_raw, jnp.zeros((q - qb, LANES), F32)], axis=0)

    xs = xbc[:, :D_INNER]
    xs_bf = xs.astype(BF16)
    xs_t = xs.T

    v = dt_raw + dtb_ref[...]
    dt = jnp.maximum(v, 0.0) + jnp.log1p(jnp.exp(-jnp.abs(v)))
    row_q = lax.broadcasted_iota(jnp.int32, (q, LANES), 0)
    if qb < q:
        dt = jnp.where(row_q < qb, dt, 0.0)
    a = -jnp.exp(alog_ref[...])
    ri = lax.broadcasted_iota(jnp.int32, (q, q), 0)
    ci = lax.broadcasted_iota(jnp.int32, (q, q), 1)
    causal = ci <= ri
    cs = _dot_f32(causal.astype(F32), dt * a)
    cs_t = cs.T
    dt_t = dt.T
    ecs = jnp.exp(cs)

    for g in range(SSD_GROUPS):
        b_g = xbc[:, D_INNER + g * SSD_STATE:D_INNER + (g + 1) * SSD_STATE].astype(BF16)
        c_g = xbc[:qb, D_INNER + SSD_GROUPS * SSD_STATE + g * SSD_STATE:
                  D_INNER + SSD_GROUPS * SSD_STATE + (g + 1) * SSD_STATE].astype(BF16)
        cb = _dot_nt(c_g, b_g)
        for r in range(SSD_HEADS // SSD_GROUPS):
            h = g * (SSD_HEADS // SSD_GROUPS) + r
            lo = h * SSD_HEAD_DIM
            cs_col = cs[:qb, h:h + 1]
            cs_row = cs_t[h:h + 1, :]
            dt_row = dt_t[h:h + 1, :]
            seg = jnp.where(causal[:qb], cs_col - cs_row, -jnp.inf)
            m_h = (cb * jnp.exp(seg) * dt_row).astype(BF16)
            s_h = s_ref[h]
            y_h = _dot(m_h, xs_bf[:, lo:lo + SSD_HEAD_DIM])
            y_h = y_h + _dot_nt(c_g, s_h.astype(BF16)) * ecs[:qb, h:h + 1]
            ybuf_ref[:, lo:lo + SSD_HEAD_DIM] = y_h
            cs_end = cs_row[:, q - 1:q]
            w_row = dt_row * jnp.exp(cs_end - cs_row)
            xw = (xs_t[lo:lo + SSD_HEAD_DIM, :] * w_row).astype(BF16)
            s_ref[h] = s_h * jnp.exp(cs_end) + _dot(xw, b_g)

    y = ybuf_ref[...] + dsk_ref[...] * xs[:qb]
    gt = y * _silu(z)
    gw = D_INNER // SSD_GROUPS
    parts = []
    for g in range(SSD_GROUPS):
        gg = gt[:, g * gw:(g + 1) * gw]
        parts.append(gg * lax.rsqrt(jnp.mean(gg * gg, axis=-1, keepdims=True) + RMS_EPS))
    yn = jnp.concatenate(parts, axis=1) * nw_ref[...]
    y_ref[0] = yn.astype(y_ref.dtype)

    @pl.when(c == nchunks - 1)
    def _():
        sf_ref[0] = s_ref[...]


def _ssd(zx3, qkvd3, conv_state8, ssm0, cw, cb, dtb, alog, dsk, nw):
    nb, l, _ = zx3.shape
    q = SSD_CHUNK
    qb = min(l, q)
    assert l % qb == 0 and qb % SUBLANES == 0
    nchunks = l // qb
    half = CONV_DIM // 2
    row_blk = lambda col: (lambda b, c: (b, c, col))
    full2 = lambda b, c: (0, 0)
    return pl.pallas_call(
        functools.partial(_ssd_kernel, qb=qb, q=q, nchunks=nchunks),
        grid=(nb, nchunks),
        in_specs=[pl.BlockSpec((1, qb, D_INNER), row_blk(0)),
                  pl.BlockSpec((1, qb, half), row_blk(1)),
                  pl.BlockSpec((1, qb, half), row_blk(2)),
                  pl.BlockSpec((1, qb, LANES), row_blk(DT_COL_BLOCK)),
                  pl.BlockSpec((1, SUBLANES, CONV_DIM), lambda b, c: (b, 0, 0)),
                  pl.BlockSpec((1, SSD_HEADS, SSD_HEAD_DIM, SSD_STATE), lambda b, c: (b, 0, 0, 0)),
                  pl.BlockSpec((SSD_CONV, CONV_DIM), full2),
                  pl.BlockSpec((1, CONV_DIM), full2),
                  pl.BlockSpec((1, LANES), full2),
                  pl.BlockSpec((1, LANES), full2),
                  pl.BlockSpec((1, D_INNER), full2),
                  pl.BlockSpec((1, D_INNER), full2)],
        out_specs=[pl.BlockSpec((1, qb, D_INNER), lambda b, c: (b, c, 0)),
                   pl.BlockSpec((1, SSD_HEADS, SSD_HEAD_DIM, SSD_STATE), lambda b, c: (b, 0, 0, 0))],
        out_shape=[jax.ShapeDtypeStruct((nb, l, D_INNER), BF16),
                   jax.ShapeDtypeStruct((nb, SSD_HEADS, SSD_HEAD_DIM, SSD_STATE), F32)],
        scratch_shapes=[pltpu.VMEM((SSD_HEADS, SSD_HEAD_DIM, SSD_STATE), F32),
                        pltpu.VMEM((SUBLANES, CONV_DIM), F32),
                        pltpu.VMEM((qb, D_INNER), F32)],
        compiler_params=_cparams(("parallel", "arbitrary")),
        name="ssd",
    )(zx3, zx3, zx3, qkvd3, conv_state8, ssm0, cw, cb, dtb, alog, dsk, nw)


def _rope_tables(pos):
    half = ROT_DIM // 2
    inv = ROPE_THETA ** (-jnp.arange(half, dtype=F32) * 2.0 / ROT_DIM)
    ang = pos.astype(F32)[:, None] * inv[None, :]
    cos, sin = jnp.cos(ang), jnp.sin(ang)
    n = pos.shape[0]
    pad = jnp.zeros((n, ATT_HEAD_DIM - ROT_DIM), F32)
    zero = jnp.zeros((n, half), F32)
    c_head = jnp.concatenate([cos, cos, pad + 1.0], axis=1)
    s1_head = jnp.concatenate([-sin, zero, pad], axis=1)
    s2_head = jnp.concatenate([zero, sin, pad], axis=1)
    rep = LANES // ATT_HEAD_DIM
    return jnp.tile(c_head, (1, rep)), jnp.tile(s1_head, (1, rep)), jnp.tile(s2_head, (1, rep))


def _rope_group(xg, c, s1, s2):
    half = ROT_DIM // 2
    return xg * c + pltpu.roll(xg, LANES - half, axis=1) * s1 + pltpu.roll(xg, half, axis=1) * s2


def _head_from_group(xg, odd, lane):
    if odd:
        xg = pltpu.roll(xg, ATT_HEAD_DIM, axis=1)
    return jnp.where(lane < ATT_HEAD_DIM, xg, 0.0)


def _prompt_prep_kernel(q_ref, k_ref, v_ref, c_ref, s1_ref, s2_ref,
                        krot_ref, qt_ref, kaug_ref, vt_ref, kmean_ref):
    i = pl.program_id(1)
    rows = q_ref.shape[1]
    c, s1, s2 = c_ref[...], s1_ref[...], s2_ref[...]
    lane = lax.broadcasted_iota(jnp.int32, (rows, LANES), 1)
    scale = ATT_HEAD_DIM ** -0.5
    hd = ATT_HEAD_DIM

    @pl.when(i == 0)
    def _():
        kmean_ref[...] = jnp.zeros_like(kmean_ref)

    for cg in range(ATT_WIDTH // LANES):
        qg = _rope_group(q_ref[0, :, cg * LANES:(cg + 1) * LANES], c, s1, s2) * scale
        qg_t = qg.T.astype(BF16)
        for odd in range(2):
            g, r = divmod(2 * cg + odd, GQA_REP)
            qt_ref[0, g, 0, :, r * rows:(r + 1) * rows] = qg_t[odd * hd:(odd + 1) * hd, :]

    onehot = jnp.where(lane == ATT_HEAD_DIM + i, 1.0, 0.0)
    for cg in range(KV_WIDTH // LANES):
        kg = _rope_group(k_ref[0, :, cg * LANES:(cg + 1) * LANES], c, s1, s2)
        krot_ref[0, :, cg * LANES:(cg + 1) * LANES] = kg
        vg_t = v_ref[0, :, cg * LANES:(cg + 1) * LANES].T.astype(BF16)
        for odd in range(2):
            g = 2 * cg + odd
            kh = _head_from_group(kg, odd, lane)
            kaug_ref[0, g] = (kh + onehot).astype(BF16)
            vt_ref[0, g] = vg_t[odd * hd:(odd + 1) * hd, :]
            kmean_ref[0, g, pl.ds(i, 1), :] = jnp.mean(kh, axis=0, keepdims=True)


def _prompt_prep(qkvd3, tables):
    b, s, _ = qkvd3.shape
    nblk = s // MOBA_BLOCK
    assert s % MOBA_BLOCK == 0 and nblk <= SUBLANES
    blk = MOBA_BLOCK
    tab = pl.BlockSpec((blk, LANES), lambda bi, i: (i, 0))
    return pl.pallas_call(
        _prompt_prep_kernel,
        grid=(b, nblk),
        in_specs=[pl.BlockSpec((1, blk, ATT_WIDTH), lambda bi, i: (bi, i, 0)),
                  pl.BlockSpec((1, blk, KV_WIDTH), lambda bi, i: (bi, i, ATT_WIDTH // KV_WIDTH)),
                  pl.BlockSpec((1, blk, KV_WIDTH), lambda bi, i: (bi, i, ATT_WIDTH // KV_WIDTH + 1)),
                  tab, tab, tab],
        out_specs=[pl.BlockSpec((1, blk, KV_WIDTH), lambda bi, i: (bi, i, 0)),
                   pl.BlockSpec((1, KV_HEADS, 1, ATT_HEAD_DIM, GQA_REP * blk), lambda bi, i: (bi, 0, i, 0, 0)),
                   pl.BlockSpec((1, KV_HEADS, blk, LANES), lambda bi, i: (bi, 0, i, 0)),
                   pl.BlockSpec((1, KV_HEADS, ATT_HEAD_DIM, blk), lambda bi, i: (bi, 0, 0, i)),
                   pl.BlockSpec((1, KV_HEADS, SUBLANES, LANES), lambda bi, i: (bi, 0, 0, 0))],
        out_shape=[jax.ShapeDtypeStruct((b, s, KV_WIDTH), F32),
                   jax.ShapeDtypeStruct((b, KV_HEADS, nblk, ATT_HEAD_DIM, GQA_REP * blk), BF16),
                   jax.ShapeDtypeStruct((b, KV_HEADS, s, LANES), BF16),
                   jax.ShapeDtypeStruct((b, KV_HEADS, ATT_HEAD_DIM, s), BF16),
                   jax.ShapeDtypeStruct((b, KV_HEADS, SUBLANES, LANES), F32)],
        compiler_params=_cparams(("parallel", "arbitrary")),
        name="prompt_prep",
    )(qkvd3, qkvd3, qkvd3, *tables)


def _topk_rows(gate, valid, row_f, rounds=MOBA_TOPK):
    g0 = jnp.where(valid, gate, -jnp.inf)
    sel = jnp.zeros(gate.shape, dtype=jnp.bool_)
    for _ in range(rounds):
        m = jnp.max(g0, axis=0, keepdims=True)
        idx = jnp.min(jnp.where(g0 == m, row_f, 1e9), axis=0, keepdims=True)
        pick = jnp.logical_and(row_f == idx, m > -jnp.inf)
        sel = jnp.logical_or(sel, pick)
        g0 = jnp.where(pick, -jnp.inf, g0)
    return sel


def _topk_lanes(gate, valid, lane_f, rounds=MOBA_TOPK):
    g0 = jnp.where(valid, gate, -jnp.inf)
    sel = jnp.zeros(gate.shape, dtype=jnp.bool_)
    for _ in range(rounds):
        m = jnp.max(g0, axis=1, keepdims=True)
        idx = jnp.min(jnp.where(g0 == m, lane_f, 1e9), axis=1, keepdims=True)
        pick = jnp.logical_and(lane_f == idx, m > -jnp.inf)
        sel = jnp.logical_or(sel, pick)
        g0 = jnp.where(pick, -jnp.inf, g0)
    return sel


def _moba_prompt_block(n_past, q_ref, k_ref, v_ref, km_ref, y_ref):
    blk = MOBA_BLOCK
    hd = ATT_HEAD_DIM
    nq = GQA_REP * blk
    qt = q_ref[0, 0, 0]
    q0 = jnp.concatenate([qt, jnp.zeros((LANES - hd, nq), BF16)], axis=0)
    k_own = k_ref[0, 0, n_past * blk:(n_past + 1) * blk, :]
    key_i = lax.broadcasted_iota(jnp.int32, (blk, nq), 0)
    q_i = lax.broadcasted_iota(jnp.int32, (blk, nq), 1) % blk
    s_own = jnp.where(key_i <= q_i, _dot(k_own, q0), NEG_INF)
    m = jnp.max(s_own, axis=0, keepdims=True)
    if n_past > 0:
        pad8 = jnp.zeros((SUBLANES, LANES), F32)
        gate = _dot(jnp.concatenate([km_ref[0, 0], pad8], axis=0).astype(BF16), q0)
        row = lax.broadcasted_iota(jnp.int32, gate.shape, 0)
        sel = _topk_rows(gate, row < n_past, row.astype(F32), rounds=min(MOBA_TOPK, n_past))
        selb = jnp.where(row < SUBLANES, jnp.where(sel, 0.0, NEG_INF), 0.0).astype(BF16)
        q_aug = jnp.concatenate([qt, selb, jnp.zeros((LANES - hd - 2 * SUBLANES, nq), BF16)], axis=0)
        s_past = _dot(k_ref[0, 0, 0:n_past * blk, :], q_aug)
        m = jnp.maximum(m, jnp.max(s_past, axis=0, keepdims=True))
    p_own = jnp.exp(s_own - m)
    den = jnp.sum(p_own, axis=0, keepdims=True)
    o = _dot(v_ref[0, 0, :, n_past * blk:(n_past + 1) * blk], p_own.astype(BF16))
    if n_past > 0:
        p_past = jnp.exp(s_past - m)
        den = den + jnp.sum(p_past, axis=0, keepdims=True)
        o = o + _dot(v_ref[0, 0, :, 0:n_past * blk], p_past.astype(BF16))
    o = o / den
    for pair in range(GQA_REP // 2):
        two = jnp.concatenate([o[:, 2 * pair * blk:(2 * pair + 1) * blk],
                               o[:, (2 * pair + 1) * blk:(2 * pair + 2) * blk]], axis=0)
        y_ref[:, pair * LANES:(pair + 1) * LANES] = two.T.astype(y_ref.dtype)


def _moba_prompt_kernel(q_ref, k_ref, v_ref, km_ref, y_ref, *, nblk):
    i = pl.program_id(2)
    for n_past in range(nblk):
        pl.when(i == n_past)(functools.partial(_moba_prompt_block, n_past, q_ref, k_ref, v_ref, km_ref, y_ref))


def _moba_prompt(q_t, k_aug, v_t, kmean):
    b, _, s, _ = k_aug.shape
    nblk = s // MOBA_BLOCK
    blk = MOBA_BLOCK
    return pl.pallas_call(
        functools.partial(_moba_prompt_kernel, nblk=nblk),
        grid=(b, KV_HEADS, nblk),
        in_specs=[pl.BlockSpec((1, 1, 1, ATT_HEAD_DIM, GQA_REP * blk), lambda bi, g, i: (bi, g, i, 0, 0)),
                  pl.BlockSpec((1, 1, s, LANES), lambda bi, g, i: (bi, g, 0, 0)),
                  pl.BlockSpec((1, 1, ATT_HEAD_DIM, s), lambda bi, g, i: (bi, g, 0, 0)),
                  pl.BlockSpec((1, 1, SUBLANES, LANES), lambda bi, g, i: (bi, g, 0, 0))],
        out_specs=pl.BlockSpec((blk, GQA_REP * ATT_HEAD_DIM), lambda bi, g, i: (bi * nblk + i, g)),
        out_shape=jax.ShapeDtypeStruct((b * s, ATT_WIDTH), BF16),
        compiler_params=_cparams(("parallel", "parallel", "arbitrary")),
        name="moba_prompt",
    )(q_t, k_aug, v_t, kmean)


def _rope_kernel(x_ref, c_ref, s1_ref, s2_ref, o_ref, *, q_groups):
    c, s1, s2 = c_ref[...], s1_ref[...], s2_ref[...]
    scale = ATT_HEAD_DIM ** -0.5
    for cg in range(x_ref.shape[1] // LANES):
        xg = _rope_group(x_ref[:, cg * LANES:(cg + 1) * LANES], c, s1, s2)
        if cg < q_groups:
            xg = xg * scale
        o_ref[:, cg * LANES:(cg + 1) * LANES] = xg


def _rope_sample(qkvd, tables):
    t = qkvd.shape[0]
    w = ATT_WIDTH + KV_WIDTH
    tab = pl.BlockSpec((t, LANES), lambda i: (0, 0))
    return pl.pallas_call(
        functools.partial(_rope_kernel, q_groups=ATT_WIDTH // LANES),
        grid=(1,),
        in_specs=[pl.BlockSpec((t, w), lambda i: (0, 0)), tab, tab, tab],
        out_specs=pl.BlockSpec((t, w), lambda i: (0, 0)),
        out_shape=jax.ShapeDtypeStruct((t, w), F32),
        compiler_params=_cparams(("arbitrary",)),
        name="rope_sample",
    )(qkvd, *tables)


def _moba_sample_kernel(pt_ref, qx_ref, kn_ref, vn_ref, e_ref, ck_ref, cv_ref, y_ref,
                        kbuf, vbuf, sem, *, npages, t):
    s = pl.program_id(0)
    ns = pl.num_programs(0)
    slot = s % 2

    def k_copy(seq, p, sl):
        return pltpu.make_async_copy(ck_ref.at[pt_ref[seq, p]], kbuf.at[sl, :, p * PAGE_SIZE:(p + 1) * PAGE_SIZE],
                                     sem.at[0, sl])

    def v_copy(seq, p, sl):
        return pltpu.make_async_copy(cv_ref.at[pt_ref[seq, p]], vbuf.at[sl, :, p * PAGE_SIZE:(p + 1) * PAGE_SIZE],
                                     sem.at[1, sl])

    def start_all(seq, sl):
        for p in range(npages):
            k_copy(seq, p, sl).start()
            v_copy(seq, p, sl).start()

    @pl.when(s == 0)
    def _():
        start_all(0, 0)

    @pl.when(s + 1 < ns)
    def _():
        start_all(s + 1, 1 - slot)

    for p in range(npages):
        k_copy(s, p, slot).wait()
        v_copy(s, p, slot).wait()

    past = npages * PAGE_SIZE
    nblk = past // MOBA_BLOCK
    rows = ATT_HEADS * t
    chunk = min(past, 2048)
    qx = qx_ref[0]

    s_raw = jnp.concatenate(
        [_dot(qx, kbuf[slot, :, c * chunk:(c + 1) * chunk].astype(BF16)) for c in range(past // chunk)], axis=1)
    lane = lax.broadcasted_iota(jnp.int32, (rows, LANES), 1)
    gate = jnp.zeros((rows, LANES), F32)
    for n in range(nblk):
        col = jnp.sum(s_raw[:, n * MOBA_BLOCK:(n + 1) * MOBA_BLOCK], axis=1, keepdims=True)
        gate = jnp.where(lane == n, col, gate)
    sel = _topk_lanes(gate, lane < nblk, lane.astype(F32))
    selb = jnp.where(sel, 0.0, NEG_INF).astype(BF16)
    s_past = s_raw + _dot(selb, e_ref[...])

    kn = jnp.concatenate([kn_ref[0], jnp.zeros((LANES - t, KV_WIDTH), F32)], axis=0).astype(BF16)
    vn = jnp.concatenate([vn_ref[0], jnp.zeros((LANES - t, KV_WIDTH), F32)], axis=0).astype(BF16)
    row = lax.broadcasted_iota(jnp.int32, (rows, LANES), 0)
    own_ok = lane <= (row % t)
    s_own = jnp.where(own_ok, _dot_nt(qx, kn), NEG_INF)

    m = jnp.maximum(jnp.max(s_past, axis=1, keepdims=True), jnp.max(s_own, axis=1, keepdims=True))
    p_past = jnp.exp(s_past - m)
    p_own = jnp.exp(s_own - m)
    den = jnp.sum(p_past, axis=1, keepdims=True) + jnp.sum(p_own, axis=1, keepdims=True)
    o = _dot(p_own.astype(BF16), vn)
    for c in range(past // chunk):
        o = o + _dot_nt(p_past[:, c * chunk:(c + 1) * chunk].astype(BF16),
                        vbuf[slot, :, c * chunk:(c + 1) * chunk].astype(BF16))
    o = o / den
    for h in range(ATT_HEADS):
        g = h // GQA_REP
        y_ref[0, :, h * ATT_HEAD_DIM:(h + 1) * ATT_HEAD_DIM] = (
            o[h * t:(h + 1) * t, g * ATT_HEAD_DIM:(g + 1) * ATT_HEAD_DIM].astype(y_ref.dtype))


def _moba_sample(page_table, q_exp, k_new, v_new, e_mat, cache_k, cache_v):
    ns, npages = page_table.shape
    t = k_new.shape[1]
    past = npages * PAGE_SIZE
    assert past % MOBA_BLOCK == 0 and past // MOBA_BLOCK <= LANES and t == SUBLANES
    rows = ATT_HEADS * t
    grid_spec = pltpu.PrefetchScalarGridSpec(
        num_scalar_prefetch=1,
        grid=(ns,),
        in_specs=[pl.BlockSpec((1, rows, KV_WIDTH), lambda s, pt: (s, 0, 0)),
                  pl.BlockSpec((1, t, KV_WIDTH), lambda s, pt: (s, 0, 0)),
                  pl.BlockSpec((1, t, KV_WIDTH), lambda s, pt: (s, 0, 0)),
                  pl.BlockSpec((LANES, past), lambda s, pt: (0, 0)),
                  pl.BlockSpec(memory_space=pl.ANY),
                  pl.BlockSpec(memory_space=pl.ANY)],
        out_specs=pl.BlockSpec((1, t, ATT_WIDTH), lambda s, pt: (s, 0, 0)),
        scratch_shapes=[pltpu.VMEM((2, KV_WIDTH, past), F32),
                        pltpu.VMEM((2, KV_WIDTH, past), F32),
                        pltpu.SemaphoreType.DMA((2, 2))],
    )
    return pl.pallas_call(
        functools.partial(_moba_sample_kernel, npages=npages, t=t),
        grid_spec=grid_spec,
        out_shape=jax.ShapeDtypeStruct((ns, t, ATT_WIDTH), BF16),
        compiler_params=_cparams(("arbitrary",)),
        name="moba_sample",
    )(page_table, q_exp, k_new, v_new, e_mat, cache_k, cache_v)


def _mem_attn_kernel(q_ref, mk_ref, mv_ref, y_ref, *, heads_split):
    scale = MEM_HEAD_DIM ** -0.5
    for h in range(MEM_HEADS):
        lo, hi = h * MEM_HEAD_DIM, (h + 1) * MEM_HEAD_DIM
        mk = mk_ref[0, :, h, :] if heads_split else mk_ref[0, :, lo:hi]
        mv = mv_ref[0, :, h, :] if heads_split else mv_ref[0, :, lo:hi]
        q = (q_ref[0, :, lo:hi].astype(F32) * scale).astype(BF16)
        s = _dot_nt(q, mk.astype(BF16))
        m = jnp.max(s, axis=1, keepdims=True)
        p = jnp.exp(s - m)
        den = jnp.sum(p, axis=1, keepdims=True)
        o = _dot(p.astype(BF16), mv.astype(BF16)) / den
        y_ref[0, :, lo:hi] = o.astype(y_ref.dtype)


def _mem_attn(qm3, mk, mv):
    nb, l, _ = qm3.shape
    m = mk.shape[1]
    tl = min(l, 512)
    assert l % tl == 0
    heads_split = mk.ndim == 4
    if heads_split:
        mem_spec = pl.BlockSpec((1, m, MEM_HEADS, MEM_HEAD_DIM), lambda b, i: (b, 0, 0, 0))
    else:
        mem_spec = pl.BlockSpec((1, m, MEM_WIDTH), lambda b, i: (b, 0, 0))
    return pl.pallas_call(
        functools.partial(_mem_attn_kernel, heads_split=heads_split),
        grid=(nb, l // tl),
        in_specs=[pl.BlockSpec((1, tl, MEM_WIDTH), lambda b, i: (b, i, 0)), mem_spec, mem_spec],
        out_specs=pl.BlockSpec((1, tl, MEM_WIDTH), lambda b, i: (b, i, 0)),
        out_shape=jax.ShapeDtypeStruct((nb, l, MEM_WIDTH), BF16),
        compiler_params=_cparams(("parallel", "arbitrary")),
        name="mem_attn",
    )(qm3, mk, mv)


def _merge_kernel(x_ref, gs_ref, ys_ref, ya_ref, ym_ref, ws_ref, wa_ref, wm_ref, wo_ref, h_ref):
    d = D_MODEL
    merged = gs_ref[:, 0:d].astype(F32) * _dot(ys_ref[...], ws_ref[...])
    merged = merged + gs_ref[:, d:2 * d].astype(F32) * _dot(ya_ref[...], wa_ref[...])
    merged = merged + gs_ref[:, 2 * d:3 * d].astype(F32) * _dot(ym_ref[...], wm_ref[...])
    h_ref[...] = x_ref[...] + _dot(merged.astype(BF16), wo_ref[...])


def _merge(x, gs, y_ssd, y_att, y_mem, ws, wa, wm, wo):
    t = x.shape[0]
    tm = min(t, 512)
    assert t % tm == 0
    rows = lambda w: pl.BlockSpec((tm, w), lambda i: (i, 0))
    full = lambda a: pl.BlockSpec(a.shape, lambda i: (0, 0))
    return pl.pallas_call(
        _merge_kernel,
        grid=(t // tm,),
        in_specs=[rows(D_MODEL), rows(3 * D_MODEL), rows(D_INNER), rows(ATT_WIDTH), rows(MEM_WIDTH),
                  full(ws), full(wa), full(wm), full(wo)],
        out_specs=rows(D_MODEL),
        out_shape=jax.ShapeDtypeStruct((t, D_MODEL), F32),
        compiler_params=_cparams(("parallel",)),
        name="merge",
    )(x, gs, y_ssd, y_att, y_mem, ws, wa, wm, wo)


def _ffn_gate_kernel(u_ref, st_ref, w_ref, b_ref, a_ref, carry_ref, *, tl, ntiles):
    i = pl.program_id(1)

    @pl.when(i == 0)
    def _():
        carry_ref[...] = st_ref[0]

    u = u_ref[0]
    halo = carry_ref[...]
    acc = _shift_rows(u, 2, halo) * w_ref[0:1, :]
    acc = acc + _shift_rows(u, 1, halo) * w_ref[1:2, :]
    acc = acc + u * w_ref[2:3, :]
    uc = acc + b_ref[...]
    if ntiles > 1:
        carry_ref[...] = u[tl - SUBLANES:tl, :]
    a_ref[0] = (_silu(uc[:, :D_FF]) * uc[:, D_FF:]).astype(a_ref.dtype)


def _ffn_gate(u3, state8, w, b):
    nb, l, c = u3.shape
    tl = min(l, 256)
    assert l % tl == 0 and tl % SUBLANES == 0
    ntiles = l // tl
    return pl.pallas_call(
        functools.partial(_ffn_gate_kernel, tl=tl, ntiles=ntiles),
        grid=(nb, ntiles),
        in_specs=[pl.BlockSpec((1, tl, c), lambda bi, i: (bi, i, 0)),
                  pl.BlockSpec((1, SUBLANES, c), lambda bi, i: (bi, 0, 0)),
                  pl.BlockSpec((FFN_CONV, c), lambda bi, i: (0, 0)),
                  pl.BlockSpec((1, c), lambda bi, i: (0, 0))],
        out_specs=pl.BlockSpec((1, tl, D_FF), lambda bi, i: (bi, i, 0)),
        out_shape=jax.ShapeDtypeStruct((nb, l, D_FF), BF16),
        scratch_shapes=[pltpu.VMEM((SUBLANES, c), F32)],
        compiler_params=_cparams(("parallel", "arbitrary")),
        name="ffn_gate",
    )(u3, state8, w, b)


def _ffn_fused_kernel(h_ref, st_ref, n2_ref, wu_ref, cw_ref, cb_ref, wd_ref, fw_ref, y_ref, last_ref, carry_ref,
                      *, tl, ntiles, ck):
    i = pl.program_id(1)

    @pl.when(i == 0)
    def _():
        carry_ref[...] = st_ref[0]

    h = h_ref[0]
    ms = jnp.mean(h * h, axis=-1, keepdims=True)
    hn = (h * lax.rsqrt(ms + RMS_EPS) * n2_ref[...]).astype(BF16)

    def conv(u, off):
        halo = carry_ref[:, off:off + ck]
        acc = _shift_rows(u, 2, halo) * cw_ref[0:1, off:off + ck]
        acc = acc + _shift_rows(u, 1, halo) * cw_ref[1:2, off:off + ck]
        acc = acc + u * cw_ref[2:3, off:off + ck]
        carry_ref[:, off:off + ck] = u[tl - SUBLANES:tl, :]
        return acc + cb_ref[:, off:off + ck]

    out = h
    for c in range(D_FF // ck):
        lo = c * ck
        ug = conv(_dot(hn, wu_ref[:, lo:lo + ck]), lo)
        uv = conv(_dot(hn, wu_ref[:, D_FF + lo:D_FF + lo + ck]), D_FF + lo)
        out = out + _dot((_silu(ug) * uv).astype(BF16), wd_ref[lo:lo + ck, :])
    ms = jnp.mean(out * out, axis=-1, keepdims=True)
    y_ref[0] = out * lax.rsqrt(ms + RMS_EPS) * fw_ref[...]

    @pl.when(i == ntiles - 1)
    def _():
        last_ref[0] = carry_ref[...]


def _ffn_fused(h3, state8, n2, wu, cw, cb, wd, fw):
    nb, l, d = h3.shape
    tl = 512
    ck = D_FF // 2
    assert l % tl == 0 and ck % LANES == 0
    ntiles = l // tl
    const = lambda a: pl.BlockSpec(a.shape, lambda bi, i: (0,) * a.ndim, pipeline_mode=pl.Buffered(1))
    n2, fw = n2.reshape(1, d), fw.reshape(1, d)
    return pl.pallas_call(
        functools.partial(_ffn_fused_kernel, tl=tl, ntiles=ntiles, ck=ck),
        grid=(nb, ntiles),
        in_specs=[pl.BlockSpec((1, tl, d), lambda bi, i: (bi, i, 0)),
                  pl.BlockSpec((1, SUBLANES, 2 * D_FF), lambda bi, i: (bi, 0, 0)),
                  const(n2), const(wu), const(cw), const(cb), const(wd), const(fw)],
        out_specs=[pl.BlockSpec((1, tl, d), lambda bi, i: (bi, i, 0)),
                   pl.BlockSpec((1, SUBLANES, 2 * D_FF), lambda bi, i: (bi, 0, 0))],
        out_shape=[jax.ShapeDtypeStruct((nb, l, d), F32),
                   jax.ShapeDtypeStruct((nb, SUBLANES, 2 * D_FF), F32)],
        scratch_shapes=[pltpu.VMEM((SUBLANES, 2 * D_FF), F32)],
        compiler_params=_cparams(("parallel", "arbitrary")),
        name="ffn_fused",
    )(h3, state8, n2, wu, cw, cb, wd, fw)


def _down_kernel(a_ref, w_ref, h_ref, nw_ref, y_ref):
    h = h_ref[...] + _dot(a_ref[...], w_ref[...])
    ms = jnp.mean(h * h, axis=-1, keepdims=True)
    y_ref[...] = h * lax.rsqrt(ms + RMS_EPS) * nw_ref[...]


def _down(act, w, h, nw):
    t = h.shape[0]
    tm = min(t, 512)
    assert t % tm == 0
    return pl.pallas_call(
        _down_kernel,
        grid=(t // tm,),
        in_specs=[pl.BlockSpec((tm, D_FF), lambda i: (i, 0)),
                  pl.BlockSpec((D_FF, D_MODEL), lambda i: (0, 0)),
                  pl.BlockSpec((tm, D_MODEL), lambda i: (i, 0)),
                  pl.BlockSpec((1, D_MODEL), lambda i: (0, 0))],
        out_specs=pl.BlockSpec((tm, D_MODEL), lambda i: (i, 0)),
        out_shape=jax.ShapeDtypeStruct((t, D_MODEL), F32),
        compiler_params=_cparams(("parallel",)),
        name="ffn_down",
    )(act, w, h, nw.reshape(1, D_MODEL))


def _pad_state_rows(state):
    nb, k, c = state.shape
    return jnp.concatenate([jnp.zeros((nb, SUBLANES - k, c), state.dtype), state], axis=1)


def _layer(x3, mem_k3, mem_v3, ssm0, conv_state, ffn_state, attend, p):
    nb, l, d = x3.shape
    t = nb * l
    x = x3.reshape(t, d)
    nw1 = p["norm1_w"]
    zx = _norm_matmul(x, nw1, p["w_zx"], tn=2048, out_dtype=BF16, name="proj_zx")
    qkvd = _norm_matmul(x, nw1, p["w_qkvd"], tn=QKVD_WIDTH, name="proj_qkvd")
    qm = _norm_matmul(x, nw1, p["w_qm"], tn=MEM_WIDTH, out_dtype=BF16, name="proj_qm")
    gs = _norm_matmul(x, nw1, p["w_gates"], tn=D_MODEL, out_dtype=BF16, act="sigmoid", name="proj_gates")

    zx3 = zx.reshape(nb, l, D_INNER + CONV_DIM)
    qkvd3 = qkvd.reshape(nb, l, QKVD_WIDTH)
    y_ssd, ssm_new = _ssd(zx3, qkvd3, _pad_state_rows(conv_state), ssm0, p["ssd_conv_w"], p["ssd_conv_b"],
                          p["dt_bias"], p["a_log"], p["d_skip"], p["ssd_norm_w"])
    conv_new = zx3[:, l - (SSD_CONV - 1):, D_INNER:].astype(F32)

    y_att, k_new, v_new = attend(qkvd3)
    y_mem = _mem_attn(qm.reshape(nb, l, MEM_WIDTH), mem_k3, mem_v3)

    h = _merge(x, gs, y_ssd.reshape(t, D_INNER), y_att.reshape(t, ATT_WIDTH), y_mem.reshape(t, MEM_WIDTH),
               p["w_ssd_out"], p["w_attn_out"], p["w_mem_out"], p["w_o"])

    if l % 512 == 0:
        y3, last8 = _ffn_fused(h.reshape(nb, l, d), _pad_state_rows(ffn_state), p["norm2_w"], p["w_up"],
                               p["ffn_conv_w"], p["ffn_conv_b"], p["w_down"], p["final_norm_w"])
        ffn_new = last8[:, SUBLANES - (FFN_CONV - 1):]
        return y3, k_new, v_new, ssm_new, conv_new, ffn_new

    u = _norm_matmul(h, p["norm2_w"], p["w_up"], tn=2 * D_FF // 11, name="ffn_up")
    u3 = u.reshape(nb, l, 2 * D_FF)
    ffn_new = u3[:, l - (FFN_CONV - 1):]
    act = _ffn_gate(u3, _pad_state_rows(ffn_state), p["ffn_conv_w"], p["ffn_conv_b"])
    y = _down(act.reshape(t, D_FF), p["w_down"], h, p["final_norm_w"])
    return y.reshape(nb, l, d), k_new, v_new, ssm_new, conv_new, ffn_new


def _attend_prompt(qkvd3):
    b, s, _ = qkvd3.shape
    tables = _rope_tables(jnp.arange(s, dtype=jnp.int32))
    k_rot, q_t, k_aug, v_t, kmean = _prompt_prep(qkvd3, tables)
    y_att = _moba_prompt(q_t, k_aug, v_t, kmean)
    v = qkvd3[:, :, ATT_WIDTH + KV_WIDTH:ATT_WIDTH + 2 * KV_WIDTH]
    return y_att, k_rot.reshape(b, s, KV_HEADS, ATT_HEAD_DIM), v.reshape(b, s, KV_HEADS, ATT_HEAD_DIM)


def _attend_sample(qkvd3, cache_k, cache_v, page_table):
    ns, t, _ = qkvd3.shape
    npages = page_table.shape[1]
    past = npages * PAGE_SIZE
    pos = past + jnp.arange(t, dtype=jnp.int32)
    tables = tuple(jnp.tile(tb, (ns, 1)) for tb in _rope_tables(pos))
    qk = _rope_sample(qkvd3.reshape(ns * t, QKVD_WIDTH), tables)
    q_rot = qk[:, :ATT_WIDTH].reshape(ns, t, KV_HEADS, GQA_REP, ATT_HEAD_DIM)
    k_rot = qk[:, ATT_WIDTH:].reshape(ns, t, KV_WIDTH)
    v = qkvd3[:, :, ATT_WIDTH + KV_WIDTH:ATT_WIDTH + 2 * KV_WIDTH]
    q_ht = q_rot.transpose(0, 2, 3, 1, 4)
    eye = jnp.eye(KV_HEADS, dtype=F32)
    q_exp = (q_ht[:, :, :, :, None, :] * eye[None, :, None, None, :, None]).reshape(ns, ATT_HEADS * t, KV_WIDTH)
    blk_of_key = jnp.arange(past, dtype=jnp.int32) // MOBA_BLOCK
    e_mat = (jnp.arange(LANES, dtype=jnp.int32)[:, None] == blk_of_key[None, :]).astype(BF16)
    n_phys = cache_k.shape[0]
    to_pages = lambda c: jnp.transpose(c, (0, 2, 3, 1)).reshape(n_phys, KV_WIDTH, PAGE_SIZE)
    y_att = _moba_sample(page_table, q_exp.astype(BF16), k_rot, v, e_mat, to_pages(cache_k), to_pages(cache_v))
    return y_att, k_rot.reshape(ns, t, KV_HEADS, ATT_HEAD_DIM), v.reshape(ns, t, KV_HEADS, ATT_HEAD_DIM)


def _layer_params(l, norm1_w, w_in, ssd_conv_w, ssd_conv_b, dt_bias, a_log, d_skip, ssd_norm_w, w_ssd_out,
                  w_attn_out, w_mem_out, w_o, norm2_w, w_up, ffn_conv_w, ffn_conv_b, w_down, final_norm_w):
    w = w_in[l]
    o_z, o_x, o_dt = 0, D_INNER, D_INNER + CONV_DIM
    o_q = o_dt + SSD_HEADS
    o_k, o_v = o_q + ATT_WIDTH, o_q + ATT_WIDTH + KV_WIDTH
    o_qm = o_v + KV_WIDTH
    o_g = o_qm + MEM_WIDTH
    pad_lanes = lambda a: jnp.pad(a, (0, LANES - a.shape[0])).reshape(1, LANES)
    w_dt = jnp.pad(w[:, o_dt:o_q], ((0, 0), (0, LANES - SSD_HEADS)))
    return {
        "norm1_w": norm1_w[l],
        "w_zx": w[:, o_z:o_dt].astype(BF16),
        "w_qkvd": jnp.concatenate([w[:, o_q:o_qm], w_dt], axis=1).astype(BF16),
        "w_qm": w[:, o_qm:o_g].astype(BF16),
        "w_gates": w[:, o_g:].astype(BF16),
        "ssd_conv_w": ssd_conv_w[l],
        "ssd_conv_b": ssd_conv_b[l].reshape(1, CONV_DIM),
        "dt_bias": pad_lanes(dt_bias[l]),
        "a_log": pad_lanes(a_log[l]),
        "d_skip": jnp.repeat(d_skip[l], SSD_HEAD_DIM).reshape(1, D_INNER),
        "ssd_norm_w": ssd_norm_w[l].reshape(1, D_INNER),
        "w_ssd_out": w_ssd_out[l].astype(BF16),
        "w_attn_out": w_attn_out[l].astype(BF16),
        "w_mem_out": w_mem_out[l].astype(BF16),
        "w_o": w_o[l].astype(BF16),
        "norm2_w": norm2_w[l],
        "w_up": w_up[l].astype(BF16),
        "ffn_conv_w": ffn_conv_w[l],
        "ffn_conv_b": ffn_conv_b[l].reshape(1, 2 * D_FF),
        "w_down": w_down[l].astype(BF16),
        "final_norm_w": final_norm_w,
    }


def kernel(x_prompt, x_sample, cache_k, cache_v, cache_mem_k, cache_mem_v, state_ssm, state_conv,
           state_ffn_conv, page_table, mem_prompt, norm1_w, w_in, ssd_conv_w, ssd_conv_b, dt_bias, a_log,
           d_skip, ssd_norm_w, mem_norm_w, w_mem_kv, w_ssd_out, w_attn_out, w_mem_out, w_o, norm2_w, w_up,
           ffn_conv_w, ffn_conv_b, w_down, final_norm_w):
    depth = w_in.shape[0]
    assert depth == 1, "the final RMSNorm is fused into the single layer's last kernel"
    b_p, s_p, _ = x_prompt.shape
    n_mem = mem_prompt.shape[1]
    ns = x_sample.shape[0]
    l = 0
    p = _layer_params(l, norm1_w, w_in, ssd_conv_w, ssd_conv_b, dt_bias, a_log, d_skip, ssd_norm_w, w_ssd_out,
                      w_attn_out, w_mem_out, w_o, norm2_w, w_up, ffn_conv_w, ffn_conv_b, w_down, final_norm_w)

    memx = mem_prompt.reshape(b_p * n_mem, D_MODEL)
    wkv = w_mem_kv[l].astype(BF16)
    mk_p = _norm_matmul(memx, mem_norm_w[l], wkv[:, :MEM_WIDTH], tn=MEM_WIDTH, name="mem_k")
    mv_p = _norm_matmul(memx, mem_norm_w[l], wkv[:, MEM_WIDTH:], tn=MEM_WIDTH, name="mem_v")
    mk_p3 = mk_p.reshape(b_p, n_mem, MEM_WIDTH)
    mv_p3 = mv_p.reshape(b_p, n_mem, MEM_WIDTH)
    ssm0 = jnp.zeros((b_p, SSD_HEADS, SSD_HEAD_DIM, SSD_STATE), F32)
    conv0 = jnp.zeros((b_p, SSD_CONV - 1, CONV_DIM), F32)
    ffn0 = jnp.zeros((b_p, FFN_CONV - 1, 2 * D_FF), F32)
    y_p, k_p, v_p, s_p_new, c_p, f_p = _layer(x_prompt, mk_p3, mv_p3, ssm0, conv0, ffn0, _attend_prompt, p)

    attend_s = functools.partial(_attend_sample, cache_k=cache_k[l], cache_v=cache_v[l], page_table=page_table)
    mk_s3 = cache_mem_k[l].reshape(ns, -1, MEM_WIDTH)
    mv_s3 = cache_mem_v[l].reshape(ns, -1, MEM_WIDTH)
    y_s, k_s, v_s, s_s_new, c_s, f_s = _layer(x_sample, mk_s3, mv_s3, state_ssm[l], state_conv[l],
                                              state_ffn_conv[l], attend_s, p)

    mem_shape = (1, b_p, n_mem, MEM_HEADS, MEM_HEAD_DIM)
    return (y_p, y_s, k_p[None], v_p[None], mk_p.reshape(mem_shape), mv_p.reshape(mem_shape),
            s_p_new[None], c_p[None], f_p[None], k_s[None], v_s[None], s_s_new[None], c_s[None], f_s[None])
```

```python
import functools
import math

import jax
import jax.numpy as jnp
from jax import lax
from jax.experimental import pallas as pl
from jax.experimental.pallas import tpu as pltpu

F32 = jnp.float32
BF16 = jnp.bfloat16

D_MODEL = 1024
D_INNER = 2048
SSD_HEAD_DIM = 64
SSD_HEADS = 32
SSD_GROUPS = 8
SSD_STATE = 128
SSD_CONV = 4
SSD_CHUNK = 128
CONV_DIM = 4096
ATT_HEADS = 16
ATT_HEAD_DIM = 64
KV_HEADS = 4
GQA_REP = 4
ATT_WIDTH = 1024
KV_WIDTH = 256
MOBA_BLOCK = 256
MOBA_TOPK = 3
ROT_DIM = 16
ROPE_THETA = 500000.0
MEM_HEADS = 4
MEM_HEAD_DIM = 256
MEM_WIDTH = 1024
D_FF = 2816
FFN_CONV = 3
RMS_EPS = 1e-6
NEG_INF = -1e30
PAGE_SIZE = 128

LANES = 128
SUBLANES = 8
VMEM_LIMIT = 56 * 1024 * 1024

QKVD_WIDTH = ATT_WIDTH + 2 * KV_WIDTH + LANES
DT_COL_BLOCK = (ATT_WIDTH + 2 * KV_WIDTH) // LANES


def _cparams(sem):
    return pltpu.CompilerParams(dimension_semantics=sem, vmem_limit_bytes=VMEM_LIMIT)


def _dot(a, b):
    return jnp.dot(a, b, preferred_element_type=F32)


def _dot_nt(a, b):
    return lax.dot_general(a, b, (((1,), (1,)), ((), ())), preferred_element_type=F32)


def _dot_f32(a, b):
    return jnp.dot(a, b, preferred_element_type=F32, precision=lax.Precision.HIGHEST)


def _silu(x):
    return x * jax.nn.sigmoid(x)


def _norm_matmul_kernel(x_ref, nw_ref, w_ref, o_ref, xn_ref, *, act):
    @pl.when(pl.program_id(1) == 0)
    def _():
        x = x_ref[...]
        ms = jnp.mean(x * x, axis=-1, keepdims=True)
        xn_ref[...] = (x * lax.rsqrt(ms + RMS_EPS) * nw_ref[...]).astype(BF16)

    y = _dot(xn_ref[...], w_ref[...])
    if act == "sigmoid":
        y = jax.nn.sigmoid(y)
    o_ref[...] = y.astype(o_ref.dtype)


def _norm_matmul(x, nw, w, *, tn, out_dtype=F32, act=None, name="norm_matmul"):
    t, d = x.shape
    n = w.shape[1]
    tm = min(t, 1024)
    assert t % tm == 0 and n % tn == 0
    return pl.pallas_call(
        functools.partial(_norm_matmul_kernel, act=act),
        grid=(t // tm, n // tn),
        in_specs=[pl.BlockSpec((tm, d), lambda i, j: (i, 0)),
                  pl.BlockSpec((1, d), lambda i, j: (0, 0)),
                  pl.BlockSpec((d, tn), lambda i, j: (0, j))],
        out_specs=pl.BlockSpec((tm, tn), lambda i, j: (i, j)),
        out_shape=jax.ShapeDtypeStruct((t, n), out_dtype),
        scratch_shapes=[pltpu.VMEM((tm, d), BF16)],
        compiler_params=_cparams(("parallel", "arbitrary")),
        name=name,
    )(x, nw.reshape(1, d), w)


def _shift_rows(x, k, halo):
    r = pltpu.roll(x, k, axis=0)
    row = lax.broadcasted_iota(jnp.int32, halo.shape, 0)
    top = jnp.where(row < k, pltpu.roll(halo, k, axis=0), r[:SUBLANES])
    if x.shape[0] == SUBLANES:
        return top
    return jnp.concatenate([top, r[SUBLANES:]], axis=0)


def _ssd_kernel(z_ref, xa_ref, xb_ref, dt_ref, cst_ref, s0_ref, cw_ref, cb_ref, dtb_ref, alog_ref,
                dsk_ref, nw_ref, y_ref, sf_ref, s_ref, carry_ref, ybuf_ref, *, qb, q, nchunks):
    c = pl.program_id(1)

    @pl.when(c == 0)
    def _():
        s_ref[...] = s0_ref[0]
        carry_ref[...] = cst_ref[0]

    pre = jnp.concatenate([xa_ref[0], xb_ref[0]], axis=1).astype(F32)
    halo = carry_ref[...]
    w0, w1, w2, w3 = (cw_ref[k:k + 1, :] for k in range(SSD_CONV))
    prev = _shift_rows(pre, 1, halo)
    halo_pair = halo * w1 + pltpu.roll(halo, 1, axis=0) * w0
    acc = _shift_rows(pre * w1 + prev * w0, 2, halo_pair) + (prev * w2 + pre * w3)
    xbc = _silu(acc + cb_ref[...])
    if nchunks > 1:
        carry_ref[...] = pre[qb - SUBLANES:qb, :]

    z = z_ref[0].astype(F32)
    dt_raw = dt_ref[0]
    if qb < q:
        xbc = jnp.concatenate([xbc, jnp.zeros((q - qb, CONV_DIM), F32)], axis=0)
        dt_raw = jnp.concatenate([dt_raw, jnp.zeros((q - qb, LANES), F32)], axis=0)

    xs = xbc[:, :D_INNER]
    xs_bf = xs.astype(BF16)
    xs_t = xs.T

    v = dt_raw + dtb_ref[...]
    dt = jnp.maximum(v, 0.0) + jnp.log1p(jnp.exp(-jnp.abs(v)))
    row_q = lax.broadcasted_iota(jnp.int32, (q, LANES), 0)
    if qb < q:
        dt = jnp.where(row_q < qb, dt, 0.0)
    a = -jnp.exp(alog_ref[...])
    ri = lax.broadcasted_iota(jnp.int32, (q, q), 0)
    ci = lax.broadcasted_iota(jnp.int32, (q, q), 1)
    causal = ci <= ri
    cs = _dot_f32(causal.astype(F32), dt * a)
    cs_t = cs.T
    dt_t = dt.T
    ecs = jnp.exp(cs)

    for g in range(SSD_GROUPS):
        b_g = xbc[:, D_INNER + g * SSD_STATE:D_INNER + (g + 1) * SSD_STATE].astype(BF16)
        c_g = xbc[:qb, D_INNER + SSD_GROUPS * SSD_STATE + g * SSD_STATE:
                  D_INNER + SSD_GROUPS * SSD_STATE + (g + 1) * SSD_STATE].astype(BF16)
        cb = _dot_nt(c_g, b_g)
        for r in range(SSD_HEADS // SSD_GROUPS):
            h = g * (SSD_HEADS // SSD_GROUPS) + r
            lo = h * SSD_HEAD_DIM
            cs_col = cs[:qb, h:h + 1]
            cs_row = cs_t[h:h + 1, :]
            dt_row = dt_t[h:h + 1, :]
            seg = jnp.where(causal[:qb], cs_col - cs_row, -jnp.inf)
            m_h = (cb * jnp.exp(seg) * dt_row).astype(BF16)
            s_h = s_ref[h]
            y_h = _dot(m_h, xs_bf[:, lo:lo + SSD_HEAD_DIM])
            y_h = y_h + _dot_nt(c_g, s_h.astype(BF16)) * ecs[:qb, h:h + 1]
            ybuf_ref[:, lo:lo + SSD_HEAD_DIM] = y_h
            cs_end = cs_row[:, q - 1:q]
            w_row = dt_row * jnp.exp(cs_end - cs_row)
            xw = (xs_t[lo:lo + SSD_HEAD_DIM, :] * w_row).astype(BF16)
            s_ref[h] = s_h * jnp.exp(cs_end) + _dot(xw, b_g)

    y = ybuf_ref[...] + dsk_ref[...] * xs[:qb]
    gt = y * _silu(z)
    gw = D_INNER // SSD_GROUPS
    parts = []
    for g in range(SSD_GROUPS):
        gg = gt[:, g * gw:(g + 1) * gw]
        parts.append(gg * lax.rsqrt(jnp.mean(gg * gg, axis=-1, keepdims=True) + RMS_EPS))
    yn = jnp.concatenate(parts, axis=1) * nw_ref[...]
    y_ref[0] = yn.astype(y_ref.dtype)

    @pl.when(c == nchunks - 1)
    def _():
        sf_ref[0] = s_ref[...]


def _ssd(zx3, qkvd3, conv_state8, ssm0, cw, cb, dtb, alog, dsk, nw):
    nb, l, _ = zx3.shape
    q = SSD_CHUNK
    qb = min(l, q)
    assert l % qb == 0 and qb % SUBLANES == 0
    nchunks = l // qb
    half = CONV_DIM // 2
    row_blk = lambda col: (lambda b, c: (b, c, col))
    full2 = lambda b, c: (0, 0)
    return pl.pallas_call(
        functools.partial(_ssd_kernel, qb=qb, q=q, nchunks=nchunks),
        grid=(nb, nchunks),
        in_specs=[pl.BlockSpec((1, qb, D_INNER), row_blk(0)),
                  pl.BlockSpec((1, qb, half), row_blk(1)),
                  pl.BlockSpec((1, qb, half), row_blk(2)),
                  pl.BlockSpec((1, qb, LANES), row_blk(DT_COL_BLOCK)),
                  pl.BlockSpec((1, SUBLANES, CONV_DIM), lambda b, c: (b, 0, 0)),
                  pl.BlockSpec((1, SSD_HEADS, SSD_HEAD_DIM, SSD_STATE), lambda b, c: (b, 0, 0, 0)),
                  pl.BlockSpec((SSD_CONV, CONV_DIM), full2),
                  pl.BlockSpec((1, CONV_DIM), full2),
                  pl.BlockSpec((1, LANES), full2),
                  pl.BlockSpec((1, LANES), full2),
                  pl.BlockSpec((1, D_INNER), full2),
                  pl.BlockSpec((1, D_INNER), full2)],
        out_specs=[pl.BlockSpec((1, qb, D_INNER), lambda b, c: (b, c, 0)),
                   pl.BlockSpec((1, SSD_HEADS, SSD_HEAD_DIM, SSD_STATE), lambda b, c: (b, 0, 0, 0))],
        out_shape=[jax.ShapeDtypeStruct((nb, l, D_INNER), BF16),
                   jax.ShapeDtypeStruct((nb, SSD_HEADS, SSD_HEAD_DIM, SSD_STATE), F32)],
        scratch_shapes=[pltpu.VMEM((SSD_HEADS, SSD_HEAD_DIM, SSD_STATE), F32),
                        pltpu.VMEM((SUBLANES, CONV_DIM), F32),
                        pltpu.VMEM((qb, D_INNER), F32)],
        compiler_params=_cparams(("parallel", "arbitrary")),
        name="ssd",
    )(zx3, zx3, zx3, qkvd3, conv_state8, ssm0, cw, cb, dtb, alog, dsk, nw)


def _rope_tables(pos):
    half = ROT_DIM // 2
    inv = ROPE_THETA ** (-jnp.arange(half, dtype=F32) * 2.0 / ROT_DIM)
    ang = pos.astype(F32)[:, None] * inv[None, :]
    cos, sin = jnp.cos(ang), jnp.sin(ang)
    n = pos.shape[0]
    pad = jnp.zeros((n, ATT_HEAD_DIM - ROT_DIM), F32)
    zero = jnp.zeros((n, half), F32)
    c_head = jnp.concatenate([cos, cos, pad + 1.0], axis=1)
    s1_head = jnp.concatenate([-sin, zero, pad], axis=1)
    s2_head = jnp.concatenate([zero, sin, pad], axis=1)
    rep = LANES // ATT_HEAD_DIM
    return jnp.tile(c_head, (1, rep)), jnp.tile(s1_head, (1, rep)), jnp.tile(s2_head, (1, rep))


def _rope_group(xg, c, s1, s2):
    half = ROT_DIM // 2
    return xg * c + pltpu.roll(xg, LANES - half, axis=1) * s1 + pltpu.roll(xg, half, axis=1) * s2


def _head_from_group(xg, odd, lane):
    if odd:
        xg = pltpu.roll(xg, ATT_HEAD_DIM, axis=1)
    return jnp.where(lane < ATT_HEAD_DIM, xg, 0.0)


def _prompt_prep_kernel(q_ref, k_ref, v_ref, c_ref, s1_ref, s2_ref,
                        krot_ref, qt_ref, kaug_ref, vt_ref, kmean_ref):
    i = pl.program_id(1)
    rows = q_ref.shape[1]
    c, s1, s2 = c_ref[...], s1_ref[...], s2_ref[...]
    lane = lax.broadcasted_iota(jnp.int32, (rows, LANES), 1)
    scale = ATT_HEAD_DIM ** -0.5
    hd = ATT_HEAD_DIM

    @pl.when(i == 0)
    def _():
        kmean_ref[...] = jnp.zeros_like(kmean_ref)

    for cg in range(ATT_WIDTH // LANES):
        qg = _rope_group(q_ref[0, :, cg * LANES:(cg + 1) * LANES], c, s1, s2) * scale
        qg_t = qg.T.astype(BF16)
        for odd in range(2):
            g, r = divmod(2 * cg + odd, GQA_REP)
            qt_ref[0, g, 0, :, r * rows:(r + 1) * rows] = qg_t[odd * hd:(odd + 1) * hd, :]

    onehot = jnp.where(lane == ATT_HEAD_DIM + i, 1.0, 0.0)
    for cg in range(KV_WIDTH // LANES):
        kg = _rope_group(k_ref[0, :, cg * LANES:(cg + 1) * LANES], c, s1, s2)
        krot_ref[0, :, cg * LANES:(cg + 1) * LANES] = kg
        vg_t = v_ref[0, :, cg * LANES:(cg + 1) * LANES].T.astype(BF16)
        for odd in range(2):
            g = 2 * cg + odd
            kh = _head_from_group(kg, odd, lane)
            kaug_ref[0, g] = (kh + onehot).astype(BF16)
            vt_ref[0, g] = vg_t[odd * hd:(odd + 1) * hd, :]
            kmean_ref[0, g, pl.ds(i, 1), :] = jnp.mean(kh, axis=0, keepdims=True)


def _prompt_prep(qkvd3, tables):
    b, s, _ = qkvd3.shape
    nblk = s // MOBA_BLOCK
    assert s % MOBA_BLOCK == 0 and nblk <= SUBLANES
    blk = MOBA_BLOCK
    tab = pl.BlockSpec((blk, LANES), lambda bi, i: (i, 0))
    return pl.pallas_call(
        _prompt_prep_kernel,
        grid=(b, nblk),
        in_specs=[pl.BlockSpec((1, blk, ATT_WIDTH), lambda bi, i: (bi, i, 0)),
                  pl.BlockSpec((1, blk, KV_WIDTH), lambda bi, i: (bi, i, ATT_WIDTH // KV_WIDTH)),
                  pl.BlockSpec((1, blk, KV_WIDTH), lambda bi, i: (bi, i, ATT_WIDTH // KV_WIDTH + 1)),
                  tab, tab, tab],
        out_specs=[pl.BlockSpec((1, blk, KV_WIDTH), lambda bi, i: (bi, i, 0)),
                   pl.BlockSpec((1, KV_HEADS, 1, ATT_HEAD_DIM, GQA_REP * blk), lambda bi, i: (bi, 0, i, 0, 0)),
                   pl.BlockSpec((1, KV_HEADS, blk, LANES), lambda bi, i: (bi, 0, i, 0)),
                   pl.BlockSpec((1, KV_HEADS, ATT_HEAD_DIM, blk), lambda bi, i: (bi, 0, 0, i)),
                   pl.BlockSpec((1, KV_HEADS, SUBLANES, LANES), lambda bi, i: (bi, 0, 0, 0))],
        out_shape=[jax.ShapeDtypeStruct((b, s, KV_WIDTH), F32),
                   jax.ShapeDtypeStruct((b, KV_HEADS, nblk, ATT_HEAD_DIM, GQA_REP * blk), BF16),
                   jax.ShapeDtypeStruct((b, KV_HEADS, s, LANES), BF16),
                   jax.ShapeDtypeStruct((b, KV_HEADS, ATT_HEAD_DIM, s), BF16),
                   jax.ShapeDtypeStruct((b, KV_HEADS, SUBLANES, LANES), F32)],
        compiler_params=_cparams(("parallel", "arbitrary")),
        name="prompt_prep",
    )(qkvd3, qkvd3, qkvd3, *tables)


def _topk_rows(gate, valid, row_f, rounds=MOBA_TOPK):
    g0 = jnp.where(valid, gate, -jnp.inf)
    sel = jnp.zeros(gate.shape, dtype=jnp.bool_)
    for _ in range(rounds):
        m = jnp.max(g0, axis=0, keepdims=True)
        idx = jnp.min(jnp.where(g0 == m, row_f, 1e9), axis=0, keepdims=True)
        pick = jnp.logical_and(row_f == idx, m > -jnp.inf)
        sel = jnp.logical_or(sel, pick)
        g0 = jnp.where(pick, -jnp.inf, g0)
    return sel


def _topk_lanes(gate, valid, lane_f, rounds=MOBA_TOPK):
    g0 = jnp.where(valid, gate, -jnp.inf)
    sel = jnp.zeros(gate.shape, dtype=jnp.bool_)
    for _ in range(rounds):
        m = jnp.max(g0, axis=1, keepdims=True)
        idx = jnp.min(jnp.where(g0 == m, lane_f, 1e9), axis=1, keepdims=True)
        pick = jnp.logical_and(lane_f == idx, m > -jnp.inf)
        sel = jnp.logical_or(sel, pick)
        g0 = jnp.where(pick, -jnp.inf, g0)
    return sel


def _moba_prompt_block(n_past, q_ref, k_ref, v_ref, km_ref, y_ref):
    blk = MOBA_BLOCK
    hd = ATT_HEAD_DIM
    nq = GQA_REP * blk
    qt = q_ref[0, 0, 0]
    q0 = jnp.concatenate([qt, jnp.zeros((LANES - hd, nq), BF16)], axis=0)
    k_own = k_ref[0, 0, n_past * blk:(n_past + 1) * blk, :]
    key_i = lax.broadcasted_iota(jnp.int32, (blk, nq), 0)
    q_i = lax.broadcasted_iota(jnp.int32, (blk, nq), 1) % blk
    s_own = jnp.where(key_i <= q_i, _dot(k_own, q0), NEG_INF)
    m = jnp.max(s_own, axis=0, keepdims=True)
    if n_past > 0:
        pad8 = jnp.zeros((SUBLANES, LANES), F32)
        gate = _dot(jnp.concatenate([km_ref[0, 0], pad8], axis=0).astype(BF16), q0)
        row = lax.broadcasted_iota(jnp.int32, gate.shape, 0)
        sel = _topk_rows(gate, row < n_past, row.astype(F32), rounds=min(MOBA_TOPK, n_past))
        selb = jnp.where(row < SUBLANES, jnp.where(sel, 0.0, NEG_INF), 0.0).astype(BF16)
        q_aug = jnp.concatenate([qt, selb, jnp.zeros((LANES - hd - 2 * SUBLANES, nq), BF16)], axis=0)
        s_past = _dot(k_ref[0, 0, 0:n_past * blk, :], q_aug)
        m = jnp.maximum(m, jnp.max(s_past, axis=0, keepdims=True))
    p_own = jnp.exp(s_own - m)
    den = jnp.sum(p_own, axis=0, keepdims=True)
    o = _dot(v_ref[0, 0, :, n_past * blk:(n_past + 1) * blk], p_own.astype(BF16))
    if n_past > 0:
        p_past = jnp.exp(s_past - m)
        den = den + jnp.sum(p_past, axis=0, keepdims=True)
        o = o + _dot(v_ref[0, 0, :, 0:n_past * blk], p_past.astype(BF16))
    o = o / den
    for pair in range(GQA_REP // 2):
        two = jnp.concatenate([o[:, 2 * pair * blk:(2 * pair + 1) * blk],
                               o[:, (2 * pair + 1) * blk:(2 * pair + 2) * blk]], axis=0)
        y_ref[:, pair * LANES:(pair + 1) * LANES] = two.T.astype(y_ref.dtype)


def _moba_prompt_kernel(q_ref, k_ref, v_ref, km_ref, y_ref, *, nblk):
    i = pl.program_id(2)
    for n_past in range(nblk):
        pl.when(i == n_past)(functools.partial(_moba_prompt_block, n_past, q_ref, k_ref, v_ref, km_ref, y_ref))


def _moba_prompt(q_t, k_aug, v_t, kmean):
    b, _, s, _ = k_aug.shape
    nblk = s // MOBA_BLOCK
    blk = MOBA_BLOCK
    return pl.pallas_call(
        functools.partial(_moba_prompt_kernel, nblk=nblk),
        grid=(b, KV_HEADS, nblk),
        in_specs=[pl.BlockSpec((1, 1, 1, ATT_HEAD_DIM, GQA_REP * blk), lambda bi, g, i: (bi, g, i, 0, 0)),
                  pl.BlockSpec((1, 1, s, LANES), lambda bi, g, i: (bi, g, 0, 0)),
                  pl.BlockSpec((1, 1, ATT_HEAD_DIM, s), lambda bi, g, i: (bi, g, 0, 0)),
                  pl.BlockSpec((1, 1, SUBLANES, LANES), lambda bi, g, i: (bi, g, 0, 0))],
        out_specs=pl.BlockSpec((blk, GQA_REP * ATT_HEAD_DIM), lambda bi, g, i: (bi * nblk + i, g)),
        out_shape=jax.ShapeDtypeStruct((b * s, ATT_WIDTH), BF16),
        compiler_params=_cparams(("parallel", "parallel", "arbitrary")),
        name="moba_prompt",
    )(q_t, k_aug, v_t, kmean)


def _rope_kernel(x_ref, c_ref, s1_ref, s2_ref, o_ref, *, q_groups):
    c, s1, s2 = c_ref[...], s1_ref[...], s2_ref[...]
    scale = ATT_HEAD_DIM ** -0.5
    for cg in range(x_ref.shape[1] // LANES):
        xg = _rope_group(x_ref[:, cg * LANES:(cg + 1) * LANES], c, s1, s2)
        if cg < q_groups:
            xg = xg * scale
        o_ref[:, cg * LANES:(cg + 1) * LANES] = xg


def _rope_sample(qkvd, tables):
    t = qkvd.shape[0]
    w = ATT_WIDTH + KV_WIDTH
    tab = pl.BlockSpec((t, LANES), lambda i: (0, 0))
    return pl.pallas_call(
        functools.partial(_rope_kernel, q_groups=ATT_WIDTH // LANES),
        grid=(1,),
        in_specs=[pl.BlockSpec((t, w), lambda i: (0, 0)), tab, tab, tab],
        out_specs=pl.BlockSpec((t, w), lambda i: (0, 0)),
        out_shape=jax.ShapeDtypeStruct((t, w), F32),
        compiler_params=_cparams(("arbitrary",)),
        name="rope_sample",
    )(qkvd, *tables)


def _moba_sample_kernel(pt_ref, qx_ref, kn_ref, vn_ref, e_ref, ck_ref, cv_ref, y_ref,
                        kbuf, vbuf, sem, *, npages, t):
    s = pl.program_id(0)
    ns = pl.num_programs(0)
    slot = s % 2

    def k_copy(seq, p, sl):
        return pltpu.make_async_copy(ck_ref.at[pt_ref[seq, p]], kbuf.at[sl, :, p * PAGE_SIZE:(p + 1) * PAGE_SIZE],
                                     sem.at[0, sl])

    def v_copy(seq, p, sl):
        return pltpu.make_async_copy(cv_ref.at[pt_ref[seq, p]], vbuf.at[sl, :, p * PAGE_SIZE:(p + 1) * PAGE_SIZE],
                                     sem.at[1, sl])

    def start_all(seq, sl):
        for p in range(npages):
            k_copy(seq, p, sl).start()
            v_copy(seq, p, sl).start()

    @pl.when(s == 0)
    def _():
        start_all(0, 0)

    @pl.when(s + 1 < ns)
    def _():
        start_all(s + 1, 1 - slot)

    for p in range(npages):
        k_copy(s, p, slot).wait()
        v_copy(s, p, slot).wait()

    past = npages * PAGE_SIZE
    nblk = past // MOBA_BLOCK
    rows = ATT_HEADS * t
    chunk = min(past, 2048)
    qx = qx_ref[0]

    s_raw = jnp.concatenate(
        [_dot(qx, kbuf[slot, :, c * chunk:(c + 1) * chunk].astype(BF16)) for c in range(past // chunk)], axis=1)
    lane = lax.broadcasted_iota(jnp.int32, (rows, LANES), 1)
    gate = jnp.zeros((rows, LANES), F32)
    for n in range(nblk):
        col = jnp.sum(s_raw[:, n * MOBA_BLOCK:(n + 1) * MOBA_BLOCK], axis=1, keepdims=True)
        gate = jnp.where(lane == n, col, gate)
    sel = _topk_lanes(gate, lane < nblk, lane.astype(F32))
    selb = jnp.where(sel, 0.0, NEG_INF).astype(BF16)
    s_past = s_raw + _dot(selb, e_ref[...])

    kn = jnp.concatenate([kn_ref[0], jnp.zeros((LANES - t, KV_WIDTH), F32)], axis=0).astype(BF16)
    vn = jnp.concatenate([vn_ref[0], jnp.zeros((LANES - t, KV_WIDTH), F32)], axis=0).astype(BF16)
    row = lax.broadcasted_iota(jnp.int32, (rows, LANES), 0)
    own_ok = lane <= (row % t)
    s_own = jnp.where(own_ok, _dot_nt(qx, kn), NEG_INF)

    m = jnp.maximum(jnp.max(s_past, axis=1, keepdims=True), jnp.max(s_own, axis=1, keepdims=True))
    p_past = jnp.exp(s_past - m)
    p_own = jnp.exp(s_own - m)
    den = jnp.sum(p_past, axis=1, keepdims=True) + jnp.sum(p_own, axis=1, keepdims=True)
    o = _dot(p_own.astype(BF16), vn)
    for c in range(past // chunk):
        o = o + _dot_nt(p_past[:, c * chunk:(c + 1) * chunk].astype(BF16),
                        vbuf[slot, :, c * chunk:(c + 1) * chunk].astype(BF16))
    o = o / den
    for h in range(ATT_HEADS):
        g = h // GQA_REP
        y_ref[0, :, h * ATT_HEAD_DIM:(h + 1) * ATT_HEAD_DIM] = (
            o[h * t:(h + 1) * t, g * ATT_HEAD_DIM:(g + 1) * ATT_HEAD_DIM].astype(y_ref.dtype))


def _moba_sample(page_table, q_exp, k_new, v_new, e_mat, cache_k, cache_v):
    ns, npages = page_table.shape
    t = k_new.shape[1]
    past = npages * PAGE_SIZE
    assert past % MOBA_BLOCK == 0 and past // MOBA_BLOCK <= LANES and t == SUBLANES
    rows = ATT_HEADS * t
    grid_spec = pltpu.PrefetchScalarGridSpec(
        num_scalar_prefetch=1,
        grid=(ns,),
        in_specs=[pl.BlockSpec((1, rows, KV_WIDTH), lambda s, pt: (s, 0, 0)),
                  pl.BlockSpec((1, t, KV_WIDTH), lambda s, pt: (s, 0, 0)),
                  pl.BlockSpec((1, t, KV_WIDTH), lambda s, pt: (s, 0, 0)),
                  pl.BlockSpec((LANES, past), lambda s, pt: (0, 0)),
                  pl.BlockSpec(memory_space=pl.ANY),
                  pl.BlockSpec(memory_space=pl.ANY)],
        out_specs=pl.BlockSpec((1, t, ATT_WIDTH), lambda s, pt: (s, 0, 0)),
        scratch_shapes=[pltpu.VMEM((2, KV_WIDTH, past), F32),
                        pltpu.VMEM((2, KV_WIDTH, past), F32),
                        pltpu.SemaphoreType.DMA((2, 2))],
    )
    return pl.pallas_call(
        functools.partial(_moba_sample_kernel, npages=npages, t=t),
        grid_spec=grid_spec,
        out_shape=jax.ShapeDtypeStruct((ns, t, ATT_WIDTH), BF16),
        compiler_params=_cparams(("arbitrary",)),
        name="moba_sample",
    )(page_table, q_exp, k_new, v_new, e_mat, cache_k, cache_v)


def _mem_head(q, mk, mv):
    q = (q.astype(F32) * MEM_HEAD_DIM ** -0.5).astype(BF16)
    s = _dot_nt(q, mk.astype(BF16))
    m = jnp.max(s, axis=1, keepdims=True)
    p = jnp.exp(s - m)
    den = jnp.sum(p, axis=1, keepdims=True)
    return _dot(p.astype(BF16), mv.astype(BF16)) / den


def _mem_attn_kernel(q_ref, mk_ref, mv_ref, y_ref):
    for h in range(MEM_HEADS):
        lo, hi = h * MEM_HEAD_DIM, (h + 1) * MEM_HEAD_DIM
        y_ref[0, :, lo:hi] = _mem_head(q_ref[0, :, lo:hi], mk_ref[0, :, lo:hi], mv_ref[0, :, lo:hi]).astype(y_ref.dtype)


def _mem_attn_paged_kernel(q_ref, mk_hbm, mv_hbm, y_ref, kbuf, vbuf, sem):
    s = pl.program_id(0)
    ns = pl.num_programs(0)
    slot = s % 2

    def copies(seq, sl):
        out = []
        for h in range(MEM_HEADS):
            out.append(pltpu.make_async_copy(mk_hbm.at[seq, :, h, :], kbuf.at[sl, h], sem.at[0, sl]))
            out.append(pltpu.make_async_copy(mv_hbm.at[seq, :, h, :], vbuf.at[sl, h], sem.at[1, sl]))
        return out

    @pl.when(s == 0)
    def _():
        for c in copies(0, 0):
            c.start()

    @pl.when(s + 1 < ns)
    def _():
        for c in copies(s + 1, 1 - slot):
            c.start()

    for c in copies(s, slot):
        c.wait()
    for h in range(MEM_HEADS):
        lo, hi = h * MEM_HEAD_DIM, (h + 1) * MEM_HEAD_DIM
        y_ref[0, :, lo:hi] = _mem_head(q_ref[0, :, lo:hi], kbuf[slot, h], vbuf[slot, h]).astype(y_ref.dtype)


def _mem_attn_paged(qm3, mk4, mv4):
    ns, l, _ = qm3.shape
    m = mk4.shape[1]
    return pl.pallas_call(
        _mem_attn_paged_kernel,
        grid=(ns,),
        in_specs=[pl.BlockSpec((1, l, MEM_WIDTH), lambda s: (s, 0, 0)),
                  pl.BlockSpec(memory_space=pl.ANY), pl.BlockSpec(memory_space=pl.ANY)],
        out_specs=pl.BlockSpec((1, l, MEM_WIDTH), lambda s: (s, 0, 0)),
        out_shape=jax.ShapeDtypeStruct((ns, l, MEM_WIDTH), BF16),
        scratch_shapes=[pltpu.VMEM((2, MEM_HEADS, m, MEM_HEAD_DIM), F32),
                        pltpu.VMEM((2, MEM_HEADS, m, MEM_HEAD_DIM), F32),
                        pltpu.SemaphoreType.DMA((2, 2))],
        compiler_params=_cparams(("arbitrary",)),
        name="mem_attn_paged",
    )(qm3, mk4, mv4)


def _mem_attn(qm3, mk, mv):
    nb, l, _ = qm3.shape
    if mk.ndim == 4:
        return _mem_attn_paged(qm3, mk, mv)
    m = mk.shape[1]
    tl = min(l, 512)
    assert l % tl == 0
    mem_spec = pl.BlockSpec((1, m, MEM_WIDTH), lambda b, i: (b, 0, 0))
    return pl.pallas_call(
        _mem_attn_kernel,
        grid=(nb, l // tl),
        in_specs=[pl.BlockSpec((1, tl, MEM_WIDTH), lambda b, i: (b, i, 0)), mem_spec, mem_spec],
        out_specs=pl.BlockSpec((1, tl, MEM_WIDTH), lambda b, i: (b, i, 0)),
        out_shape=jax.ShapeDtypeStruct((nb, l, MEM_WIDTH), BF16),
        compiler_params=_cparams(("parallel", "arbitrary")),
        name="mem_attn",
    )(qm3, mk, mv)


def _merge_kernel(x_ref, gs_ref, ys_ref, ya_ref, ym_ref, ws_ref, wa_ref, wm_ref, wo_ref, h_ref):
    d = D_MODEL
    merged = gs_ref[:, 0:d].astype(F32) * _dot(ys_ref[...], ws_ref[...])
    merged = merged + gs_ref[:, d:2 * d].astype(F32) * _dot(ya_ref[...], wa_ref[...])
    merged = merged + gs_ref[:, 2 * d:3 * d].astype(F32) * _dot(ym_ref[...], wm_ref[...])
    h_ref[...] = x_ref[...] + _dot(merged.astype(BF16), wo_ref[...])


def _merge(x, gs, y_ssd, y_att, y_mem, ws, wa, wm, wo):
    t = x.shape[0]
    tm = min(t, 512)
    assert t % tm == 0
    rows = lambda w: pl.BlockSpec((tm, w), lambda i: (i, 0))
    full = lambda a: pl.BlockSpec(a.shape, lambda i: (0, 0))
    return pl.pallas_call(
        _merge_kernel,
        grid=(t // tm,),
        in_specs=[rows(D_MODEL), rows(3 * D_MODEL), rows(D_INNER), rows(ATT_WIDTH), rows(MEM_WIDTH),
                  full(ws), full(wa), full(wm), full(wo)],
        out_specs=rows(D_MODEL),
        out_shape=jax.ShapeDtypeStruct((t, D_MODEL), F32),
        compiler_params=_cparams(("parallel",)),
        name="merge",
    )(x, gs, y_ssd, y_att, y_mem, ws, wa, wm, wo)


def _ffn_gate_kernel(u_ref, st_ref, w_ref, b_ref, a_ref, carry_ref, *, tl, ntiles):
    i = pl.program_id(1)

    @pl.when(i == 0)
    def _():
        carry_ref[...] = st_ref[0]

    u = u_ref[0]
    halo = carry_ref[...]
    acc = _shift_rows(u, 2, halo) * w_ref[0:1, :]
    acc = acc + _shift_rows(u, 1, halo) * w_ref[1:2, :]
    acc = acc + u * w_ref[2:3, :]
    uc = acc + b_ref[...]
    if ntiles > 1:
        carry_ref[...] = u[tl - SUBLANES:tl, :]
    a_ref[0] = (_silu(uc[:, :D_FF]) * uc[:, D_FF:]).astype(a_ref.dtype)


def _ffn_gate(u3, state8, w, b):
    nb, l, c = u3.shape
    tl = min(l, 256)
    assert l % tl == 0 and tl % SUBLANES == 0
    ntiles = l // tl
    return pl.pallas_call(
        functools.partial(_ffn_gate_kernel, tl=tl, ntiles=ntiles),
        grid=(nb, ntiles),
        in_specs=[pl.BlockSpec((1, tl, c), lambda bi, i: (bi, i, 0)),
                  pl.BlockSpec((1, SUBLANES, c), lambda bi, i: (bi, 0, 0)),
                  pl.BlockSpec((FFN_CONV, c), lambda bi, i: (0, 0)),
                  pl.BlockSpec((1, c), lambda bi, i: (0, 0))],
        out_specs=pl.BlockSpec((1, tl, D_FF), lambda bi, i: (bi, i, 0)),
        out_shape=jax.ShapeDtypeStruct((nb, l, D_FF), BF16),
        scratch_shapes=[pltpu.VMEM((SUBLANES, c), F32)],
        compiler_params=_cparams(("parallel", "arbitrary")),
        name="ffn_gate",
    )(u3, state8, w, b)


def _ffn_fused_kernel(h_ref, st_ref, n2_ref, wu_ref, cw_ref, cb_ref, wd_ref, fw_ref, y_ref, last_ref, carry_ref,
                      *, tl, ntiles, ck):
    i = pl.program_id(1)

    @pl.when(i == 0)
    def _():
        carry_ref[...] = st_ref[0]

    h = h_ref[0]
    ms = jnp.mean(h * h, axis=-1, keepdims=True)
    hn = (h * lax.rsqrt(ms + RMS_EPS) * n2_ref[...]).astype(BF16)

    def conv(u, off):
        halo = carry_ref[:, off:off + ck]
        acc = _shift_rows(u, 2, halo) * cw_ref[0:1, off:off + ck]
        acc = acc + _shift_rows(u, 1, halo) * cw_ref[1:2, off:off + ck]
        acc = acc + u * cw_ref[2:3, off:off + ck]
        carry_ref[:, off:off + ck] = u[tl - SUBLANES:tl, :]
        return acc + cb_ref[:, off:off + ck]

    out = h
    for c in range(D_FF // ck):
        lo = c * ck
        ug = conv(_dot(hn, wu_ref[:, lo:lo + ck]), lo)
        uv = conv(_dot(hn, wu_ref[:, D_FF + lo:D_FF + lo + ck]), D_FF + lo)
        out = out + _dot((_silu(ug) * uv).astype(BF16), wd_ref[lo:lo + ck, :])
    ms = jnp.mean(out * out, axis=-1, keepdims=True)
    y_ref[0] = out * lax.rsqrt(ms + RMS_EPS) * fw_ref[...]

    @pl.when(i == ntiles - 1)
    def _():
        last_ref[0] = carry_ref[...]


def _ffn_fused(h3, state8, n2, wu, cw, cb, wd, fw):
    nb, l, d = h3.shape
    tl = 512
    ck = D_FF // 2
    assert l % tl == 0 and ck % LANES == 0
    ntiles = l // tl
    const = lambda a: pl.BlockSpec(a.shape, lambda bi, i: (0,) * a.ndim, pipeline_mode=pl.Buffered(1))
    n2, fw = n2.reshape(1, d), fw.reshape(1, d)
    return pl.pallas_call(
        functools.partial(_ffn_fused_kernel, tl=tl, ntiles=ntiles, ck=ck),
        grid=(nb, ntiles),
        in_specs=[pl.BlockSpec((1, tl, d), lambda bi, i: (bi, i, 0)),
                  pl.BlockSpec((1, SUBLANES, 2 * D_FF), lambda bi, i: (bi, 0, 0)),
                  const(n2), const(wu), const(cw), const(cb), const(wd), const(fw)],
        out_specs=[pl.BlockSpec((1, tl, d), lambda bi, i: (bi, i, 0)),
                   pl.BlockSpec((1, SUBLANES, 2 * D_FF), lambda bi, i: (bi, 0, 0))],
        out_shape=[jax.ShapeDtypeStruct((nb, l, d), F32),
                   jax.ShapeDtypeStruct((nb, SUBLANES, 2 * D_FF), F32)],
        scratch_shapes=[pltpu.VMEM((SUBLANES, 2 * D_FF), F32)],
        compiler_params=_cparams(("parallel", "arbitrary")),
        name="ffn_fused",
    )(h3, state8, n2, wu, cw, cb, wd, fw)


def _down_kernel(a_ref, w_ref, h_ref, nw_ref, y_ref):
    h = h_ref[...] + _dot(a_ref[...], w_ref[...])
    ms = jnp.mean(h * h, axis=-1, keepdims=True)
    y_ref[...] = h * lax.rsqrt(ms + RMS_EPS) * nw_ref[...]


def _down(act, w, h, nw):
    t = h.shape[0]
    tm = min(t, 512)
    assert t % tm == 0
    return pl.pallas_call(
        _down_kernel,
        grid=(t // tm,),
        in_specs=[pl.BlockSpec((tm, D_FF), lambda i: (i, 0)),
                  pl.BlockSpec((D_FF, D_MODEL), lambda i: (0, 0)),
                  pl.BlockSpec((tm, D_MODEL), lambda i: (i, 0)),
                  pl.BlockSpec((1, D_MODEL), lambda i: (0, 0))],
        out_specs=pl.BlockSpec((tm, D_MODEL), lambda i: (i, 0)),
        out_shape=jax.ShapeDtypeStruct((t, D_MODEL), F32),
        compiler_params=_cparams(("parallel",)),
        name="ffn_down",
    )(act, w, h, nw.reshape(1, D_MODEL))


def _pad_state_rows(state):
    nb, k, c = state.shape
    return jnp.concatenate([jnp.zeros((nb, SUBLANES - k, c), state.dtype), state], axis=1)


def _layer(x3, mem_k3, mem_v3, ssm0, conv_state, ffn_state, attend, p):
    nb, l, d = x3.shape
    t = nb * l
    x = x3.reshape(t, d)
    nw1 = p["norm1_w"]
    zx = _norm_matmul(x, nw1, p["w_zx"], tn=2048, out_dtype=BF16, name="proj_zx")
    qkvd = _norm_matmul(x, nw1, p["w_qkvd"], tn=QKVD_WIDTH, name="proj_qkvd")
    qm = _norm_matmul(x, nw1, p["w_qm"], tn=MEM_WIDTH, out_dtype=BF16, name="proj_qm")
    gs = _norm_matmul(x, nw1, p["w_gates"], tn=D_MODEL, out_dtype=BF16, act="sigmoid", name="proj_gates")

    zx3 = zx.reshape(nb, l, D_INNER + CONV_DIM)
    qkvd3 = qkvd.reshape(nb, l, QKVD_WIDTH)
    y_ssd, ssm_new = _ssd(zx3, qkvd3, _pad_state_rows(conv_state), ssm0, p["ssd_conv_w"], p["ssd_conv_b"],
                          p["dt_bias"], p["a_log"], p["d_skip"], p["ssd_norm_w"])
    conv_new = zx3[:, l - (SSD_CONV - 1):, D_INNER:].astype(F32)

    y_att, k_new, v_new = attend(qkvd3)
    y_mem = _mem_attn(qm.reshape(nb, l, MEM_WIDTH), mem_k3, mem_v3)

    h = _merge(x, gs, y_ssd.reshape(t, D_INNER), y_att.reshape(t, ATT_WIDTH), y_mem.reshape(t, MEM_WIDTH),
               p["w_ssd_out"], p["w_attn_out"], p["w_mem_out"], p["w_o"])

    if l % 512 == 0:
        y3, last8 = _ffn_fused(h.reshape(nb, l, d), _pad_state_rows(ffn_state), p["norm2_w"], p["w_up"],
                               p["ffn_conv_w"], p["ffn_conv_b"], p["w_down"], p["final_norm_w"])
        ffn_new = last8[:, SUBLANES - (FFN_CONV - 1):]
        return y3, k_new, v_new, ssm_new, conv_new, ffn_new

    u = _norm_matmul(h, p["norm2_w"], p["w_up"], tn=2 * D_FF // 11, name="ffn_up")
    u3 = u.reshape(nb, l, 2 * D_FF)
    ffn_new = u3[:, l - (FFN_CONV - 1):]
    act = _ffn_gate(u3, _pad_state_rows(ffn_state), p["ffn_conv_w"], p["ffn_conv_b"])
    y = _down(act.reshape(t, D_FF), p["w_down"], h, p["final_norm_w"])
    return y.reshape(nb, l, d), k_new, v_new, ssm_new, conv_new, ffn_new


def _attend_prompt(qkvd3):
    b, s, _ = qkvd3.shape
    tables = _rope_tables(jnp.arange(s, dtype=jnp.int32))
    k_rot, q_t, k_aug, v_t, kmean = _prompt_prep(qkvd3, tables)
    y_att = _moba_prompt(q_t, k_aug, v_t, kmean)
    v = qkvd3[:, :, ATT_WIDTH + KV_WIDTH:ATT_WIDTH + 2 * KV_WIDTH]
    return y_att, k_rot.reshape(b, s, KV_HEADS, ATT_HEAD_DIM), v.reshape(b, s, KV_HEADS, ATT_HEAD_DIM)


def _attend_sample(qkvd3, cache_k, cache_v, page_table):
    ns, t, _ = qkvd3.shape
    npages = page_table.shape[1]
    past = npages * PAGE_SIZE
    pos = past + jnp.arange(t, dtype=jnp.int32)
    tables = tuple(jnp.tile(tb, (ns, 1)) for tb in _rope_tables(pos))
    qk = _rope_sample(qkvd3.reshape(ns * t, QKVD_WIDTH), tables)
    q_rot = qk[:, :ATT_WIDTH].reshape(ns, t, KV_HEADS, GQA_REP, ATT_HEAD_DIM)
    k_rot = qk[:, ATT_WIDTH:].reshape(ns, t, KV_WIDTH)
    v = qkvd3[:, :, ATT_WIDTH + KV_WIDTH:ATT_WIDTH + 2 * KV_WIDTH]
    q_ht = q_rot.transpose(0, 2, 3, 1, 4)
    eye = jnp.eye(KV_HEADS, dtype=F32)
    q_exp = (q_ht[:, :, :, :, None, :] * eye[None, :, None, None, :, None]).reshape(ns, ATT_HEADS * t, KV_WIDTH)
    blk_of_key = jnp.arange(past, dtype=jnp.int32) // MOBA_BLOCK
    e_mat = (jnp.arange(LANES, dtype=jnp.int32)[:, None] == blk_of_key[None, :]).astype(BF16)
    n_phys = cache_k.shape[0]
    to_pages = lambda c: jnp.transpose(c, (0, 2, 3, 1)).reshape(n_phys, KV_WIDTH, PAGE_SIZE)
    y_att = _moba_sample(page_table, q_exp.astype(BF16), k_rot, v, e_mat, to_pages(cache_k), to_pages(cache_v))
    return y_att, k_rot.reshape(ns, t, KV_HEADS, ATT_HEAD_DIM), v.reshape(ns, t, KV_HEADS, ATT_HEAD_DIM)


def _layer_params(l, norm1_w, w_in, ssd_conv_w, ssd_conv_b, dt_bias, a_log, d_skip, ssd_norm_w, w_ssd_out,
                  w_attn_out, w_mem_out, w_o, norm2_w, w_up, ffn_conv_w, ffn_conv_b, w_down, final_norm_w):
    w = w_in[l]
    o_z, o_x, o_dt = 0, D_INNER, D_INNER + CONV_DIM
    o_q = o_dt + SSD_HEADS
    o_k, o_v = o_q + ATT_WIDTH, o_q + ATT_WIDTH + KV_WIDTH
    o_qm = o_v + KV_WIDTH
    o_g = o_qm + MEM_WIDTH
    pad_lanes = lambda a: jnp.pad(a, (0, LANES - a.shape[0])).reshape(1, LANES)
    w_dt = jnp.pad(w[:, o_dt:o_q], ((0, 0), (0, LANES - SSD_HEADS)))
    return {
        "norm1_w": norm1_w[l],
        "w_zx": w[:, o_z:o_dt].astype(BF16),
        "w_qkvd": jnp.concatenate([w[:, o_q:o_qm], w_dt], axis=1).astype(BF16),
        "w_qm": w[:, o_qm:o_g].astype(BF16),
        "w_gates": w[:, o_g:].astype(BF16),
        "ssd_conv_w": ssd_conv_w[l],
        "ssd_conv_b": ssd_conv_b[l].reshape(1, CONV_DIM),
        "dt_bias": pad_lanes(dt_bias[l]),
        "a_log": pad_lanes(a_log[l]),
        "d_skip": jnp.repeat(d_skip[l], SSD_HEAD_DIM).reshape(1, D_INNER),
        "ssd_norm_w": ssd_norm_w[l].reshape(1, D_INNER),
        "w_ssd_out": w_ssd_out[l].astype(BF16),
        "w_attn_out": w_attn_out[l].astype(BF16),
        "w_mem_out": w_mem_out[l].astype(BF16),
        "w_o": w_o[l].astype(BF16),
        "norm2_w": norm2_w[l],
        "w_up": w_up[l].astype(BF16),
        "ffn_conv_w": ffn_conv_w[l],
        "ffn_conv_b": ffn_conv_b[l].reshape(1, 2 * D_FF),
        "w_down": w_down[l].astype(BF16),
        "final_norm_w": final_norm_w,
    }


def kernel(x_prompt, x_sample, cache_k, cache_v, cache_mem_k, cache_mem_v, state_ssm, state_conv,
           state_ffn_conv, page_table, mem_prompt, norm1_w, w_in, ssd_conv_w, ssd_conv_b, dt_bias, a_log,
           d_skip, ssd_norm_w, mem_norm_w, w_mem_kv, w_ssd_out, w_attn_out, w_mem_out, w_o, norm2_w, w_up,
           ffn_conv_w, ffn_conv_b, w_down, final_norm_w):
    depth = w_in.shape[0]
    assert depth == 1, "the final RMSNorm is fused into the single layer's last kernel"
    b_p, s_p, _ = x_prompt.shape
    n_mem = mem_prompt.shape[1]
    ns = x_sample.shape[0]
    l = 0
    p = _layer_params(l, norm1_w, w_in, ssd_conv_w, ssd_conv_b, dt_bias, a_log, d_skip, ssd_norm_w, w_ssd_out,
                      w_attn_out, w_mem_out, w_o, norm2_w, w_up, ffn_conv_w, ffn_conv_b, w_down, final_norm_w)

    memx = mem_prompt.reshape(b_p * n_mem, D_MODEL)
    wkv = w_mem_kv[l].astype(BF16)
    mk_p = _norm_matmul(memx, mem_norm_w[l], wkv[:, :MEM_WIDTH], tn=MEM_WIDTH, name="mem_k")
    mv_p = _norm_matmul(memx, mem_norm_w[l], wkv[:, MEM_WIDTH:], tn=MEM_WIDTH, name="mem_v")
    mk_p3 = mk_p.reshape(b_p, n_mem, MEM_WIDTH)
    mv_p3 = mv_p.reshape(b_p, n_mem, MEM_WIDTH)
    ssm0 = jnp.zeros((b_p, SSD_HEADS, SSD_HEAD_DIM, SSD_STATE), F32)
    conv0 = jnp.zeros((b_p, SSD_CONV - 1, CONV_DIM), F32)
    ffn0 = jnp.zeros((b_p, FFN_CONV - 1, 2 * D_FF), F32)
    y_p, k_p, v_p, s_p_new, c_p, f_p = _layer(x_prompt, mk_p3, mv_p3, ssm0, conv0, ffn0, _attend_prompt, p)

    attend_s = functools.partial(_attend_sample, cache_k=cache_k[l], cache_v=cache_v[l], page_table=page_table)
    y_s, k_s, v_s, s_s_new, c_s, f_s = _layer(x_sample, cache_mem_k[l], cache_mem_v[l], state_ssm[l], state_conv[l],
                                              state_ffn_conv[l], attend_s, p)

    mem_shape = (1, b_p, n_mem, MEM_HEADS, MEM_HEAD_DIM)
    return (y_p, y_s, k_p[None], v_p[None], mk_p.reshape(mem_shape), mv_p.reshape(mem_shape),
            s_p_new[None], c_p[None], f_p[None], k_s[None], v_s[None], s_s_new[None], c_s[None], f_s[None])
```

```python
import functools
import math

import jax
import jax.numpy as jnp
from jax import lax
from jax.experimental import pallas as pl
from jax.experimental.pallas import tpu as pltpu

F32 = jnp.float32
BF16 = jnp.bfloat16

D_MODEL = 1024
D_INNER = 2048
SSD_HEAD_DIM = 64
SSD_HEADS = 32
SSD_GROUPS = 8
SSD_STATE = 128
SSD_CONV = 4
SSD_CHUNK = 128
CONV_DIM = 4096
ATT_HEADS = 16
ATT_HEAD_DIM = 64
KV_HEADS = 4
GQA_REP = 4
ATT_WIDTH = 1024
KV_WIDTH = 256
MOBA_BLOCK = 256
MOBA_TOPK = 3
ROT_DIM = 16
ROPE_THETA = 500000.0
MEM_HEADS = 4
MEM_HEAD_DIM = 256
MEM_WIDTH = 1024
D_FF = 2816
FFN_CONV = 3
RMS_EPS = 1e-6
NEG_INF = -1e30
LOG2_E = 1.4426950408889634
PAGE_SIZE = 128

LANES = 128
SUBLANES = 8
VMEM_LIMIT = 56 * 1024 * 1024
V_ONES = 16

QKVD_WIDTH = ATT_WIDTH + 2 * KV_WIDTH + LANES
DT_COL_BLOCK = (ATT_WIDTH + 2 * KV_WIDTH) // LANES


def _cparams(sem):
    return pltpu.CompilerParams(dimension_semantics=sem, vmem_limit_bytes=VMEM_LIMIT)


def _dot(a, b):
    return jnp.dot(a, b, preferred_element_type=F32)


def _dot_nt(a, b):
    return lax.dot_general(a, b, (((1,), (1,)), ((), ())), preferred_element_type=F32)


def _dot_f32(a, b):
    return jnp.dot(a, b, preferred_element_type=F32, precision=lax.Precision.HIGHEST)


def _silu(x):
    return x * jax.nn.sigmoid(x)


def _norm_matmul_kernel(x_ref, nw_ref, w_ref, o_ref, xn_ref, *, act):
    @pl.when(pl.program_id(1) == 0)
    def _():
        x = x_ref[...]
        ms = jnp.mean(x * x, axis=-1, keepdims=True)
        xn_ref[...] = (x * lax.rsqrt(ms + RMS_EPS) * nw_ref[...]).astype(BF16)

    y = _dot(xn_ref[...], w_ref[...])
    if act == "sigmoid":
        y = jax.nn.sigmoid(y)
    o_ref[...] = y.astype(o_ref.dtype)


def _norm_matmul(x, nw, w, *, tn, out_dtype=F32, act=None, name="norm_matmul"):
    t, d = x.shape
    n = w.shape[1]
    tm = min(t, 1024)
    assert t % tm == 0 and n % tn == 0
    return pl.pallas_call(
        functools.partial(_norm_matmul_kernel, act=act),
        grid=(t // tm, n // tn),
        in_specs=[pl.BlockSpec((tm, d), lambda i, j: (i, 0)),
                  pl.BlockSpec((1, d), lambda i, j: (0, 0)),
                  pl.BlockSpec((d, tn), lambda i, j: (0, j))],
        out_specs=pl.BlockSpec((tm, tn), lambda i, j: (i, j)),
        out_shape=jax.ShapeDtypeStruct((t, n), out_dtype),
        scratch_shapes=[pltpu.VMEM((tm, d), BF16)],
        compiler_params=_cparams(("parallel", "arbitrary")),
        name=name,
    )(x, nw.reshape(1, d), w)


def _shift_rows(x, k, halo):
    r = pltpu.roll(x, k, axis=0)
    row = lax.broadcasted_iota(jnp.int32, halo.shape, 0)
    top = jnp.where(row < k, pltpu.roll(halo, k, axis=0), r[:SUBLANES])
    if x.shape[0] == SUBLANES:
        return top
    return jnp.concatenate([top, r[SUBLANES:]], axis=0)


def _ssd_kernel(z_ref, xa_ref, xb_ref, dt_ref, cst_ref, s0_ref, cw_ref, cb_ref, dtb_ref, alog_ref,
                dsk_ref, nw_ref, y_ref, sf_ref, s_ref, carry_ref, ybuf_ref, *, qb, q, nchunks):
    c = pl.program_id(1)

    @pl.when(c == 0)
    def _():
        s_ref[...] = s0_ref[0]
        carry_ref[...] = cst_ref[0]

    pre = jnp.concatenate([xa_ref[0], xb_ref[0]], axis=1).astype(F32)
    halo = carry_ref[...]
    w0, w1, w2, w3 = (cw_ref[k:k + 1, :] for k in range(SSD_CONV))
    prev = _shift_rows(pre, 1, halo)
    halo_pair = halo * w1 + pltpu.roll(halo, 1, axis=0) * w0
    acc = _shift_rows(pre * w1 + prev * w0, 2, halo_pair) + (prev * w2 + pre * w3)
    xbc = _silu(acc + cb_ref[...])
    if nchunks > 1:
        carry_ref[...] = pre[qb - SUBLANES:qb, :]

    z = z_ref[0].astype(F32)
    dt_raw = dt_ref[0]
    if qb < q:
        xbc = jnp.concatenate([xbc, jnp.zeros((q - qb, CONV_DIM), F32)], axis=0)
        dt_raw = jnp.concatenate([dt_raw, jnp.zeros((q - qb, LANES), F32)], axis=0)

    xs = xbc[:, :D_INNER]
    xs_bf = xs.astype(BF16)
    xs_t = xs.T

    v = dt_raw + dtb_ref[...]
    dt = jnp.maximum(v, 0.0) + jnp.log1p(jnp.exp(-jnp.abs(v)))
    row_q = lax.broadcasted_iota(jnp.int32, (q, LANES), 0)
    if qb < q:
        dt = jnp.where(row_q < qb, dt, 0.0)
    a = -jnp.exp(alog_ref[...])
    ri = lax.broadcasted_iota(jnp.int32, (q, q), 0)
    ci = lax.broadcasted_iota(jnp.int32, (q, q), 1)
    causal = ci <= ri
    cs = _dot_f32(causal.astype(F32), dt * a)
    cs_t = cs.T
    dt_t = dt.T
    ecs = jnp.exp(cs)

    for g in range(SSD_GROUPS):
        b_g = xbc[:, D_INNER + g * SSD_STATE:D_INNER + (g + 1) * SSD_STATE].astype(BF16)
        c_g = xbc[:qb, D_INNER + SSD_GROUPS * SSD_STATE + g * SSD_STATE:
                  D_INNER + SSD_GROUPS * SSD_STATE + (g + 1) * SSD_STATE].astype(BF16)
        cb = _dot_nt(c_g, b_g)
        for r in range(SSD_HEADS // SSD_GROUPS):
            h = g * (SSD_HEADS // SSD_GROUPS) + r
            lo = h * SSD_HEAD_DIM
            cs_col = cs[:qb, h:h + 1]
            cs_row = cs_t[h:h + 1, :]
            dt_row = dt_t[h:h + 1, :]
            seg = jnp.where(causal[:qb], cs_col - cs_row, -jnp.inf)
            m_h = (cb * jnp.exp(seg) * dt_row).astype(BF16)
            s_h = s_ref[h]
            y_h = _dot(m_h, xs_bf[:, lo:lo + SSD_HEAD_DIM])
            y_h = y_h + _dot_nt(c_g, s_h.astype(BF16)) * ecs[:qb, h:h + 1]
            ybuf_ref[:, lo:lo + SSD_HEAD_DIM] = y_h
            cs_end = cs_row[:, q - 1:q]
            w_row = dt_row * jnp.exp(cs_end - cs_row)
            xw = (xs_t[lo:lo + SSD_HEAD_DIM, :] * w_row).astype(BF16)
            s_ref[h] = s_h * jnp.exp(cs_end) + _dot(xw, b_g)

    y = ybuf_ref[...] + dsk_ref[...] * xs[:qb]
    gt = y * _silu(z)
    gw = D_INNER // SSD_GROUPS
    parts = []
    for g in range(SSD_GROUPS):
        gg = gt[:, g * gw:(g + 1) * gw]
        parts.append(gg * lax.rsqrt(jnp.mean(gg * gg, axis=-1, keepdims=True) + RMS_EPS))
    yn = jnp.concatenate(parts, axis=1) * nw_ref[...]
    y_ref[0] = yn.astype(y_ref.dtype)

    @pl.when(c == nchunks - 1)
    def _():
        sf_ref[0] = s_ref[...]


def _ssd(zx3, qkvd3, conv_state8, ssm0, cw, cb, dtb, alog, dsk, nw):
    nb, l, _ = zx3.shape
    q = SSD_CHUNK
    qb = min(l, q)
    assert l % qb == 0 and qb % SUBLANES == 0
    nchunks = l // qb
    half = CONV_DIM // 2
    row_blk = lambda col: (lambda b, c: (b, c, col))
    full2 = lambda b, c: (0, 0)
    return pl.pallas_call(
        functools.partial(_ssd_kernel, qb=qb, q=q, nchunks=nchunks),
        grid=(nb, nchunks),
        in_specs=[pl.BlockSpec((1, qb, D_INNER), row_blk(0)),
                  pl.BlockSpec((1, qb, half), row_blk(1)),
                  pl.BlockSpec((1, qb, half), row_blk(2)),
                  pl.BlockSpec((1, qb, LANES), row_blk(DT_COL_BLOCK)),
                  pl.BlockSpec((1, SUBLANES, CONV_DIM), lambda b, c: (b, 0, 0)),
                  pl.BlockSpec((1, SSD_HEADS, SSD_HEAD_DIM, SSD_STATE), lambda b, c: (b, 0, 0, 0)),
                  pl.BlockSpec((SSD_CONV, CONV_DIM), full2),
                  pl.BlockSpec((1, CONV_DIM), full2),
                  pl.BlockSpec((1, LANES), full2),
                  pl.BlockSpec((1, LANES), full2),
                  pl.BlockSpec((1, D_INNER), full2),
                  pl.BlockSpec((1, D_INNER), full2)],
        out_specs=[pl.BlockSpec((1, qb, D_INNER), lambda b, c: (b, c, 0)),
                   pl.BlockSpec((1, SSD_HEADS, SSD_HEAD_DIM, SSD_STATE), lambda b, c: (b, 0, 0, 0))],
        out_shape=[jax.ShapeDtypeStruct((nb, l, D_INNER), BF16),
                   jax.ShapeDtypeStruct((nb, SSD_HEADS, SSD_HEAD_DIM, SSD_STATE), F32)],
        scratch_shapes=[pltpu.VMEM((SSD_HEADS, SSD_HEAD_DIM, SSD_STATE), F32),
                        pltpu.VMEM((SUBLANES, CONV_DIM), F32),
                        pltpu.VMEM((qb, D_INNER), F32)],
        compiler_params=_cparams(("parallel", "arbitrary")),
        name="ssd",
    )(zx3, zx3, zx3, qkvd3, conv_state8, ssm0, cw, cb, dtb, alog, dsk, nw)


def _rope_tables(pos):
    half = ROT_DIM // 2
    inv = ROPE_THETA ** (-jnp.arange(half, dtype=F32) * 2.0 / ROT_DIM)
    ang = pos.astype(F32)[:, None] * inv[None, :]
    cos, sin = jnp.cos(ang), jnp.sin(ang)
    n = pos.shape[0]
    pad = jnp.zeros((n, ATT_HEAD_DIM - ROT_DIM), F32)
    zero = jnp.zeros((n, half), F32)
    c_head = jnp.concatenate([cos, cos, pad + 1.0], axis=1)
    s1_head = jnp.concatenate([-sin, zero, pad], axis=1)
    s2_head = jnp.concatenate([zero, sin, pad], axis=1)
    rep = LANES // ATT_HEAD_DIM
    return jnp.tile(c_head, (1, rep)), jnp.tile(s1_head, (1, rep)), jnp.tile(s2_head, (1, rep))


def _rope_group(xg, c, s1, s2):
    half = ROT_DIM // 2
    return xg * c + pltpu.roll(xg, LANES - half, axis=1) * s1 + pltpu.roll(xg, half, axis=1) * s2


def _head_from_group(xg, odd, lane):
    if odd:
        xg = pltpu.roll(xg, ATT_HEAD_DIM, axis=1)
    return jnp.where(lane < ATT_HEAD_DIM, xg, 0.0)


def _prompt_prep_kernel(q_ref, k_ref, v_ref, c_ref, s1_ref, s2_ref,
                        krot_ref, vout_ref, qt_ref, kaug_ref, vt_ref, kmean_ref):
    i = pl.program_id(1)
    rows = q_ref.shape[1]
    c, s1, s2 = c_ref[...], s1_ref[...], s2_ref[...]
    lane = lax.broadcasted_iota(jnp.int32, (rows, LANES), 1)
    scale = ATT_HEAD_DIM ** -0.5 * LOG2_E
    hd = ATT_HEAD_DIM

    @pl.when(i == 0)
    def _():
        kmean_ref[...] = jnp.zeros_like(kmean_ref)

    for cg in range(ATT_WIDTH // LANES):
        qg = _rope_group(q_ref[0, :, cg * LANES:(cg + 1) * LANES], c, s1, s2) * scale
        qg_t = qg.T.astype(BF16)
        for odd in range(2):
            g, r = divmod(2 * cg + odd, GQA_REP)
            qt_ref[0, g, 0, :, r * rows:(r + 1) * rows] = qg_t[odd * hd:(odd + 1) * hd, :]

    onehot = jnp.where(lane == ATT_HEAD_DIM + i, 1.0, 0.0)
    for cg in range(KV_WIDTH // LANES):
        kg = _rope_group(k_ref[0, :, cg * LANES:(cg + 1) * LANES], c, s1, s2)
        krot_ref[0, :, cg * LANES:(cg + 1) * LANES] = kg
        vg = v_ref[0, :, cg * LANES:(cg + 1) * LANES]
        vout_ref[0, :, cg * LANES:(cg + 1) * LANES] = vg
        vg_t = vg.T.astype(BF16)
        for odd in range(2):
            g = 2 * cg + odd
            kh = _head_from_group(kg, odd, lane)
            kaug_ref[0, g] = (kh + onehot).astype(BF16)
            vt_ref[0, g, 0:hd, :] = vg_t[odd * hd:(odd + 1) * hd, :]
            vt_ref[0, g, hd:, :] = jnp.ones((V_ONES, rows), BF16)
            kmean_ref[0, g, pl.ds(i, 1), :] = jnp.mean(kh, axis=0, keepdims=True)


def _prompt_prep(qkvd3, tables):
    b, s, _ = qkvd3.shape
    nblk = s // MOBA_BLOCK
    assert s % MOBA_BLOCK == 0 and nblk <= SUBLANES
    blk = MOBA_BLOCK
    tab = pl.BlockSpec((blk, LANES), lambda bi, i: (i, 0))
    return pl.pallas_call(
        _prompt_prep_kernel,
        grid=(b, nblk),
        in_specs=[pl.BlockSpec((1, blk, ATT_WIDTH), lambda bi, i: (bi, i, 0)),
                  pl.BlockSpec((1, blk, KV_WIDTH), lambda bi, i: (bi, i, ATT_WIDTH // KV_WIDTH)),
                  pl.BlockSpec((1, blk, KV_WIDTH), lambda bi, i: (bi, i, ATT_WIDTH // KV_WIDTH + 1)),
                  tab, tab, tab],
        out_specs=[pl.BlockSpec((1, blk, KV_WIDTH), lambda bi, i: (bi, i, 0)),
                   pl.BlockSpec((1, blk, KV_WIDTH), lambda bi, i: (bi, i, 0)),
                   pl.BlockSpec((1, KV_HEADS, 1, ATT_HEAD_DIM, GQA_REP * blk), lambda bi, i: (bi, 0, i, 0, 0)),
                   pl.BlockSpec((1, KV_HEADS, blk, LANES), lambda bi, i: (bi, 0, i, 0)),
                   pl.BlockSpec((1, KV_HEADS, ATT_HEAD_DIM + V_ONES, blk), lambda bi, i: (bi, 0, 0, i)),
                   pl.BlockSpec((1, KV_HEADS, SUBLANES, LANES), lambda bi, i: (bi, 0, 0, 0))],
        out_shape=[jax.ShapeDtypeStruct((b, s, KV_WIDTH), F32),
                   jax.ShapeDtypeStruct((b, s, KV_WIDTH), F32),
                   jax.ShapeDtypeStruct((b, KV_HEADS, nblk, ATT_HEAD_DIM, GQA_REP * blk), BF16),
                   jax.ShapeDtypeStruct((b, KV_HEADS, s, LANES), BF16),
                   jax.ShapeDtypeStruct((b, KV_HEADS, ATT_HEAD_DIM + V_ONES, s), BF16),
                   jax.ShapeDtypeStruct((b, KV_HEADS, SUBLANES, LANES), F32)],
        compiler_params=_cparams(("parallel", "arbitrary")),
        name="prompt_prep",
    )(qkvd3, qkvd3, qkvd3, *tables)


def _topk_rows(gate, valid, row_f, rounds=MOBA_TOPK):
    g0 = jnp.where(valid, gate, -jnp.inf)
    sel = jnp.zeros(gate.shape, dtype=jnp.bool_)
    for _ in range(rounds):
        m = jnp.max(g0, axis=0, keepdims=True)
        idx = jnp.min(jnp.where(g0 == m, row_f, 1e9), axis=0, keepdims=True)
        pick = jnp.logical_and(row_f == idx, m > -jnp.inf)
        sel = jnp.logical_or(sel, pick)
        g0 = jnp.where(pick, -jnp.inf, g0)
    return sel


def _topk_lanes(gate, valid, lane_f, rounds=MOBA_TOPK):
    g0 = jnp.where(valid, gate, -jnp.inf)
    sel = jnp.zeros(gate.shape, dtype=jnp.bool_)
    for _ in range(rounds):
        m = jnp.max(g0, axis=1, keepdims=True)
        idx = jnp.min(jnp.where(g0 == m, lane_f, 1e9), axis=1, keepdims=True)
        pick = jnp.logical_and(lane_f == idx, m > -jnp.inf)
        sel = jnp.logical_or(sel, pick)
        g0 = jnp.where(pick, -jnp.inf, g0)
    return sel


def _moba_prompt_block(n_past, q_ref, k_ref, v_ref, km_ref, y_ref):
    blk = MOBA_BLOCK
    hd = ATT_HEAD_DIM
    nq = GQA_REP * blk
    qt = q_ref[0, 0, 0]
    q0 = jnp.concatenate([qt, jnp.zeros((LANES - hd, nq), BF16)], axis=0)
    k_own = k_ref[0, 0, n_past * blk:(n_past + 1) * blk, :]
    key_i = lax.broadcasted_iota(jnp.int32, (blk, nq), 0)
    q_i = lax.broadcasted_iota(jnp.int32, (blk, nq), 1) % blk
    s_own = jnp.where(key_i <= q_i, _dot(k_own, q0), NEG_INF)
    m = jnp.max(s_own, axis=0, keepdims=True)
    if n_past > 0:
        pad8 = jnp.zeros((SUBLANES, LANES), F32)
        gate = _dot(jnp.concatenate([km_ref[0, 0], pad8], axis=0).astype(BF16), q0)
        row = lax.broadcasted_iota(jnp.int32, gate.shape, 0)
        sel = _topk_rows(gate, row < n_past, row.astype(F32), rounds=min(MOBA_TOPK, n_past))
        selb = jnp.where(row < SUBLANES, jnp.where(sel, 0.0, NEG_INF), 0.0).astype(BF16)
        q_aug = jnp.concatenate([qt, selb, jnp.zeros((LANES - hd - 2 * SUBLANES, nq), BF16)], axis=0)
        s_past = _dot(k_ref[0, 0, 0:n_past * blk, :], q_aug)
        m = jnp.maximum(m, jnp.max(s_past, axis=0, keepdims=True))
    p_own = jnp.exp2(s_own - m)
    o = _dot(v_ref[0, 0, :, n_past * blk:(n_past + 1) * blk], p_own.astype(BF16))
    if n_past > 0:
        p_past = jnp.exp2(s_past - m)
        o = o + _dot(v_ref[0, 0, :, 0:n_past * blk], p_past.astype(BF16))
    o = o[0:hd] / o[hd:hd + 1]
    for pair in range(GQA_REP // 2):
        two = jnp.concatenate([o[:, 2 * pair * blk:(2 * pair + 1) * blk],
                               o[:, (2 * pair + 1) * blk:(2 * pair + 2) * blk]], axis=0)
        y_ref[:, pair * LANES:(pair + 1) * LANES] = two.T.astype(y_ref.dtype)


def _moba_prompt_kernel(q_ref, k_ref, v_ref, km_ref, y_ref, *, nblk):
    i = pl.program_id(2)
    for n_past in range(nblk):
        pl.when(i == n_past)(functools.partial(_moba_prompt_block, n_past, q_ref, k_ref, v_ref, km_ref, y_ref))


def _moba_prompt(q_t, k_aug, v_t, kmean):
    b, _, s, _ = k_aug.shape
    nblk = s // MOBA_BLOCK
    blk = MOBA_BLOCK
    return pl.pallas_call(
        functools.partial(_moba_prompt_kernel, nblk=nblk),
        grid=(b, KV_HEADS, nblk),
        in_specs=[pl.BlockSpec((1, 1, 1, ATT_HEAD_DIM, GQA_REP * blk), lambda bi, g, i: (bi, g, i, 0, 0)),
                  pl.BlockSpec((1, 1, s, LANES), lambda bi, g, i: (bi, g, 0, 0)),
                  pl.BlockSpec((1, 1, ATT_HEAD_DIM + V_ONES, s), lambda bi, g, i: (bi, g, 0, 0)),
                  pl.BlockSpec((1, 1, SUBLANES, LANES), lambda bi, g, i: (bi, g, 0, 0))],
        out_specs=pl.BlockSpec((blk, GQA_REP * ATT_HEAD_DIM), lambda bi, g, i: (bi * nblk + i, g)),
        out_shape=jax.ShapeDtypeStruct((b * s, ATT_WIDTH), BF16),
        compiler_params=_cparams(("parallel", "parallel", "arbitrary")),
        name="moba_prompt",
    )(q_t, k_aug, v_t, kmean)


def _rope_kernel(x_ref, c_ref, s1_ref, s2_ref, o_ref, *, q_groups):
    c, s1, s2 = c_ref[...], s1_ref[...], s2_ref[...]
    scale = ATT_HEAD_DIM ** -0.5
    for cg in range(x_ref.shape[1] // LANES):
        xg = _rope_group(x_ref[:, cg * LANES:(cg + 1) * LANES], c, s1, s2)
        if cg < q_groups:
            xg = xg * scale
        o_ref[:, cg * LANES:(cg + 1) * LANES] = xg


def _rope_sample(qkvd, tables):
    t = qkvd.shape[0]
    w = ATT_WIDTH + KV_WIDTH
    tab = pl.BlockSpec((t, LANES), lambda i: (0, 0))
    return pl.pallas_call(
        functools.partial(_rope_kernel, q_groups=ATT_WIDTH // LANES),
        grid=(1,),
        in_specs=[pl.BlockSpec((t, w), lambda i: (0, 0)), tab, tab, tab],
        out_specs=pl.BlockSpec((t, w), lambda i: (0, 0)),
        out_shape=jax.ShapeDtypeStruct((t, w), F32),
        compiler_params=_cparams(("arbitrary",)),
        name="rope_sample",
    )(qkvd, *tables)


def _moba_sample_kernel(pt_ref, qx_ref, kn_ref, vn_ref, e_ref, ck_ref, cv_ref, y_ref,
                        kbuf, vbuf, sem, *, npages, t):
    s = pl.program_id(0)
    ns = pl.num_programs(0)
    slot = s % 2

    def k_copy(seq, p, sl):
        return pltpu.make_async_copy(ck_ref.at[pt_ref[seq, p]], kbuf.at[sl, :, p * PAGE_SIZE:(p + 1) * PAGE_SIZE],
                                     sem.at[0, sl])

    def v_copy(seq, p, sl):
        return pltpu.make_async_copy(cv_ref.at[pt_ref[seq, p]], vbuf.at[sl, :, p * PAGE_SIZE:(p + 1) * PAGE_SIZE],
                                     sem.at[1, sl])

    def start_all(seq, sl):
        for p in range(npages):
            k_copy(seq, p, sl).start()
            v_copy(seq, p, sl).start()

    @pl.when(s == 0)
    def _():
        start_all(0, 0)

    @pl.when(s + 1 < ns)
    def _():
        start_all(s + 1, 1 - slot)

    for p in range(npages):
        k_copy(s, p, slot).wait()
        v_copy(s, p, slot).wait()

    past = npages * PAGE_SIZE
    nblk = past // MOBA_BLOCK
    rows = ATT_HEADS * t
    chunk = min(past, 2048)
    qx = qx_ref[0]

    s_raw = jnp.concatenate(
        [_dot(qx, kbuf[slot, :, c * chunk:(c + 1) * chunk].astype(BF16)) for c in range(past // chunk)], axis=1)
    lane = lax.broadcasted_iota(jnp.int32, (rows, LANES), 1)
    gate = jnp.zeros((rows, LANES), F32)
    for n in range(nblk):
        col = jnp.sum(s_raw[:, n * MOBA_BLOCK:(n + 1) * MOBA_BLOCK], axis=1, keepdims=True)
        gate = jnp.where(lane == n, col, gate)
    sel = _topk_lanes(gate, lane < nblk, lane.astype(F32))
    selb = jnp.where(sel, 0.0, NEG_INF).astype(BF16)
    s_past = s_raw + _dot(selb, e_ref[...])

    kn = jnp.concatenate([kn_ref[0], jnp.zeros((LANES - t, KV_WIDTH), F32)], axis=0).astype(BF16)
    vn = jnp.concatenate([vn_ref[0], jnp.zeros((LANES - t, KV_WIDTH), F32)], axis=0).astype(BF16)
    row = lax.broadcasted_iota(jnp.int32, (rows, LANES), 0)
    own_ok = lane <= (row % t)
    s_own = jnp.where(own_ok, _dot_nt(qx, kn), NEG_INF)

    m = jnp.maximum(jnp.max(s_past, axis=1, keepdims=True), jnp.max(s_own, axis=1, keepdims=True))
    p_past = jnp.exp(s_past - m)
    p_own = jnp.exp(s_own - m)
    den = jnp.sum(p_past, axis=1, keepdims=True) + jnp.sum(p_own, axis=1, keepdims=True)
    o = _dot(p_own.astype(BF16), vn)
    for c in range(past // chunk):
        o = o + _dot_nt(p_past[:, c * chunk:(c + 1) * chunk].astype(BF16),
                        vbuf[slot, :, c * chunk:(c + 1) * chunk].astype(BF16))
    o = o / den
    for h in range(ATT_HEADS):
        g = h // GQA_REP
        y_ref[0, :, h * ATT_HEAD_DIM:(h + 1) * ATT_HEAD_DIM] = (
            o[h * t:(h + 1) * t, g * ATT_HEAD_DIM:(g + 1) * ATT_HEAD_DIM].astype(y_ref.dtype))


def _moba_sample(page_table, q_exp, k_new, v_new, e_mat, cache_k, cache_v):
    ns, npages = page_table.shape
    t = k_new.shape[1]
    past = npages * PAGE_SIZE
    assert past % MOBA_BLOCK == 0 and past // MOBA_BLOCK <= LANES and t == SUBLANES
    rows = ATT_HEADS * t
    grid_spec = pltpu.PrefetchScalarGridSpec(
        num_scalar_prefetch=1,
        grid=(ns,),
        in_specs=[pl.BlockSpec((1, rows, KV_WIDTH), lambda s, pt: (s, 0, 0)),
                  pl.BlockSpec((1, t, KV_WIDTH), lambda s, pt: (s, 0, 0)),
                  pl.BlockSpec((1, t, KV_WIDTH), lambda s, pt: (s, 0, 0)),
                  pl.BlockSpec((LANES, past), lambda s, pt: (0, 0)),
                  pl.BlockSpec(memory_space=pl.ANY),
                  pl.BlockSpec(memory_space=pl.ANY)],
        out_specs=pl.BlockSpec((1, t, ATT_WIDTH), lambda s, pt: (s, 0, 0)),
        scratch_shapes=[pltpu.VMEM((2, KV_WIDTH, past), F32),
                        pltpu.VMEM((2, KV_WIDTH, past), F32),
                        pltpu.SemaphoreType.DMA((2, 2))],
    )
    return pl.pallas_call(
        functools.partial(_moba_sample_kernel, npages=npages, t=t),
        grid_spec=grid_spec,
        out_shape=jax.ShapeDtypeStruct((ns, t, ATT_WIDTH), BF16),
        compiler_params=_cparams(("arbitrary",)),
        name="moba_sample",
    )(page_table, q_exp, k_new, v_new, e_mat, cache_k, cache_v)


def _mem_head(q, mk, mv):
    q = (q.astype(F32) * MEM_HEAD_DIM ** -0.5).astype(BF16)
    s = _dot_nt(q, mk.astype(BF16))
    m = jnp.max(s, axis=1, keepdims=True)
    p = jnp.exp(s - m)
    den = jnp.sum(p, axis=1, keepdims=True)
    return _dot(p.astype(BF16), mv.astype(BF16)) / den


def _mem_attn_kernel(q_ref, mk_ref, mv_ref, y_ref):
    for h in range(MEM_HEADS):
        lo, hi = h * MEM_HEAD_DIM, (h + 1) * MEM_HEAD_DIM
        y_ref[0, :, lo:hi] = _mem_head(q_ref[0, :, lo:hi], mk_ref[0, :, lo:hi], mv_ref[0, :, lo:hi]).astype(y_ref.dtype)


def _mem_attn_paged_kernel(q_ref, mk_hbm, mv_hbm, y_ref, kbuf, vbuf, sem):
    s = pl.program_id(0)
    ns = pl.num_programs(0)
    slot = s % 2

    def copies(seq, sl):
        out = []
        for h in range(MEM_HEADS):
            out.append(pltpu.make_async_copy(mk_hbm.at[seq, :, h, :], kbuf.at[sl, h], sem.at[0, sl]))
            out.append(pltpu.make_async_copy(mv_hbm.at[seq, :, h, :], vbuf.at[sl, h], sem.at[1, sl]))
        return out

    @pl.when(s == 0)
    def _():
        for c in copies(0, 0):
            c.start()

    @pl.when(s + 1 < ns)
    def _():
        for c in copies(s + 1, 1 - slot):
            c.start()

    for c in copies(s, slot):
        c.wait()
    for h in range(MEM_HEADS):
        lo, hi = h * MEM_HEAD_DIM, (h + 1) * MEM_HEAD_DIM
        y_ref[0, :, lo:hi] = _mem_head(q_ref[0, :, lo:hi], kbuf[slot, h], vbuf[slot, h]).astype(y_ref.dtype)


def _mem_attn_paged(qm3, mk4, mv4):
    ns, l, _ = qm3.shape
    m = mk4.shape[1]
    return pl.pallas_call(
        _mem_attn_paged_kernel,
        grid=(ns,),
        in_specs=[pl.BlockSpec((1, l, MEM_WIDTH), lambda s: (s, 0, 0)),
                  pl.BlockSpec(memory_space=pl.ANY), pl.BlockSpec(memory_space=pl.ANY)],
        out_specs=pl.BlockSpec((1, l, MEM_WIDTH), lambda s: (s, 0, 0)),
        out_shape=jax.ShapeDtypeStruct((ns, l, MEM_WIDTH), BF16),
        scratch_shapes=[pltpu.VMEM((2, MEM_HEADS, m, MEM_HEAD_DIM), F32),
                        pltpu.VMEM((2, MEM_HEADS, m, MEM_HEAD_DIM), F32),
                        pltpu.SemaphoreType.DMA((2, 2))],
        compiler_params=_cparams(("arbitrary",)),
        name="mem_attn_paged",
    )(qm3, mk4, mv4)


def _mem_attn(qm3, mk, mv):
    nb, l, _ = qm3.shape
    if mk.ndim == 4:
        return _mem_attn_paged(qm3, mk, mv)
    m = mk.shape[1]
    tl = min(l, 512)
    assert l % tl == 0
    mem_spec = pl.BlockSpec((1, m, MEM_WIDTH), lambda b, i: (b, 0, 0))
    return pl.pallas_call(
        _mem_attn_kernel,
        grid=(nb, l // tl),
        in_specs=[pl.BlockSpec((1, tl, MEM_WIDTH), lambda b, i: (b, i, 0)), mem_spec, mem_spec],
        out_specs=pl.BlockSpec((1, tl, MEM_WIDTH), lambda b, i: (b, i, 0)),
        out_shape=jax.ShapeDtypeStruct((nb, l, MEM_WIDTH), BF16),
        compiler_params=_cparams(("parallel", "arbitrary")),
        name="mem_attn",
    )(qm3, mk, mv)


def _merge_kernel(x_ref, nw_ref, wg_ref, ys_ref, ya_ref, ym_ref, ws_ref, wa_ref, wm_ref, wo_ref, h_ref):
    d = D_MODEL
    x = x_ref[...]
    ms = jnp.mean(x * x, axis=-1, keepdims=True)
    xn = (x * lax.rsqrt(ms + RMS_EPS) * nw_ref[...]).astype(BF16)

    def gate(k):
        return jax.nn.sigmoid(_dot(xn, wg_ref[:, k * d:(k + 1) * d]))

    merged = gate(0) * _dot(ys_ref[...], ws_ref[...])
    merged = merged + gate(1) * _dot(ya_ref[...], wa_ref[...])
    merged = merged + gate(2) * _dot(ym_ref[...], wm_ref[...])
    h_ref[...] = x + _dot(merged.astype(BF16), wo_ref[...])


def _merge(x, nw, wg, y_ssd, y_att, y_mem, ws, wa, wm, wo):
    t = x.shape[0]
    tm = min(t, 512)
    assert t % tm == 0
    nw = nw.reshape(1, D_MODEL)
    rows = lambda w: pl.BlockSpec((tm, w), lambda i: (i, 0))
    const = lambda a: pl.BlockSpec(a.shape, lambda i: (0, 0), pipeline_mode=pl.Buffered(1))
    return pl.pallas_call(
        _merge_kernel,
        grid=(t // tm,),
        in_specs=[rows(D_MODEL), const(nw), const(wg), rows(D_INNER), rows(ATT_WIDTH), rows(MEM_WIDTH),
                  const(ws), const(wa), const(wm), const(wo)],
        out_specs=rows(D_MODEL),
        out_shape=jax.ShapeDtypeStruct((t, D_MODEL), F32),
        compiler_params=_cparams(("parallel",)),
        name="merge",
    )(x, nw, wg, y_ssd, y_att, y_mem, ws, wa, wm, wo)


def _ffn_gate_kernel(u_ref, st_ref, w_ref, b_ref, a_ref, carry_ref, *, tl, ntiles):
    i = pl.program_id(1)

    @pl.when(i == 0)
    def _():
        carry_ref[...] = st_ref[0]

    u = u_ref[0]
    halo = carry_ref[...]
    acc = _shift_rows(u, 2, halo) * w_ref[0:1, :]
    acc = acc + _shift_rows(u, 1, halo) * w_ref[1:2, :]
    acc = acc + u * w_ref[2:3, :]
    uc = acc + b_ref[...]
    if ntiles > 1:
        carry_ref[...] = u[tl - SUBLANES:tl, :]
    a_ref[0] = (_silu(uc[:, :D_FF]) * uc[:, D_FF:]).astype(a_ref.dtype)


def _ffn_gate(u3, state8, w, b):
    nb, l, c = u3.shape
    tl = min(l, 256)
    assert l % tl == 0 and tl % SUBLANES == 0
    ntiles = l // tl
    return pl.pallas_call(
        functools.partial(_ffn_gate_kernel, tl=tl, ntiles=ntiles),
        grid=(nb, ntiles),
        in_specs=[pl.BlockSpec((1, tl, c), lambda bi, i: (bi, i, 0)),
                  pl.BlockSpec((1, SUBLANES, c), lambda bi, i: (bi, 0, 0)),
                  pl.BlockSpec((FFN_CONV, c), lambda bi, i: (0, 0)),
                  pl.BlockSpec((1, c), lambda bi, i: (0, 0))],
        out_specs=pl.BlockSpec((1, tl, D_FF), lambda bi, i: (bi, i, 0)),
        out_shape=jax.ShapeDtypeStruct((nb, l, D_FF), BF16),
        scratch_shapes=[pltpu.VMEM((SUBLANES, c), F32)],
        compiler_params=_cparams(("parallel", "arbitrary")),
        name="ffn_gate",
    )(u3, state8, w, b)


def _ffn_fused_kernel(h_ref, st_ref, n2_ref, wu_ref, cw_ref, cb_ref, wd_ref, fw_ref, y_ref, last_ref, carry_ref,
                      *, tl, ntiles, ck):
    i = pl.program_id(1)

    @pl.when(i == 0)
    def _():
        carry_ref[...] = st_ref[0]

    h = h_ref[0]
    ms = jnp.mean(h * h, axis=-1, keepdims=True)
    hn = (h * lax.rsqrt(ms + RMS_EPS) * n2_ref[...]).astype(BF16)

    def conv(u, off):
        halo = carry_ref[:, off:off + ck]
        acc = _shift_rows(u, 2, halo) * cw_ref[0:1, off:off + ck]
        acc = acc + _shift_rows(u, 1, halo) * cw_ref[1:2, off:off + ck]
        acc = acc + u * cw_ref[2:3, off:off + ck]
        carry_ref[:, off:off + ck] = u[tl - SUBLANES:tl, :]
        return acc + cb_ref[:, off:off + ck]

    out = h
    for c in range(D_FF // ck):
        lo = c * ck
        ug = conv(_dot(hn, wu_ref[:, lo:lo + ck]), lo)
        uv = conv(_dot(hn, wu_ref[:, D_FF + lo:D_FF + lo + ck]), D_FF + lo)
        out = out + _dot((_silu(ug) * uv).astype(BF16), wd_ref[lo:lo + ck, :])
    ms = jnp.mean(out * out, axis=-1, keepdims=True)
    y_ref[0] = out * lax.rsqrt(ms + RMS_EPS) * fw_ref[...]

    @pl.when(i == ntiles - 1)
    def _():
        last_ref[0] = carry_ref[...]


def _ffn_fused(h3, state8, n2, wu, cw, cb, wd, fw):
    nb, l, d = h3.shape
    tl = 512
    ck = D_FF // 2
    assert l % tl == 0 and ck % LANES == 0
    ntiles = l // tl
    const = lambda a: pl.BlockSpec(a.shape, lambda bi, i: (0,) * a.ndim, pipeline_mode=pl.Buffered(1))
    n2, fw = n2.reshape(1, d), fw.reshape(1, d)
    return pl.pallas_call(
        functools.partial(_ffn_fused_kernel, tl=tl, ntiles=ntiles, ck=ck),
        grid=(nb, ntiles),
        in_specs=[pl.BlockSpec((1, tl, d), lambda bi, i: (bi, i, 0)),
                  pl.BlockSpec((1, SUBLANES, 2 * D_FF), lambda bi, i: (bi, 0, 0)),
                  const(n2), const(wu), const(cw), const(cb), const(wd), const(fw)],
        out_specs=[pl.BlockSpec((1, tl, d), lambda bi, i: (bi, i, 0)),
                   pl.BlockSpec((1, SUBLANES, 2 * D_FF), lambda bi, i: (bi, 0, 0))],
        out_shape=[jax.ShapeDtypeStruct((nb, l, d), F32),
                   jax.ShapeDtypeStruct((nb, SUBLANES, 2 * D_FF), F32)],
        scratch_shapes=[pltpu.VMEM((SUBLANES, 2 * D_FF), F32)],
        compiler_params=_cparams(("parallel", "arbitrary")),
        name="ffn_fused",
    )(h3, state8, n2, wu, cw, cb, wd, fw)


def _down_kernel(a_ref, w_ref, h_ref, nw_ref, y_ref):
    h = h_ref[...] + _dot(a_ref[...], w_ref[...])
    ms = jnp.mean(h * h, axis=-1, keepdims=True)
    y_ref[...] = h * lax.rsqrt(ms + RMS_EPS) * nw_ref[...]


def _down(act, w, h, nw):
    t = h.shape[0]
    tm = min(t, 512)
    assert t % tm == 0
    return pl.pallas_call(
        _down_kernel,
        grid=(t // tm,),
        in_specs=[pl.BlockSpec((tm, D_FF), lambda i: (i, 0)),
                  pl.BlockSpec((D_FF, D_MODEL), lambda i: (0, 0)),
                  pl.BlockSpec((tm, D_MODEL), lambda i: (i, 0)),
                  pl.BlockSpec((1, D_MODEL), lambda i: (0, 0))],
        out_specs=pl.BlockSpec((tm, D_MODEL), lambda i: (i, 0)),
        out_shape=jax.ShapeDtypeStruct((t, D_MODEL), F32),
        compiler_params=_cparams(("parallel",)),
        name="ffn_down",
    )(act, w, h, nw.reshape(1, D_MODEL))


def _pad_state_rows(state):
    nb, k, c = state.shape
    return jnp.concatenate([jnp.zeros((nb, SUBLANES - k, c), state.dtype), state], axis=1)


def _layer(x3, mem_k3, mem_v3, ssm0, conv_state, ffn_state, attend, p):
    nb, l, d = x3.shape
    t = nb * l
    x = x3.reshape(t, d)
    nw1 = p["norm1_w"]
    zx = _norm_matmul(x, nw1, p["w_zx"], tn=2048, out_dtype=BF16, name="proj_zx")
    qkvd = _norm_matmul(x, nw1, p["w_qkvd"], tn=QKVD_WIDTH, name="proj_qkvd")
    qm = _norm_matmul(x, nw1, p["w_qm"], tn=MEM_WIDTH, out_dtype=BF16, name="proj_qm")

    zx3 = zx.reshape(nb, l, D_INNER + CONV_DIM)
    qkvd3 = qkvd.reshape(nb, l, QKVD_WIDTH)
    y_ssd, ssm_new = _ssd(zx3, qkvd3, _pad_state_rows(conv_state), ssm0, p["ssd_conv_w"], p["ssd_conv_b"],
                          p["dt_bias"], p["a_log"], p["d_skip"], p["ssd_norm_w"])
    conv_new = zx3[:, l - (SSD_CONV - 1):, D_INNER:].astype(F32)

    y_att, k_new, v_new = attend(qkvd3)
    y_mem = _mem_attn(qm.reshape(nb, l, MEM_WIDTH), mem_k3, mem_v3)

    h = _merge(x, nw1, p["w_gates"], y_ssd.reshape(t, D_INNER), y_att.reshape(t, ATT_WIDTH), y_mem.reshape(t, MEM_WIDTH),
               p["w_ssd_out"], p["w_attn_out"], p["w_mem_out"], p["w_o"])

    if l % 512 == 0:
        y3, last8 = _ffn_fused(h.reshape(nb, l, d), _pad_state_rows(ffn_state), p["norm2_w"], p["w_up"],
                               p["ffn_conv_w"], p["ffn_conv_b"], p["w_down"], p["final_norm_w"])
        ffn_new = last8[:, SUBLANES - (FFN_CONV - 1):]
        return y3, k_new, v_new, ssm_new, conv_new, ffn_new

    u = _norm_matmul(h, p["norm2_w"], p["w_up"], tn=2 * D_FF // 11, name="ffn_up")
    u3 = u.reshape(nb, l, 2 * D_FF)
    ffn_new = u3[:, l - (FFN_CONV - 1):]
    act = _ffn_gate(u3, _pad_state_rows(ffn_state), p["ffn_conv_w"], p["ffn_conv_b"])
    y = _down(act.reshape(t, D_FF), p["w_down"], h, p["final_norm_w"])
    return y.reshape(nb, l, d), k_new, v_new, ssm_new, conv_new, ffn_new


def _attend_prompt(qkvd3):
    b, s, _ = qkvd3.shape
    tables = _rope_tables(jnp.arange(s, dtype=jnp.int32))
    k_rot, v, q_t, k_aug, v_t, kmean = _prompt_prep(qkvd3, tables)
    y_att = _moba_prompt(q_t, k_aug, v_t, kmean)
    return y_att, k_rot.reshape(b, s, KV_HEADS, ATT_HEAD_DIM), v.reshape(b, s, KV_HEADS, ATT_HEAD_DIM)


def _attend_sample(qkvd3, cache_k, cache_v, page_table):
    ns, t, _ = qkvd3.shape
    npages = page_table.shape[1]
    past = npages * PAGE_SIZE
    pos = past + jnp.arange(t, dtype=jnp.int32)
    tables = tuple(jnp.tile(tb, (ns, 1)) for tb in _rope_tables(pos))
    qk = _rope_sample(qkvd3.reshape(ns * t, QKVD_WIDTH), tables)
    q_rot = qk[:, :ATT_WIDTH].reshape(ns, t, KV_HEADS, GQA_REP, ATT_HEAD_DIM)
    k_rot = qk[:, ATT_WIDTH:].reshape(ns, t, KV_WIDTH)
    v = qkvd3[:, :, ATT_WIDTH + KV_WIDTH:ATT_WIDTH + 2 * KV_WIDTH]
    q_ht = q_rot.transpose(0, 2, 3, 1, 4)
    eye = jnp.eye(KV_HEADS, dtype=F32)
    q_exp = (q_ht[:, :, :, :, None, :] * eye[None, :, None, None, :, None]).reshape(ns, ATT_HEADS * t, KV_WIDTH)
    blk_of_key = jnp.arange(past, dtype=jnp.int32) // MOBA_BLOCK
    e_mat = (jnp.arange(LANES, dtype=jnp.int32)[:, None] == blk_of_key[None, :]).astype(BF16)
    n_phys = cache_k.shape[0]
    to_pages = lambda c: jnp.transpose(c, (0, 2, 3, 1)).reshape(n_phys, KV_WIDTH, PAGE_SIZE)
    y_att = _moba_sample(page_table, q_exp.astype(BF16), k_rot, v, e_mat, to_pages(cache_k), to_pages(cache_v))
    return y_att, k_rot.reshape(ns, t, KV_HEADS, ATT_HEAD_DIM), v.reshape(ns, t, KV_HEADS, ATT_HEAD_DIM)


def _layer_params(l, norm1_w, w_in, ssd_conv_w, ssd_conv_b, dt_bias, a_log, d_skip, ssd_norm_w, w_ssd_out,
                  w_attn_out, w_mem_out, w_o, norm2_w, w_up, ffn_conv_w, ffn_conv_b, w_down, final_norm_w):
    w = w_in[l]
    o_z, o_x, o_dt = 0, D_INNER, D_INNER + CONV_DIM
    o_q = o_dt + SSD_HEADS
    o_k, o_v = o_q + ATT_WIDTH, o_q + ATT_WIDTH + KV_WIDTH
    o_qm = o_v + KV_WIDTH
    o_g = o_qm + MEM_WIDTH
    pad_lanes = lambda a: jnp.pad(a, (0, LANES - a.shape[0])).reshape(1, LANES)
    w_dt = jnp.pad(w[:, o_dt:o_q], ((0, 0), (0, LANES - SSD_HEADS)))
    return {
        "norm1_w": norm1_w[l],
        "w_zx": w[:, o_z:o_dt].astype(BF16),
        "w_qkvd": jnp.concatenate([w[:, o_q:o_qm], w_dt], axis=1).astype(BF16),
        "w_qm": w[:, o_qm:o_g].astype(BF16),
        "w_gates": w[:, o_g:].astype(BF16),
        "ssd_conv_w": ssd_conv_w[l],
        "ssd_conv_b": ssd_conv_b[l].reshape(1, CONV_DIM),
        "dt_bias": pad_lanes(dt_bias[l]),
        "a_log": pad_lanes(a_log[l]),
        "d_skip": jnp.repeat(d_skip[l], SSD_HEAD_DIM).reshape(1, D_INNER),
        "ssd_norm_w": ssd_norm_w[l].reshape(1, D_INNER),
        "w_ssd_out": w_ssd_out[l].astype(BF16),
        "w_attn_out": w_attn_out[l].astype(BF16),
        "w_mem_out": w_mem_out[l].astype(BF16),
        "w_o": w_o[l].astype(BF16),
        "norm2_w": norm2_w[l],
        "w_up": w_up[l].astype(BF16),
        "ffn_conv_w": ffn_conv_w[l],
        "ffn_conv_b": ffn_conv_b[l].reshape(1, 2 * D_FF),
        "w_down": w_down[l].astype(BF16),
        "final_norm_w": final_norm_w,
    }


def kernel(x_prompt, x_sample, cache_k, cache_v, cache_mem_k, cache_mem_v, state_ssm, state_conv,
           state_ffn_conv, page_table, mem_prompt, norm1_w, w_in, ssd_conv_w, ssd_conv_b, dt_bias, a_log,
           d_skip, ssd_norm_w, mem_norm_w, w_mem_kv, w_ssd_out, w_attn_out, w_mem_out, w_o, norm2_w, w_up,
           ffn_conv_w, ffn_conv_b, w_down, final_norm_w):
    depth = w_in.shape[0]
    assert depth == 1, "the final RMSNorm is fused into the single layer's last kernel"
    b_p, s_p, _ = x_prompt.shape
    n_mem = mem_prompt.shape[1]
    ns = x_sample.shape[0]
    l = 0
    p = _layer_params(l, norm1_w, w_in, ssd_conv_w, ssd_conv_b, dt_bias, a_log, d_skip, ssd_norm_w, w_ssd_out,
                      w_attn_out, w_mem_out, w_o, norm2_w, w_up, ffn_conv_w, ffn_conv_b, w_down, final_norm_w)

    memx = mem_prompt.reshape(b_p * n_mem, D_MODEL)
    wkv = w_mem_kv[l].astype(BF16)
    mk_p = _norm_matmul(memx, mem_norm_w[l], wkv[:, :MEM_WIDTH], tn=MEM_WIDTH, name="mem_k")
    mv_p = _norm_matmul(memx, mem_norm_w[l], wkv[:, MEM_WIDTH:], tn=MEM_WIDTH, name="mem_v")
    mk_p3 = mk_p.reshape(b_p, n_mem, MEM_WIDTH)
    mv_p3 = mv_p.reshape(b_p, n_mem, MEM_WIDTH)
    ssm0 = jnp.zeros((b_p, SSD_HEADS, SSD_HEAD_DIM, SSD_STATE), F32)
    conv0 = jnp.zeros((b_p, SSD_CONV - 1, CONV_DIM), F32)
    ffn0 = jnp.zeros((b_p, FFN_CONV - 1, 2 * D_FF), F32)
    y_p, k_p, v_p, s_p_new, c_p, f_p = _layer(x_prompt, mk_p3, mv_p3, ssm0, conv0, ffn0, _attend_prompt, p)

    attend_s = functools.partial(_attend_sample, cache_k=cache_k[l], cache_v=cache_v[l], page_table=page_table)
    y_s, k_s, v_s, s_s_new, c_s, f_s = _layer(x_sample, cache_mem_k[l], cache_mem_v[l], state_ssm[l], state_conv[l],
                                              state_ffn_conv[l], attend_s, p)

    mem_shape = (1, b_p, n_mem, MEM_HEADS, MEM_HEAD_DIM)
    return (y_p, y_s, k_p[None], v_p[None], mk_p.reshape(mem_shape), mv_p.reshape(mem_shape),
            s_p_new[None], c_p[None], f_p[None], k_s[None], v_s[None], s_s_new[None], c_s[None], f_s[None])
```

```python
import functools
import math

import jax
import jax.numpy as jnp
from jax import lax
from jax.experimental import pallas as pl
from jax.experimental.pallas import tpu as pltpu

F32 = jnp.float32
BF16 = jnp.bfloat16

D_MODEL = 1024
D_INNER = 2048
SSD_HEAD_DIM = 64
SSD_HEADS = 32
SSD_GROUPS = 8
SSD_STATE = 128
SSD_CONV = 4
SSD_CHUNK = 128
CONV_DIM = 4096
ATT_HEADS = 16
ATT_HEAD_DIM = 64
KV_HEADS = 4
GQA_REP = 4
ATT_WIDTH = 1024
KV_WIDTH = 256
MOBA_BLOCK = 256
MOBA_TOPK = 3
ROT_DIM = 16
ROPE_THETA = 500000.0
MEM_HEADS = 4
MEM_HEAD_DIM = 256
MEM_WIDTH = 1024
D_FF = 2816
FFN_CONV = 3
RMS_EPS = 1e-6
NEG_INF = -1e30
LOG2_E = 1.4426950408889634
PAGE_SIZE = 128

LANES = 128
SUBLANES = 8
VMEM_LIMIT = 56 * 1024 * 1024
V_ONES = 16

QKVD_WIDTH = ATT_WIDTH + 2 * KV_WIDTH + LANES
DT_COL_BLOCK = (ATT_WIDTH + 2 * KV_WIDTH) // LANES


def _cparams(sem):
    return pltpu.CompilerParams(dimension_semantics=sem, vmem_limit_bytes=VMEM_LIMIT)


def _dot(a, b):
    return jnp.dot(a, b, preferred_element_type=F32)


def _dot_nt(a, b):
    return lax.dot_general(a, b, (((1,), (1,)), ((), ())), preferred_element_type=F32)


def _dot_f32(a, b):
    return jnp.dot(a, b, preferred_element_type=F32, precision=lax.Precision.HIGHEST)


def _silu(x):
    return x * jax.nn.sigmoid(x)


def _norm_matmul_kernel(x_ref, nw_ref, w_ref, o_ref, xn_ref, *, act):
    @pl.when(pl.program_id(1) == 0)
    def _():
        x = x_ref[...]
        ms = jnp.mean(x * x, axis=-1, keepdims=True)
        xn_ref[...] = (x * lax.rsqrt(ms + RMS_EPS) * nw_ref[...]).astype(BF16)

    y = _dot(xn_ref[...], w_ref[...])
    if act == "sigmoid":
        y = jax.nn.sigmoid(y)
    o_ref[...] = y.astype(o_ref.dtype)


def _norm_matmul(x, nw, w, *, tn, out_dtype=F32, act=None, name="norm_matmul"):
    t, d = x.shape
    n = w.shape[1]
    tm = min(t, 1024)
    assert t % tm == 0 and n % tn == 0
    return pl.pallas_call(
        functools.partial(_norm_matmul_kernel, act=act),
        grid=(t // tm, n // tn),
        in_specs=[pl.BlockSpec((tm, d), lambda i, j: (i, 0)),
                  pl.BlockSpec((1, d), lambda i, j: (0, 0)),
                  pl.BlockSpec((d, tn), lambda i, j: (0, j))],
        out_specs=pl.BlockSpec((tm, tn), lambda i, j: (i, j)),
        out_shape=jax.ShapeDtypeStruct((t, n), out_dtype),
        scratch_shapes=[pltpu.VMEM((tm, d), BF16)],
        compiler_params=_cparams(("parallel", "arbitrary")),
        name=name,
    )(x, nw.reshape(1, d), w)


def _conv4_silu(pre, halo, cw, cb):
    w0, w1, w2, w3 = (cw[k:k + 1, :] for k in range(SSD_CONV))
    prev = _shift_rows(pre, 1, halo)
    halo_pair = halo * w1 + pltpu.roll(halo, 1, axis=0) * w0
    acc = _shift_rows(pre * w1 + prev * w0, 2, halo_pair) + (prev * w2 + pre * w3)
    return _silu(acc + cb)


def _proj_zx_conv_kernel(x_ref, nw_ref, w_ref, cw_ref, cb_ref, st_ref, o_ref, xn_ref, carry_ref,
                         *, z_tiles, tiles_per_seq):
    i, j = pl.program_id(0), pl.program_id(1)
    tm = x_ref.shape[0]

    @pl.when(j == 0)
    def _():
        x = x_ref[...]
        ms = jnp.mean(x * x, axis=-1, keepdims=True)
        xn_ref[...] = (x * lax.rsqrt(ms + RMS_EPS) * nw_ref[...]).astype(BF16)

    y = _dot(xn_ref[...], w_ref[...])

    @pl.when(j < z_tiles)
    def _():
        o_ref[...] = y.astype(o_ref.dtype)

    @pl.when(j >= z_tiles)
    def _():
        col = j - z_tiles

        @pl.when(i % tiles_per_seq == 0)
        def _():
            carry_ref[col] = st_ref[0]

        o_ref[...] = _conv4_silu(y, carry_ref[col], cw_ref[...], cb_ref[...]).astype(o_ref.dtype)
        carry_ref[col] = y[tm - SUBLANES:tm, :]


def _proj_zx_conv(x, nw, w, cw, cb, conv_state8, seq_len):
    t, d = x.shape
    n = w.shape[1]
    tm, tn = 1024, 1024
    assert t % tm == 0 and seq_len % tm == 0 and D_INNER % tn == 0 and CONV_DIM % tn == 0
    z_tiles = D_INNER // tn
    tiles_per_seq = seq_len // tm
    xcol = lambda i, j: jnp.maximum(j - z_tiles, 0)
    return pl.pallas_call(
        functools.partial(_proj_zx_conv_kernel, z_tiles=z_tiles, tiles_per_seq=tiles_per_seq),
        grid=(t // tm, n // tn),
        in_specs=[pl.BlockSpec((tm, d), lambda i, j: (i, 0)),
                  pl.BlockSpec((1, d), lambda i, j: (0, 0)),
                  pl.BlockSpec((d, tn), lambda i, j: (0, j)),
                  pl.BlockSpec((SSD_CONV, tn), lambda i, j: (0, xcol(i, j))),
                  pl.BlockSpec((1, tn), lambda i, j: (0, xcol(i, j))),
                  pl.BlockSpec((1, SUBLANES, tn), lambda i, j: (i // tiles_per_seq, 0, xcol(i, j)))],
        out_specs=pl.BlockSpec((tm, tn), lambda i, j: (i, j)),
        out_shape=jax.ShapeDtypeStruct((t, n), BF16),
        scratch_shapes=[pltpu.VMEM((tm, d), BF16),
                        pltpu.VMEM((CONV_DIM // tn, SUBLANES, tn), F32)],
        compiler_params=_cparams(("arbitrary", "arbitrary")),
        name="proj_zx_conv",
    )(x, nw.reshape(1, d), w, cw, cb, conv_state8)


def _shift_rows(x, k, halo):
    r = pltpu.roll(x, k, axis=0)
    row = lax.broadcasted_iota(jnp.int32, halo.shape, 0)
    top = jnp.where(row < k, pltpu.roll(halo, k, axis=0), r[:SUBLANES])
    if x.shape[0] == SUBLANES:
        return top
    return jnp.concatenate([top, r[SUBLANES:]], axis=0)


def _ssd_kernel(z_ref, xa_ref, xb_ref, dt_ref, cst_ref, s0_ref, cw_ref, cb_ref, dtb_ref, alog_ref,
                dsk_ref, nw_ref, y_ref, sf_ref, s_ref, carry_ref, ybuf_ref, *, qb, q, nchunks, conv_done):
    c = pl.program_id(1)

    @pl.when(c == 0)
    def _():
        s_ref[...] = s0_ref[0]
        carry_ref[...] = cst_ref[0]

    pre = jnp.concatenate([xa_ref[0], xb_ref[0]], axis=1).astype(F32)
    if conv_done:
        xbc = pre
    else:
        xbc = _conv4_silu(pre, carry_ref[...], cw_ref[...], cb_ref[...])
        if nchunks > 1:
            carry_ref[...] = pre[qb - SUBLANES:qb, :]

    z = z_ref[0].astype(F32)
    dt_raw = dt_ref[0]
    if qb < q:
        xbc = jnp.concatenate([xbc, jnp.zeros((q - qb, CONV_DIM), F32)], axis=0)
        dt_raw = jnp.concatenate([dt_raw, jnp.zeros((q - qb, LANES), F32)], axis=0)

    xs = xbc[:, :D_INNER]
    xs_bf = xs.astype(BF16)
    xs_t = xs.T

    v = dt_raw + dtb_ref[...]
    dt = jnp.maximum(v, 0.0) + jnp.log1p(jnp.exp(-jnp.abs(v)))
    row_q = lax.broadcasted_iota(jnp.int32, (q, LANES), 0)
    if qb < q:
        dt = jnp.where(row_q < qb, dt, 0.0)
    a = -jnp.exp(alog_ref[...])
    ri = lax.broadcasted_iota(jnp.int32, (q, q), 0)
    ci = lax.broadcasted_iota(jnp.int32, (q, q), 1)
    causal = ci <= ri
    cs = _dot_f32(causal.astype(F32), dt * a)
    cs_t = cs.T
    dt_t = dt.T
    ecs = jnp.exp(cs)

    for g in range(SSD_GROUPS):
        b_g = xbc[:, D_INNER + g * SSD_STATE:D_INNER + (g + 1) * SSD_STATE].astype(BF16)
        c_g = xbc[:qb, D_INNER + SSD_GROUPS * SSD_STATE + g * SSD_STATE:
                  D_INNER + SSD_GROUPS * SSD_STATE + (g + 1) * SSD_STATE].astype(BF16)
        cb = _dot_nt(c_g, b_g)
        for r in range(SSD_HEADS // SSD_GROUPS):
            h = g * (SSD_HEADS // SSD_GROUPS) + r
            lo = h * SSD_HEAD_DIM
            cs_col = cs[:qb, h:h + 1]
            cs_row = cs_t[h:h + 1, :]
            dt_row = dt_t[h:h + 1, :]
            seg = jnp.where(causal[:qb], cs_col - cs_row, -jnp.inf)
            m_h = (cb * jnp.exp(seg) * dt_row).astype(BF16)
            s_h = s_ref[h]
            y_h = _dot(m_h, xs_bf[:, lo:lo + SSD_HEAD_DIM])
            y_h = y_h + _dot_nt(c_g, s_h.astype(BF16)) * ecs[:qb, h:h + 1]
            ybuf_ref[:, lo:lo + SSD_HEAD_DIM] = y_h
            cs_end = cs_row[:, q - 1:q]
            w_row = dt_row * jnp.exp(cs_end - cs_row)
            xw = (xs_t[lo:lo + SSD_HEAD_DIM, :] * w_row).astype(BF16)
            s_ref[h] = s_h * jnp.exp(cs_end) + _dot(xw, b_g)

    y = ybuf_ref[...] + dsk_ref[...] * xs[:qb]
    gt = y * _silu(z)
    gw = D_INNER // SSD_GROUPS
    parts = []
    for g in range(SSD_GROUPS):
        gg = gt[:, g * gw:(g + 1) * gw]
        parts.append(gg * lax.rsqrt(jnp.mean(gg * gg, axis=-1, keepdims=True) + RMS_EPS))
    yn = jnp.concatenate(parts, axis=1) * nw_ref[...]
    y_ref[0] = yn.astype(y_ref.dtype)

    @pl.when(c == nchunks - 1)
    def _():
        sf_ref[0] = s_ref[...]


def _ssd(zx3, qkvd3, conv_state8, ssm0, cw, cb, dtb, alog, dsk, nw, conv_done):
    nb, l, _ = zx3.shape
    q = SSD_CHUNK
    qb = min(l, q)
    assert l % qb == 0 and qb % SUBLANES == 0
    nchunks = l // qb
    half = CONV_DIM // 2
    row_blk = lambda col: (lambda b, c: (b, c, col))
    full2 = lambda b, c: (0, 0)
    return pl.pallas_call(
        functools.partial(_ssd_kernel, qb=qb, q=q, nchunks=nchunks, conv_done=conv_done),
        grid=(nb, nchunks),
        in_specs=[pl.BlockSpec((1, qb, D_INNER), row_blk(0)),
                  pl.BlockSpec((1, qb, half), row_blk(1)),
                  pl.BlockSpec((1, qb, half), row_blk(2)),
                  pl.BlockSpec((1, qb, LANES), row_blk(DT_COL_BLOCK)),
                  pl.BlockSpec((1, SUBLANES, CONV_DIM), lambda b, c: (b, 0, 0)),
                  pl.BlockSpec((1, SSD_HEADS, SSD_HEAD_DIM, SSD_STATE), lambda b, c: (b, 0, 0, 0)),
                  pl.BlockSpec((SSD_CONV, CONV_DIM), full2),
                  pl.BlockSpec((1, CONV_DIM), full2),
                  pl.BlockSpec((1, LANES), full2),
                  pl.BlockSpec((1, LANES), full2),
                  pl.BlockSpec((1, D_INNER), full2),
                  pl.BlockSpec((1, D_INNER), full2)],
        out_specs=[pl.BlockSpec((1, qb, D_INNER), lambda b, c: (b, c, 0)),
                   pl.BlockSpec((1, SSD_HEADS, SSD_HEAD_DIM, SSD_STATE), lambda b, c: (b, 0, 0, 0))],
        out_shape=[jax.ShapeDtypeStruct((nb, l, D_INNER), BF16),
                   jax.ShapeDtypeStruct((nb, SSD_HEADS, SSD_HEAD_DIM, SSD_STATE), F32)],
        scratch_shapes=[pltpu.VMEM((SSD_HEADS, SSD_HEAD_DIM, SSD_STATE), F32),
                        pltpu.VMEM((SUBLANES, CONV_DIM), F32),
                        pltpu.VMEM((qb, D_INNER), F32)],
        compiler_params=_cparams(("parallel", "arbitrary")),
        name="ssd",
    )(zx3, zx3, zx3, qkvd3, conv_state8, ssm0, cw, cb, dtb, alog, dsk, nw)


def _rope_tables(pos):
    half = ROT_DIM // 2
    inv = ROPE_THETA ** (-jnp.arange(half, dtype=F32) * 2.0 / ROT_DIM)
    ang = pos.astype(F32)[:, None] * inv[None, :]
    cos, sin = jnp.cos(ang), jnp.sin(ang)
    n = pos.shape[0]
    pad = jnp.zeros((n, ATT_HEAD_DIM - ROT_DIM), F32)
    zero = jnp.zeros((n, half), F32)
    c_head = jnp.concatenate([cos, cos, pad + 1.0], axis=1)
    s1_head = jnp.concatenate([-sin, zero, pad], axis=1)
    s2_head = jnp.concatenate([zero, sin, pad], axis=1)
    rep = LANES // ATT_HEAD_DIM
    return jnp.tile(c_head, (1, rep)), jnp.tile(s1_head, (1, rep)), jnp.tile(s2_head, (1, rep))


def _rope_group(xg, c, s1, s2):
    half = ROT_DIM // 2
    return xg * c + pltpu.roll(xg, LANES - half, axis=1) * s1 + pltpu.roll(xg, half, axis=1) * s2


def _head_from_group(xg, odd, lane):
    if odd:
        xg = pltpu.roll(xg, ATT_HEAD_DIM, axis=1)
    return jnp.where(lane < ATT_HEAD_DIM, xg, 0.0)


def _prompt_prep_kernel(q_ref, k_ref, v_ref, c_ref, s1_ref, s2_ref,
                        krot_ref, vout_ref, qt_ref, kaug_ref, vt_ref, kmean_ref):
    i = pl.program_id(1)
    rows = q_ref.shape[1]
    c, s1, s2 = c_ref[...], s1_ref[...], s2_ref[...]
    lane = lax.broadcasted_iota(jnp.int32, (rows, LANES), 1)
    scale = ATT_HEAD_DIM ** -0.5 * LOG2_E
    hd = ATT_HEAD_DIM

    @pl.when(i == 0)
    def _():
        kmean_ref[...] = jnp.zeros_like(kmean_ref)

    for cg in range(ATT_WIDTH // LANES):
        qg = _rope_group(q_ref[0, :, cg * LANES:(cg + 1) * LANES], c, s1, s2) * scale
        qg_t = qg.T.astype(BF16)
        for odd in range(2):
            g, r = divmod(2 * cg + odd, GQA_REP)
            qt_ref[0, g, 0, :, r * rows:(r + 1) * rows] = qg_t[odd * hd:(odd + 1) * hd, :]

    onehot = jnp.where(lane == ATT_HEAD_DIM + i, 1.0, 0.0)
    for cg in range(KV_WIDTH // LANES):
        kg = _rope_group(k_ref[0, :, cg * LANES:(cg + 1) * LANES], c, s1, s2)
        krot_ref[0, :, cg * LANES:(cg + 1) * LANES] = kg
        vg = v_ref[0, :, cg * LANES:(cg + 1) * LANES]
        vout_ref[0, :, cg * LANES:(cg + 1) * LANES] = vg
        vg_t = vg.T.astype(BF16)
        for odd in range(2):
            g = 2 * cg + odd
            kh = _head_from_group(kg, odd, lane)
            kaug_ref[0, g] = (kh + onehot).astype(BF16)
            vt_ref[0, g, 0:hd, :] = vg_t[odd * hd:(odd + 1) * hd, :]
            vt_ref[0, g, hd:, :] = jnp.ones((V_ONES, rows), BF16)
            kmean_ref[0, g, pl.ds(i, 1), :] = jnp.mean(kh, axis=0, keepdims=True)


def _prompt_prep(qkvd3, tables):
    b, s, _ = qkvd3.shape
    nblk = s // MOBA_BLOCK
    assert s % MOBA_BLOCK == 0 and nblk <= SUBLANES
    blk = MOBA_BLOCK
    tab = pl.BlockSpec((blk, LANES), lambda bi, i: (i, 0))
    return pl.pallas_call(
        _prompt_prep_kernel,
        grid=(b, nblk),
        in_specs=[pl.BlockSpec((1, blk, ATT_WIDTH), lambda bi, i: (bi, i, 0)),
                  pl.BlockSpec((1, blk, KV_WIDTH), lambda bi, i: (bi, i, ATT_WIDTH // KV_WIDTH)),
                  pl.BlockSpec((1, blk, KV_WIDTH), lambda bi, i: (bi, i, ATT_WIDTH // KV_WIDTH + 1)),
                  tab, tab, tab],
        out_specs=[pl.BlockSpec((1, blk, KV_WIDTH), lambda bi, i: (bi, i, 0)),
                   pl.BlockSpec((1, blk, KV_WIDTH), lambda bi, i: (bi, i, 0)),
                   pl.BlockSpec((1, KV_HEADS, 1, ATT_HEAD_DIM, GQA_REP * blk), lambda bi, i: (bi, 0, i, 0, 0)),
                   pl.BlockSpec((1, KV_HEADS, blk, LANES), lambda bi, i: (bi, 0, i, 0)),
                   pl.BlockSpec((1, KV_HEADS, ATT_HEAD_DIM + V_ONES, blk), lambda bi, i: (bi, 0, 0, i)),
                   pl.BlockSpec((1, KV_HEADS, SUBLANES, LANES), lambda bi, i: (bi, 0, 0, 0))],
        out_shape=[jax.ShapeDtypeStruct((b, s, KV_WIDTH), F32),
                   jax.ShapeDtypeStruct((b, s, KV_WIDTH), F32),
                   jax.ShapeDtypeStruct((b, KV_HEADS, nblk, ATT_HEAD_DIM, GQA_REP * blk), BF16),
                   jax.ShapeDtypeStruct((b, KV_HEADS, s, LANES), BF16),
                   jax.ShapeDtypeStruct((b, KV_HEADS, ATT_HEAD_DIM + V_ONES, s), BF16),
                   jax.ShapeDtypeStruct((b, KV_HEADS, SUBLANES, LANES), F32)],
        compiler_params=_cparams(("parallel", "arbitrary")),
        name="prompt_prep",
    )(qkvd3, qkvd3, qkvd3, *tables)


def _topk_rows(gate, valid, row_f, rounds=MOBA_TOPK):
    g0 = jnp.where(valid, gate, -jnp.inf)
    sel = jnp.zeros(gate.shape, dtype=jnp.bool_)
    for _ in range(rounds):
        m = jnp.max(g0, axis=0, keepdims=True)
        idx = jnp.min(jnp.where(g0 == m, row_f, 1e9), axis=0, keepdims=True)
        pick = jnp.logical_and(row_f == idx, m > -jnp.inf)
        sel = jnp.logical_or(sel, pick)
        g0 = jnp.where(pick, -jnp.inf, g0)
    return sel


def _topk_lanes(gate, valid, lane_f, rounds=MOBA_TOPK):
    g0 = jnp.where(valid, gate, -jnp.inf)
    sel = jnp.zeros(gate.shape, dtype=jnp.bool_)
    for _ in range(rounds):
        m = jnp.max(g0, axis=1, keepdims=True)
        idx = jnp.min(jnp.where(g0 == m, lane_f, 1e9), axis=1, keepdims=True)
        pick = jnp.logical_and(lane_f == idx, m > -jnp.inf)
        sel = jnp.logical_or(sel, pick)
        g0 = jnp.where(pick, -jnp.inf, g0)
    return sel


def _moba_prompt_block(n_past, q_ref, k_ref, v_ref, km_ref, y_ref):
    blk = MOBA_BLOCK
    hd = ATT_HEAD_DIM
    nq = GQA_REP * blk
    qt = q_ref[0, 0, 0]
    q0 = jnp.concatenate([qt, jnp.zeros((LANES - hd, nq), BF16)], axis=0)
    k_own = k_ref[0, 0, n_past * blk:(n_past + 1) * blk, :]
    key_i = lax.broadcasted_iota(jnp.int32, (blk, nq), 0)
    q_i = lax.broadcasted_iota(jnp.int32, (blk, nq), 1) % blk
    s_own = jnp.where(key_i <= q_i, _dot(k_own, q0), NEG_INF)
    m = jnp.max(s_own, axis=0, keepdims=True)
    if n_past > 0:
        pad8 = jnp.zeros((SUBLANES, LANES), F32)
        gate = _dot(jnp.concatenate([km_ref[0, 0], pad8], axis=0).astype(BF16), q0)
        row = lax.broadcasted_iota(jnp.int32, gate.shape, 0)
        sel = _topk_rows(gate, row < n_past, row.astype(F32), rounds=min(MOBA_TOPK, n_past))
        selb = jnp.where(row < SUBLANES, jnp.where(sel, 0.0, NEG_INF), 0.0).astype(BF16)
        q_aug = jnp.concatenate([qt, selb, jnp.zeros((LANES - hd - 2 * SUBLANES, nq), BF16)], axis=0)
        s_past = _dot(k_ref[0, 0, 0:n_past * blk, :], q_aug)
        m = jnp.maximum(m, jnp.max(s_past, axis=0, keepdims=True))
    p_own = jnp.exp2(s_own - m)
    o = _dot(v_ref[0, 0, :, n_past * blk:(n_past + 1) * blk], p_own.astype(BF16))
    if n_past > 0:
        p_past = jnp.exp2(s_past - m)
        o = o + _dot(v_ref[0, 0, :, 0:n_past * blk], p_past.astype(BF16))
    o = o[0:hd] / o[hd:hd + 1]
    for pair in range(GQA_REP // 2):
        two = jnp.concatenate([o[:, 2 * pair * blk:(2 * pair + 1) * blk],
                               o[:, (2 * pair + 1) * blk:(2 * pair + 2) * blk]], axis=0)
        y_ref[:, pair * LANES:(pair + 1) * LANES] = two.T.astype(y_ref.dtype)


def _moba_prompt_kernel(q_ref, k_ref, v_ref, km_ref, y_ref, *, nblk):
    i = pl.program_id(2)
    for n_past in range(nblk):
        pl.when(i == n_past)(functools.partial(_moba_prompt_block, n_past, q_ref, k_ref, v_ref, km_ref, y_ref))


def _moba_prompt(q_t, k_aug, v_t, kmean):
    b, _, s, _ = k_aug.shape
    nblk = s // MOBA_BLOCK
    blk = MOBA_BLOCK
    return pl.pallas_call(
        functools.partial(_moba_prompt_kernel, nblk=nblk),
        grid=(b, KV_HEADS, nblk),
        in_specs=[pl.BlockSpec((1, 1, 1, ATT_HEAD_DIM, GQA_REP * blk), lambda bi, g, i: (bi, g, i, 0, 0)),
                  pl.BlockSpec((1, 1, s, LANES), lambda bi, g, i: (bi, g, 0, 0)),
                  pl.BlockSpec((1, 1, ATT_HEAD_DIM + V_ONES, s), lambda bi, g, i: (bi, g, 0, 0)),
                  pl.BlockSpec((1, 1, SUBLANES, LANES), lambda bi, g, i: (bi, g, 0, 0))],
        out_specs=pl.BlockSpec((blk, GQA_REP * ATT_HEAD_DIM), lambda bi, g, i: (bi * nblk + i, g)),
        out_shape=jax.ShapeDtypeStruct((b * s, ATT_WIDTH), BF16),
        compiler_params=_cparams(("parallel", "parallel", "arbitrary")),
        name="moba_prompt",
    )(q_t, k_aug, v_t, kmean)


def _rope_kernel(x_ref, c_ref, s1_ref, s2_ref, o_ref, *, q_groups):
    c, s1, s2 = c_ref[...], s1_ref[...], s2_ref[...]
    scale = ATT_HEAD_DIM ** -0.5
    for cg in range(x_ref.shape[1] // LANES):
        xg = _rope_group(x_ref[:, cg * LANES:(cg + 1) * LANES], c, s1, s2)
        if cg < q_groups:
            xg = xg * scale
        o_ref[:, cg * LANES:(cg + 1) * LANES] = xg


def _rope_sample(qkvd, tables):
    t = qkvd.shape[0]
    w = ATT_WIDTH + KV_WIDTH
    tab = pl.BlockSpec((t, LANES), lambda i: (0, 0))
    return pl.pallas_call(
        functools.partial(_rope_kernel, q_groups=ATT_WIDTH // LANES),
        grid=(1,),
        in_specs=[pl.BlockSpec((t, w), lambda i: (0, 0)), tab, tab, tab],
        out_specs=pl.BlockSpec((t, w), lambda i: (0, 0)),
        out_shape=jax.ShapeDtypeStruct((t, w), F32),
        compiler_params=_cparams(("arbitrary",)),
        name="rope_sample",
    )(qkvd, *tables)


def _moba_sample_kernel(pt_ref, qx_ref, kn_ref, vn_ref, e_ref, ck_ref, cv_ref, y_ref,
                        kbuf, vbuf, sem, *, npages, t):
    s = pl.program_id(0)
    ns = pl.num_programs(0)
    slot = s % 2

    def k_copy(seq, p, sl):
        return pltpu.make_async_copy(ck_ref.at[pt_ref[seq, p]], kbuf.at[sl, :, p * PAGE_SIZE:(p + 1) * PAGE_SIZE],
                                     sem.at[0, sl])

    def v_copy(seq, p, sl):
        return pltpu.make_async_copy(cv_ref.at[pt_ref[seq, p]], vbuf.at[sl, :, p * PAGE_SIZE:(p + 1) * PAGE_SIZE],
                                     sem.at[1, sl])

    def start_all(seq, sl):
        for p in range(npages):
            k_copy(seq, p, sl).start()
            v_copy(seq, p, sl).start()

    @pl.when(s == 0)
    def _():
        start_all(0, 0)

    @pl.when(s + 1 < ns)
    def _():
        start_all(s + 1, 1 - slot)

    for p in range(npages):
        k_copy(s, p, slot).wait()
        v_copy(s, p, slot).wait()

    past = npages * PAGE_SIZE
    nblk = past // MOBA_BLOCK
    rows = ATT_HEADS * t
    chunk = min(past, 2048)
    qx = qx_ref[0]

    s_raw = jnp.concatenate(
        [_dot(qx, kbuf[slot, :, c * chunk:(c + 1) * chunk].astype(BF16)) for c in range(past // chunk)], axis=1)
    lane = lax.broadcasted_iota(jnp.int32, (rows, LANES), 1)
    gate = jnp.zeros((rows, LANES), F32)
    for n in range(nblk):
        col = jnp.sum(s_raw[:, n * MOBA_BLOCK:(n + 1) * MOBA_BLOCK], axis=1, keepdims=True)
        gate = jnp.where(lane == n, col, gate)
    sel = _topk_lanes(gate, lane < nblk, lane.astype(F32))
    selb = jnp.where(sel, 0.0, NEG_INF).astype(BF16)
    s_past = s_raw + _dot(selb, e_ref[...])

    kn = jnp.concatenate([kn_ref[0], jnp.zeros((LANES - t, KV_WIDTH), F32)], axis=0).astype(BF16)
    vn = jnp.concatenate([vn_ref[0], jnp.zeros((LANES - t, KV_WIDTH), F32)], axis=0).astype(BF16)
    row = lax.broadcasted_iota(jnp.int32, (rows, LANES), 0)
    own_ok = lane <= (row % t)
    s_own = jnp.where(own_ok, _dot_nt(qx, kn), NEG_INF)

    m = jnp.maximum(jnp.max(s_past, axis=1, keepdims=True), jnp.max(s_own, axis=1, keepdims=True))
    p_past = jnp.exp(s_past - m)
    p_own = jnp.exp(s_own - m)
    den = jnp.sum(p_past, axis=1, keepdims=True) + jnp.sum(p_own, axis=1, keepdims=True)
    o = _dot(p_own.astype(BF16), vn)
    for c in range(past // chunk):
        o = o + _dot_nt(p_past[:, c * chunk:(c + 1) * chunk].astype(BF16),
                        vbuf[slot, :, c * chunk:(c + 1) * chunk].astype(BF16))
    o = o / den
    for h in range(ATT_HEADS):
        g = h // GQA_REP
        y_ref[0, :, h * ATT_HEAD_DIM:(h + 1) * ATT_HEAD_DIM] = (
            o[h * t:(h + 1) * t, g * ATT_HEAD_DIM:(g + 1) * ATT_HEAD_DIM].astype(y_ref.dtype))


def _moba_sample(page_table, q_exp, k_new, v_new, e_mat, cache_k, cache_v):
    ns, npages = page_table.shape
    t = k_new.shape[1]
    past = npages * PAGE_SIZE
    assert past % MOBA_BLOCK == 0 and past // MOBA_BLOCK <= LANES and t == SUBLANES
    rows = ATT_HEADS * t
    grid_spec = pltpu.PrefetchScalarGridSpec(
        num_scalar_prefetch=1,
        grid=(ns,),
        in_specs=[pl.BlockSpec((1, rows, KV_WIDTH), lambda s, pt: (s, 0, 0)),
                  pl.BlockSpec((1, t, KV_WIDTH), lambda s, pt: (s, 0, 0)),
                  pl.BlockSpec((1, t, KV_WIDTH), lambda s, pt: (s, 0, 0)),
                  pl.BlockSpec((LANES, past), lambda s, pt: (0, 0)),
                  pl.BlockSpec(memory_space=pl.ANY),
                  pl.BlockSpec(memory_space=pl.ANY)],
        out_specs=pl.BlockSpec((1, t, ATT_WIDTH), lambda s, pt: (s, 0, 0)),
        scratch_shapes=[pltpu.VMEM((2, KV_WIDTH, past), F32),
                        pltpu.VMEM((2, KV_WIDTH, past), F32),
                        pltpu.SemaphoreType.DMA((2, 2))],
    )
    return pl.pallas_call(
        functools.partial(_moba_sample_kernel, npages=npages, t=t),
        grid_spec=grid_spec,
        out_shape=jax.ShapeDtypeStruct((ns, t, ATT_WIDTH), BF16),
        compiler_params=_cparams(("arbitrary",)),
        name="moba_sample",
    )(page_table, q_exp, k_new, v_new, e_mat, cache_k, cache_v)


def _mem_head(q, mk, mv):
    q = (q.astype(F32) * MEM_HEAD_DIM ** -0.5).astype(BF16)
    s = _dot_nt(q, mk.astype(BF16))
    m = jnp.max(s, axis=1, keepdims=True)
    p = jnp.exp(s - m)
    den = jnp.sum(p, axis=1, keepdims=True)
    return _dot(p.astype(BF16), mv.astype(BF16)) / den


def _mem_attn_kernel(q_ref, mk_ref, mv_ref, y_ref):
    for h in range(MEM_HEADS):
        lo, hi = h * MEM_HEAD_DIM, (h + 1) * MEM_HEAD_DIM
        y_ref[0, :, lo:hi] = _mem_head(q_ref[0, :, lo:hi], mk_ref[0, :, lo:hi], mv_ref[0, :, lo:hi]).astype(y_ref.dtype)


def _mem_attn_paged_kernel(q_ref, mk_hbm, mv_hbm, y_ref, kbuf, vbuf, sem):
    s = pl.program_id(0)
    ns = pl.num_programs(0)
    slot = s % 2

    def copies(seq, sl):
        out = []
        for h in range(MEM_HEADS):
            out.append(pltpu.make_async_copy(mk_hbm.at[seq, :, h, :], kbuf.at[sl, h], sem.at[0, sl]))
            out.append(pltpu.make_async_copy(mv_hbm.at[seq, :, h, :], vbuf.at[sl, h], sem.at[1, sl]))
        return out

    @pl.when(s == 0)
    def _():
        for c in copies(0, 0):
            c.start()

    @pl.when(s + 1 < ns)
    def _():
        for c in copies(s + 1, 1 - slot):
            c.start()

    for c in copies(s, slot):
        c.wait()
    for h in range(MEM_HEADS):
        lo, hi = h * MEM_HEAD_DIM, (h + 1) * MEM_HEAD_DIM
        y_ref[0, :, lo:hi] = _mem_head(q_ref[0, :, lo:hi], kbuf[slot, h], vbuf[slot, h]).astype(y_ref.dtype)


def _mem_attn_paged(qm3, mk4, mv4):
    ns, l, _ = qm3.shape
    m = mk4.shape[1]
    return pl.pallas_call(
        _mem_attn_paged_kernel,
        grid=(ns,),
        in_specs=[pl.BlockSpec((1, l, MEM_WIDTH), lambda s: (s, 0, 0)),
                  pl.BlockSpec(memory_space=pl.ANY), pl.BlockSpec(memory_space=pl.ANY)],
        out_specs=pl.BlockSpec((1, l, MEM_WIDTH), lambda s: (s, 0, 0)),
        out_shape=jax.ShapeDtypeStruct((ns, l, MEM_WIDTH), BF16),
        scratch_shapes=[pltpu.VMEM((2, MEM_HEADS, m, MEM_HEAD_DIM), F32),
                        pltpu.VMEM((2, MEM_HEADS, m, MEM_HEAD_DIM), F32),
                        pltpu.SemaphoreType.DMA((2, 2))],
        compiler_params=_cparams(("arbitrary",)),
        name="mem_attn_paged",
    )(qm3, mk4, mv4)


def _mem_attn(qm3, mk, mv):
    nb, l, _ = qm3.shape
    if mk.ndim == 4:
        return _mem_attn_paged(qm3, mk, mv)
    m = mk.shape[1]
    tl = min(l, 512)
    assert l % tl == 0
    mem_spec = pl.BlockSpec((1, m, MEM_WIDTH), lambda b, i: (b, 0, 0))
    return pl.pallas_call(
        _mem_attn_kernel,
        grid=(nb, l // tl),
        in_specs=[pl.BlockSpec((1, tl, MEM_WIDTH), lambda b, i: (b, i, 0)), mem_spec, mem_spec],
        out_specs=pl.BlockSpec((1, tl, MEM_WIDTH), lambda b, i: (b, i, 0)),
        out_shape=jax.ShapeDtypeStruct((nb, l, MEM_WIDTH), BF16),
        compiler_params=_cparams(("parallel", "arbitrary")),
        name="mem_attn",
    )(qm3, mk, mv)


def _merge_kernel(x_ref, nw_ref, wg_ref, ys_ref, ya_ref, ym_ref, ws_ref, wa_ref, wm_ref, wo_ref, h_ref):
    d = D_MODEL
    x = x_ref[...]
    ms = jnp.mean(x * x, axis=-1, keepdims=True)
    xn = (x * lax.rsqrt(ms + RMS_EPS) * nw_ref[...]).astype(BF16)

    def gate(k):
        return jax.nn.sigmoid(_dot(xn, wg_ref[:, k * d:(k + 1) * d]))

    merged = gate(0) * _dot(ys_ref[...], ws_ref[...])
    merged = merged + gate(1) * _dot(ya_ref[...], wa_ref[...])
    merged = merged + gate(2) * _dot(ym_ref[...], wm_ref[...])
    h_ref[...] = x + _dot(merged.astype(BF16), wo_ref[...])


def _merge(x, nw, wg, y_ssd, y_att, y_mem, ws, wa, wm, wo):
    t = x.shape[0]
    tm = min(t, 512)
    assert t % tm == 0
    nw = nw.reshape(1, D_MODEL)
    rows = lambda w: pl.BlockSpec((tm, w), lambda i: (i, 0))
    const = lambda a: pl.BlockSpec(a.shape, lambda i: (0, 0), pipeline_mode=pl.Buffered(1))
    return pl.pallas_call(
        _merge_kernel,
        grid=(t // tm,),
        in_specs=[rows(D_MODEL), const(nw), const(wg), rows(D_INNER), rows(ATT_WIDTH), rows(MEM_WIDTH),
                  const(ws), const(wa), const(wm), const(wo)],
        out_specs=rows(D_MODEL),
        out_shape=jax.ShapeDtypeStruct((t, D_MODEL), F32),
        compiler_params=_cparams(("parallel",)),
        name="merge",
    )(x, nw, wg, y_ssd, y_att, y_mem, ws, wa, wm, wo)


def _ffn_gate_kernel(u_ref, st_ref, w_ref, b_ref, a_ref, carry_ref, *, tl, ntiles):
    i = pl.program_id(1)

    @pl.when(i == 0)
    def _():
        carry_ref[...] = st_ref[0]

    u = u_ref[0]
    halo = carry_ref[...]
    acc = _shift_rows(u, 2, halo) * w_ref[0:1, :]
    acc = acc + _shift_rows(u, 1, halo) * w_ref[1:2, :]
    acc = acc + u * w_ref[2:3, :]
    uc = acc + b_ref[...]
    if ntiles > 1:
        carry_ref[...] = u[tl - SUBLANES:tl, :]
    a_ref[0] = (_silu(uc[:, :D_FF]) * uc[:, D_FF:]).astype(a_ref.dtype)


def _ffn_gate(u3, state8, w, b):
    nb, l, c = u3.shape
    tl = min(l, 256)
    assert l % tl == 0 and tl % SUBLANES == 0
    ntiles = l // tl
    return pl.pallas_call(
        functools.partial(_ffn_gate_kernel, tl=tl, ntiles=ntiles),
        grid=(nb, ntiles),
        in_specs=[pl.BlockSpec((1, tl, c), lambda bi, i: (bi, i, 0)),
                  pl.BlockSpec((1, SUBLANES, c), lambda bi, i: (bi, 0, 0)),
                  pl.BlockSpec((FFN_CONV, c), lambda bi, i: (0, 0)),
                  pl.BlockSpec((1, c), lambda bi, i: (0, 0))],
        out_specs=pl.BlockSpec((1, tl, D_FF), lambda bi, i: (bi, i, 0)),
        out_shape=jax.ShapeDtypeStruct((nb, l, D_FF), BF16),
        scratch_shapes=[pltpu.VMEM((SUBLANES, c), F32)],
        compiler_params=_cparams(("parallel", "arbitrary")),
        name="ffn_gate",
    )(u3, state8, w, b)


def _ffn_fused_kernel(h_ref, st_ref, n2_ref, wu_ref, cw_ref, cb_ref, wd_ref, fw_ref, y_ref, last_ref, carry_ref,
                      *, tl, ntiles, ck):
    i = pl.program_id(1)

    @pl.when(i == 0)
    def _():
        carry_ref[...] = st_ref[0]

    h = h_ref[0]
    ms = jnp.mean(h * h, axis=-1, keepdims=True)
    hn = (h * lax.rsqrt(ms + RMS_EPS) * n2_ref[...]).astype(BF16)

    def conv(u, off):
        halo = carry_ref[:, off:off + ck]
        acc = _shift_rows(u, 2, halo) * cw_ref[0:1, off:off + ck]
        acc = acc + _shift_rows(u, 1, halo) * cw_ref[1:2, off:off + ck]
        acc = acc + u * cw_ref[2:3, off:off + ck]
        carry_ref[:, off:off + ck] = u[tl - SUBLANES:tl, :]
        return acc + cb_ref[:, off:off + ck]

    out = h
    for c in range(D_FF // ck):
        lo = c * ck
        ug = conv(_dot(hn, wu_ref[:, lo:lo + ck]), lo)
        uv = conv(_dot(hn, wu_ref[:, D_FF + lo:D_FF + lo + ck]), D_FF + lo)
        out = out + _dot((_silu(ug) * uv).astype(BF16), wd_ref[lo:lo + ck, :])
    ms = jnp.mean(out * out, axis=-1, keepdims=True)
    y_ref[0] = out * lax.rsqrt(ms + RMS_EPS) * fw_ref[...]

    @pl.when(i == ntiles - 1)
    def _():
        last_ref[0] = carry_ref[...]


def _ffn_fused(h3, state8, n2, wu, cw, cb, wd, fw):
    nb, l, d = h3.shape
    tl = 512
    ck = D_FF // 2
    assert l % tl == 0 and ck % LANES == 0
    ntiles = l // tl
    const = lambda a: pl.BlockSpec(a.shape, lambda bi, i: (0,) * a.ndim, pipeline_mode=pl.Buffered(1))
    n2, fw = n2.reshape(1, d), fw.reshape(1, d)
    return pl.pallas_call(
        functools.partial(_ffn_fused_kernel, tl=tl, ntiles=ntiles, ck=ck),
        grid=(nb, ntiles),
        in_specs=[pl.BlockSpec((1, tl, d), lambda bi, i: (bi, i, 0)),
                  pl.BlockSpec((1, SUBLANES, 2 * D_FF), lambda bi, i: (bi, 0, 0)),
                  const(n2), const(wu), const(cw), const(cb), const(wd), const(fw)],
        out_specs=[pl.BlockSpec((1, tl, d), lambda bi, i: (bi, i, 0)),
                   pl.BlockSpec((1, SUBLANES, 2 * D_FF), lambda bi, i: (bi, 0, 0))],
        out_shape=[jax.ShapeDtypeStruct((nb, l, d), F32),
                   jax.ShapeDtypeStruct((nb, SUBLANES, 2 * D_FF), F32)],
        scratch_shapes=[pltpu.VMEM((SUBLANES, 2 * D_FF), F32)],
        compiler_params=_cparams(("parallel", "arbitrary")),
        name="ffn_fused",
    )(h3, state8, n2, wu, cw, cb, wd, fw)


def _down_kernel(a_ref, w_ref, h_ref, nw_ref, y_ref):
    h = h_ref[...] + _dot(a_ref[...], w_ref[...])
    ms = jnp.mean(h * h, axis=-1, keepdims=True)
    y_ref[...] = h * lax.rsqrt(ms + RMS_EPS) * nw_ref[...]


def _down(act, w, h, nw):
    t = h.shape[0]
    tm = min(t, 512)
    assert t % tm == 0
    return pl.pallas_call(
        _down_kernel,
        grid=(t // tm,),
        in_specs=[pl.BlockSpec((tm, D_FF), lambda i: (i, 0)),
                  pl.BlockSpec((D_FF, D_MODEL), lambda i: (0, 0)),
                  pl.BlockSpec((tm, D_MODEL), lambda i: (i, 0)),
                  pl.BlockSpec((1, D_MODEL), lambda i: (0, 0))],
        out_specs=pl.BlockSpec((tm, D_MODEL), lambda i: (i, 0)),
        out_shape=jax.ShapeDtypeStruct((t, D_MODEL), F32),
        compiler_params=_cparams(("parallel",)),
        name="ffn_down",
    )(act, w, h, nw.reshape(1, D_MODEL))


def _pad_state_rows(state):
    nb, k, c = state.shape
    return jnp.concatenate([jnp.zeros((nb, SUBLANES - k, c), state.dtype), state], axis=1)


def _layer(x3, mem_k3, mem_v3, ssm0, conv_state, ffn_state, attend, p):
    nb, l, d = x3.shape
    t = nb * l
    x = x3.reshape(t, d)
    nw1 = p["norm1_w"]
    conv_state8 = _pad_state_rows(conv_state)
    conv_in_proj = l % 1024 == 0
    if conv_in_proj:
        zx = _proj_zx_conv(x, nw1, p["w_zx"], p["ssd_conv_w"], p["ssd_conv_b"], conv_state8, l)
        x_tail = x3[:, l - SUBLANES:, :].reshape(nb * SUBLANES, d)
        tail = _norm_matmul(x_tail, nw1, p["w_zx"][:, D_INNER:], tn=CONV_DIM // 2, name="proj_conv_tail")
        conv_new = tail.reshape(nb, SUBLANES, CONV_DIM)[:, SUBLANES - (SSD_CONV - 1):]
    else:
        zx = _norm_matmul(x, nw1, p["w_zx"], tn=2048, out_dtype=BF16, name="proj_zx")
        conv_new = zx.reshape(nb, l, D_INNER + CONV_DIM)[:, l - (SSD_CONV - 1):, D_INNER:].astype(F32)
    qkvd = _norm_matmul(x, nw1, p["w_qkvd"], tn=QKVD_WIDTH, name="proj_qkvd")
    qm = _norm_matmul(x, nw1, p["w_qm"], tn=MEM_WIDTH, out_dtype=BF16, name="proj_qm")

    zx3 = zx.reshape(nb, l, D_INNER + CONV_DIM)
    qkvd3 = qkvd.reshape(nb, l, QKVD_WIDTH)
    y_ssd, ssm_new = _ssd(zx3, qkvd3, conv_state8, ssm0, p["ssd_conv_w"], p["ssd_conv_b"],
                          p["dt_bias"], p["a_log"], p["d_skip"], p["ssd_norm_w"], conv_in_proj)

    y_att, k_new, v_new = attend(qkvd3)
    y_mem = _mem_attn(qm.reshape(nb, l, MEM_WIDTH), mem_k3, mem_v3)

    h = _merge(x, nw1, p["w_gates"], y_ssd.reshape(t, D_INNER), y_att.reshape(t, ATT_WIDTH), y_mem.reshape(t, MEM_WIDTH),
               p["w_ssd_out"], p["w_attn_out"], p["w_mem_out"], p["w_o"])

    if l % 512 == 0:
        y3, last8 = _ffn_fused(h.reshape(nb, l, d), _pad_state_rows(ffn_state), p["norm2_w"], p["w_up"],
                               p["ffn_conv_w"], p["ffn_conv_b"], p["w_down"], p["final_norm_w"])
        ffn_new = last8[:, SUBLANES - (FFN_CONV - 1):]
        return y3, k_new, v_new, ssm_new, conv_new, ffn_new

    u = _norm_matmul(h, p["norm2_w"], p["w_up"], tn=2 * D_FF // 11, name="ffn_up")
    u3 = u.reshape(nb, l, 2 * D_FF)
    ffn_new = u3[:, l - (FFN_CONV - 1):]
    act = _ffn_gate(u3, _pad_state_rows(ffn_state), p["ffn_conv_w"], p["ffn_conv_b"])
    y = _down(act.reshape(t, D_FF), p["w_down"], h, p["final_norm_w"])
    return y.reshape(nb, l, d), k_new, v_new, ssm_new, conv_new, ffn_new


def _attend_prompt(qkvd3):
    b, s, _ = qkvd3.shape
    tables = _rope_tables(jnp.arange(s, dtype=jnp.int32))
    k_rot, v, q_t, k_aug, v_t, kmean = _prompt_prep(qkvd3, tables)
    y_att = _moba_prompt(q_t, k_aug, v_t, kmean)
    return y_att, k_rot.reshape(b, s, KV_HEADS, ATT_HEAD_DIM), v.reshape(b, s, KV_HEADS, ATT_HEAD_DIM)


def _attend_sample(qkvd3, cache_k, cache_v, page_table):
    ns, t, _ = qkvd3.shape
    npages = page_table.shape[1]
    past = npages * PAGE_SIZE
    pos = past + jnp.arange(t, dtype=jnp.int32)
    tables = tuple(jnp.tile(tb, (ns, 1)) for tb in _rope_tables(pos))
    qk = _rope_sample(qkvd3.reshape(ns * t, QKVD_WIDTH), tables)
    q_rot = qk[:, :ATT_WIDTH].reshape(ns, t, KV_HEADS, GQA_REP, ATT_HEAD_DIM)
    k_rot = qk[:, ATT_WIDTH:].reshape(ns, t, KV_WIDTH)
    v = qkvd3[:, :, ATT_WIDTH + KV_WIDTH:ATT_WIDTH + 2 * KV_WIDTH]
    q_ht = q_rot.transpose(0, 2, 3, 1, 4)
    eye = jnp.eye(KV_HEADS, dtype=F32)
    q_exp = (q_ht[:, :, :, :, None, :] * eye[None, :, None, None, :, None]).reshape(ns, ATT_HEADS * t, KV_WIDTH)
    blk_of_key = jnp.arange(past, dtype=jnp.int32) // MOBA_BLOCK
    e_mat = (jnp.arange(LANES, dtype=jnp.int32)[:, None] == blk_of_key[None, :]).astype(BF16)
    n_phys = cache_k.shape[0]
    to_pages = lambda c: jnp.transpose(c, (0, 2, 3, 1)).reshape(n_phys, KV_WIDTH, PAGE_SIZE)
    y_att = _moba_sample(page_table, q_exp.astype(BF16), k_rot, v, e_mat, to_pages(cache_k), to_pages(cache_v))
    return y_att, k_rot.reshape(ns, t, KV_HEADS, ATT_HEAD_DIM), v.reshape(ns, t, KV_HEADS, ATT_HEAD_DIM)


def _layer_params(l, norm1_w, w_in, ssd_conv_w, ssd_conv_b, dt_bias, a_log, d_skip, ssd_norm_w, w_ssd_out,
                  w_attn_out, w_mem_out, w_o, norm2_w, w_up, ffn_conv_w, ffn_conv_b, w_down, final_norm_w):
    w = w_in[l]
    o_z, o_x, o_dt = 0, D_INNER, D_INNER + CONV_DIM
    o_q = o_dt + SSD_HEADS
    o_k, o_v = o_q + ATT_WIDTH, o_q + ATT_WIDTH + KV_WIDTH
    o_qm = o_v + KV_WIDTH
    o_g = o_qm + MEM_WIDTH
    pad_lanes = lambda a: jnp.pad(a, (0, LANES - a.shape[0])).reshape(1, LANES)
    w_dt = jnp.pad(w[:, o_dt:o_q], ((0, 0), (0, LANES - SSD_HEADS)))
    return {
        "norm1_w": norm1_w[l],
        "w_zx": w[:, o_z:o_dt].astype(BF16),
        "w_qkvd": jnp.concatenate([w[:, o_q:o_qm], w_dt], axis=1).astype(BF16),
        "w_qm": w[:, o_qm:o_g].astype(BF16),
        "w_gates": w[:, o_g:].astype(BF16),
        "ssd_conv_w": ssd_conv_w[l],
        "ssd_conv_b": ssd_conv_b[l].reshape(1, CONV_DIM),
        "dt_bias": pad_lanes(dt_bias[l]),
        "a_log": pad_lanes(a_log[l]),
        "d_skip": jnp.repeat(d_skip[l], SSD_HEAD_DIM).reshape(1, D_INNER),
        "ssd_norm_w": ssd_norm_w[l].reshape(1, D_INNER),
        "w_ssd_out": w_ssd_out[l].astype(BF16),
        "w_attn_out": w_attn_out[l].astype(BF16),
        "w_mem_out": w_mem_out[l].astype(BF16),
        "w_o": w_o[l].astype(BF16),
        "norm2_w": norm2_w[l],
        "w_up": w_up[l].astype(BF16),
        "ffn_conv_w": ffn_conv_w[l],
        "ffn_conv_b": ffn_conv_b[l].reshape(1, 2 * D_FF),
        "w_down": w_down[l].astype(BF16),
        "final_norm_w": final_norm_w,
    }


def kernel(x_prompt, x_sample, cache_k, cache_v, cache_mem_k, cache_mem_v, state_ssm, state_conv,
           state_ffn_conv, page_table, mem_prompt, norm1_w, w_in, ssd_conv_w, ssd_conv_b, dt_bias, a_log,
           d_skip, ssd_norm_w, mem_norm_w, w_mem_kv, w_ssd_out, w_attn_out, w_mem_out, w_o, norm2_w, w_up,
           ffn_conv_w, ffn_conv_b, w_down, final_norm_w):
    depth = w_in.shape[0]
    assert depth == 1, "the final RMSNorm is fused into the single layer's last kernel"
    b_p, s_p, _ = x_prompt.shape
    n_mem = mem_prompt.shape[1]
    ns = x_sample.shape[0]
    l = 0
    p = _layer_params(l, norm1_w, w_in, ssd_conv_w, ssd_conv_b, dt_bias, a_log, d_skip, ssd_norm_w, w_ssd_out,
                      w_attn_out, w_mem_out, w_o, norm2_w, w_up, ffn_conv_w, ffn_conv_b, w_down, final_norm_w)

    memx = mem_prompt.reshape(b_p * n_mem, D_MODEL)
    wkv = w_mem_kv[l].astype(BF16)
    mk_p = _norm_matmul(memx, mem_norm_w[l], wkv[:, :MEM_WIDTH], tn=MEM_WIDTH, name="mem_k")
    mv_p = _norm_matmul(memx, mem_norm_w[l], wkv[:, MEM_WIDTH:], tn=MEM_WIDTH, name="mem_v")
    mk_p3 = mk_p.reshape(b_p, n_mem, MEM_WIDTH)
    mv_p3 = mv_p.reshape(b_p, n_mem, MEM_WIDTH)
    ssm0 = jnp.zeros((b_p, SSD_HEADS, SSD_HEAD_DIM, SSD_STATE), F32)
    conv0 = jnp.zeros((b_p, SSD_CONV - 1, CONV_DIM), F32)
    ffn0 = jnp.zeros((b_p, FFN_CONV - 1, 2 * D_FF), F32)
    y_p, k_p, v_p, s_p_new, c_p, f_p = _layer(x_prompt, mk_p3, mv_p3, ssm0, conv0, ffn0, _attend_prompt, p)

    attend_s = functools.partial(_attend_sample, cache_k=cache_k[l], cache_v=cache_v[l], page_table=page_table)
    y_s, k_s, v_s, s_s_new, c_s, f_s = _layer(x_sample, cache_mem_k[l], cache_mem_v[l], state_ssm[l], state_conv[l],
                                              state_ffn_conv[l], attend_s, p)

    mem_shape = (1, b_p, n_mem, MEM_HEADS, MEM_HEAD_DIM)
    return (y_p, y_s, k_p[None], v_p[None], mk_p.reshape(mem_shape), mv_p.reshape(mem_shape),
            s_p_new[None], c_p[None], f_p[None], k_s[None], v_s[None], s_s_new[None], c_s[None], f_s[None])
```

```python
import functools
import math

import jax
import jax.numpy as jnp
from jax import lax
from jax.experimental import pallas as pl
from jax.experimental.pallas import tpu as pltpu

F32 = jnp.float32
BF16 = jnp.bfloat16

D_MODEL = 1024
D_INNER = 2048
SSD_HEAD_DIM = 64
SSD_HEADS = 32
SSD_GROUPS = 8
SSD_STATE = 128
SSD_CONV = 4
SSD_CHUNK = 128
CONV_DIM = 4096
ATT_HEADS = 16
ATT_HEAD_DIM = 64
KV_HEADS = 4
GQA_REP = 4
ATT_WIDTH = 1024
KV_WIDTH = 256
MOBA_BLOCK = 256
MOBA_TOPK = 3
ROT_DIM = 16
ROPE_THETA = 500000.0
MEM_HEADS = 4
MEM_HEAD_DIM = 256
MEM_WIDTH = 1024
D_FF = 2816
FFN_CONV = 3
RMS_EPS = 1e-6
NEG_INF = -1e30
LOG2_E = 1.4426950408889634
PAGE_SIZE = 128

LANES = 128
SUBLANES = 8
VMEM_LIMIT = 56 * 1024 * 1024
V_ONES = 16
CONV_ROW_CHUNKS = 8

QKVD_WIDTH = ATT_WIDTH + 2 * KV_WIDTH + LANES
DT_COL_BLOCK = (ATT_WIDTH + 2 * KV_WIDTH) // LANES


def _cparams(sem):
    return pltpu.CompilerParams(dimension_semantics=sem, vmem_limit_bytes=VMEM_LIMIT)


def _dot(a, b):
    return jnp.dot(a, b, preferred_element_type=F32)


def _dot_nt(a, b):
    return lax.dot_general(a, b, (((1,), (1,)), ((), ())), preferred_element_type=F32)


def _dot_f32(a, b):
    return jnp.dot(a, b, preferred_element_type=F32, precision=lax.Precision.HIGHEST)


def _silu(x):
    return x * jax.nn.sigmoid(x)


def _norm_matmul_kernel(x_ref, nw_ref, w_ref, o_ref, xn_ref):
    @pl.when(pl.program_id(1) == 0)
    def _():
        x = x_ref[...]
        ms = jnp.mean(x * x, axis=-1, keepdims=True)
        xn_ref[...] = (x * lax.rsqrt(ms + RMS_EPS) * nw_ref[...]).astype(BF16)

    o_ref[...] = _dot(xn_ref[...], w_ref[...]).astype(o_ref.dtype)


def _norm_matmul(x, nw, w, *, tn, out_dtype=F32, name="norm_matmul"):
    t, d = x.shape
    n = w.shape[1]
    tm = min(t, 1024)
    assert t % tm == 0 and n % tn == 0
    return pl.pallas_call(
        _norm_matmul_kernel,
        grid=(t // tm, n // tn),
        in_specs=[pl.BlockSpec((tm, d), lambda i, j: (i, 0)),
                  pl.BlockSpec((1, d), lambda i, j: (0, 0)),
                  pl.BlockSpec((d, tn), lambda i, j: (0, j))],
        out_specs=pl.BlockSpec((tm, tn), lambda i, j: (i, j)),
        out_shape=jax.ShapeDtypeStruct((t, n), out_dtype),
        scratch_shapes=[pltpu.VMEM((tm, d), BF16)],
        compiler_params=_cparams(("parallel", "arbitrary")),
        name=name,
    )(x, nw.reshape(1, d), w)


def _conv4_silu(pre, halo, cw, cb):
    w0, w1, w2, w3 = (cw[k:k + 1, :] for k in range(SSD_CONV))
    prev = _shift_rows(pre, 1, halo)
    halo_pair = halo * w1 + pltpu.roll(halo, 1, axis=0) * w0
    acc = _shift_rows(pre * w1 + prev * w0, 2, halo_pair) + (prev * w2 + pre * w3)
    return _silu(acc + cb)


def _proj_zx_conv_kernel(x_ref, nw_ref, w_ref, cw_ref, cb_ref, st_ref, o_ref, xn_ref, carry_ref,
                         *, z_tiles, tiles_per_seq):
    i, j = pl.program_id(0), pl.program_id(1)
    tm = x_ref.shape[0]

    @pl.when(j == 0)
    def _():
        x = x_ref[...]
        ms = jnp.mean(x * x, axis=-1, keepdims=True)
        xn_ref[...] = (x * lax.rsqrt(ms + RMS_EPS) * nw_ref[...]).astype(BF16)

    @pl.when(j < z_tiles)
    def _():
        o_ref[...] = _dot(xn_ref[...], w_ref[...]).astype(o_ref.dtype)

    @pl.when(j >= z_tiles)
    def _():
        col = j - z_tiles

        @pl.when(i % tiles_per_seq == 0)
        def _():
            carry_ref[col] = st_ref[0]

        rc = tm // CONV_ROW_CHUNKS
        halo = carry_ref[col]
        for r in range(CONV_ROW_CHUNKS):
            y = _dot(xn_ref[r * rc:(r + 1) * rc, :], w_ref[...])
            o_ref[r * rc:(r + 1) * rc, :] = _conv4_silu(y, halo, cw_ref[...], cb_ref[...]).astype(o_ref.dtype)
            halo = y[rc - SUBLANES:rc, :]
        carry_ref[col] = halo


def _proj_zx_conv(x, nw, w, cw, cb, conv_state8, seq_len):
    t, d = x.shape
    n = w.shape[1]
    tm, tn = 1024, 1024
    assert t % tm == 0 and seq_len % tm == 0 and D_INNER % tn == 0 and CONV_DIM % tn == 0
    z_tiles = D_INNER // tn
    tiles_per_seq = seq_len // tm
    xcol = lambda i, j: jnp.maximum(j - z_tiles, 0)
    return pl.pallas_call(
        functools.partial(_proj_zx_conv_kernel, z_tiles=z_tiles, tiles_per_seq=tiles_per_seq),
        grid=(t // tm, n // tn),
        in_specs=[pl.BlockSpec((tm, d), lambda i, j: (i, 0)),
                  pl.BlockSpec((1, d), lambda i, j: (0, 0)),
                  pl.BlockSpec((d, tn), lambda i, j: (0, j)),
                  pl.BlockSpec((SSD_CONV, tn), lambda i, j: (0, xcol(i, j))),
                  pl.BlockSpec((1, tn), lambda i, j: (0, xcol(i, j))),
                  pl.BlockSpec((1, SUBLANES, tn), lambda i, j: (i // tiles_per_seq, 0, xcol(i, j)))],
        out_specs=pl.BlockSpec((tm, tn), lambda i, j: (i, j)),
        out_shape=jax.ShapeDtypeStruct((t, n), BF16),
        scratch_shapes=[pltpu.VMEM((tm, d), BF16),
                        pltpu.VMEM((CONV_DIM // tn, SUBLANES, tn), F32)],
        compiler_params=_cparams(("arbitrary", "arbitrary")),
        name="proj_zx_conv",
    )(x, nw.reshape(1, d), w, cw, cb, conv_state8)


def _shift_rows(x, k, halo):
    r = pltpu.roll(x, k, axis=0)
    row = lax.broadcasted_iota(jnp.int32, halo.shape, 0)
    top = jnp.where(row < k, pltpu.roll(halo, k, axis=0), r[:SUBLANES])
    if x.shape[0] == SUBLANES:
        return top
    return jnp.concatenate([top, r[SUBLANES:]], axis=0)


def _ssd_kernel(z_ref, xa_ref, xb_ref, dt_ref, cst_ref, s0_ref, cw_ref, cb_ref, dtb_ref, alog_ref,
                dsk_ref, nw_ref, y_ref, sf_ref, s_ref, carry_ref, ybuf_ref, *, qb, q, nchunks, conv_done):
    c = pl.program_id(1)

    @pl.when(c == 0)
    def _():
        s_ref[...] = s0_ref[0]
        carry_ref[...] = cst_ref[0]

    pre = jnp.concatenate([xa_ref[0], xb_ref[0]], axis=1).astype(F32)
    if conv_done:
        xbc = pre
    else:
        xbc = _conv4_silu(pre, carry_ref[...], cw_ref[...], cb_ref[...])
        if nchunks > 1:
            carry_ref[...] = pre[qb - SUBLANES:qb, :]

    z = z_ref[0].astype(F32)
    dt_raw = dt_ref[0]
    if qb < q:
        xbc = jnp.concatenate([xbc, jnp.zeros((q - qb, CONV_DIM), F32)], axis=0)
        dt_raw = jnp.concatenate([dt_raw, jnp.zeros((q - qb, LANES), F32)], axis=0)

    xs = xbc[:, :D_INNER]
    xs_bf = xs.astype(BF16)
    xs_t = xs.T

    v = dt_raw + dtb_ref[...]
    dt = jnp.maximum(v, 0.0) + jnp.log1p(jnp.exp(-jnp.abs(v)))
    row_q = lax.broadcasted_iota(jnp.int32, (q, LANES), 0)
    if qb < q:
        dt = jnp.where(row_q < qb, dt, 0.0)
    a = -jnp.exp(alog_ref[...])
    ri = lax.broadcasted_iota(jnp.int32, (q, q), 0)
    ci = lax.broadcasted_iota(jnp.int32, (q, q), 1)
    causal = ci <= ri
    cs = _dot_f32(causal.astype(F32), dt * a)
    cs_t = cs.T
    dt_t = dt.T
    ecs = jnp.exp(cs)

    for g in range(SSD_GROUPS):
        b_g = xbc[:, D_INNER + g * SSD_STATE:D_INNER + (g + 1) * SSD_STATE].astype(BF16)
        c_g = xbc[:qb, D_INNER + SSD_GROUPS * SSD_STATE + g * SSD_STATE:
                  D_INNER + SSD_GROUPS * SSD_STATE + (g + 1) * SSD_STATE].astype(BF16)
        cb = _dot_nt(c_g, b_g)
        for r in range(SSD_HEADS // SSD_GROUPS):
            h = g * (SSD_HEADS // SSD_GROUPS) + r
            lo = h * SSD_HEAD_DIM
            cs_col = cs[:qb, h:h + 1]
            cs_row = cs_t[h:h + 1, :]
            dt_row = dt_t[h:h + 1, :]
            seg = jnp.where(causal[:qb], cs_col - cs_row, -jnp.inf)
            m_h = (cb * jnp.exp(seg) * dt_row).astype(BF16)
            s_h = s_ref[h]
            y_h = _dot(m_h, xs_bf[:, lo:lo + SSD_HEAD_DIM])
            y_h = y_h + _dot_nt(c_g, s_h.astype(BF16)) * ecs[:qb, h:h + 1]
            ybuf_ref[:, lo:lo + SSD_HEAD_DIM] = y_h
            cs_end = cs_row[:, q - 1:q]
            w_row = dt_row * jnp.exp(cs_end - cs_row)
            xw = (xs_t[lo:lo + SSD_HEAD_DIM, :] * w_row).astype(BF16)
            s_ref[h] = s_h * jnp.exp(cs_end) + _dot(xw, b_g)

    y = ybuf_ref[...] + dsk_ref[...] * xs[:qb]
    gt = y * _silu(z)
    gw = D_INNER // SSD_GROUPS
    parts = []
    for g in range(SSD_GROUPS):
        gg = gt[:, g * gw:(g + 1) * gw]
        parts.append(gg * lax.rsqrt(jnp.mean(gg * gg, axis=-1, keepdims=True) + RMS_EPS))
    yn = jnp.concatenate(parts, axis=1) * nw_ref[...]
    y_ref[0] = yn.astype(y_ref.dtype)

    @pl.when(c == nchunks - 1)
    def _():
        sf_ref[0] = s_ref[...]


def _ssd(zx3, qkvd3, conv_state8, ssm0, cw, cb, dtb, alog, dsk, nw, conv_done):
    nb, l, _ = zx3.shape
    q = SSD_CHUNK
    qb = min(l, q)
    assert l % qb == 0 and qb % SUBLANES == 0
    nchunks = l // qb
    half = CONV_DIM // 2
    row_blk = lambda col: (lambda b, c: (b, c, col))
    full2 = lambda b, c: (0, 0)
    return pl.pallas_call(
        functools.partial(_ssd_kernel, qb=qb, q=q, nchunks=nchunks, conv_done=conv_done),
        grid=(nb, nchunks),
        in_specs=[pl.BlockSpec((1, qb, D_INNER), row_blk(0)),
                  pl.BlockSpec((1, qb, half), row_blk(1)),
                  pl.BlockSpec((1, qb, half), row_blk(2)),
                  pl.BlockSpec((1, qb, LANES), row_blk(DT_COL_BLOCK)),
                  pl.BlockSpec((1, SUBLANES, CONV_DIM), lambda b, c: (b, 0, 0)),
                  pl.BlockSpec((1, SSD_HEADS, SSD_HEAD_DIM, SSD_STATE), lambda b, c: (b, 0, 0, 0)),
                  pl.BlockSpec((SSD_CONV, CONV_DIM), full2),
                  pl.BlockSpec((1, CONV_DIM), full2),
                  pl.BlockSpec((1, LANES), full2),
                  pl.BlockSpec((1, LANES), full2),
                  pl.BlockSpec((1, D_INNER), full2),
                  pl.BlockSpec((1, D_INNER), full2)],
        out_specs=[pl.BlockSpec((1, qb, D_INNER), lambda b, c: (b, c, 0)),
                   pl.BlockSpec((1, SSD_HEADS, SSD_HEAD_DIM, SSD_STATE), lambda b, c: (b, 0, 0, 0))],
        out_shape=[jax.ShapeDtypeStruct((nb, l, D_INNER), BF16),
                   jax.ShapeDtypeStruct((nb, SSD_HEADS, SSD_HEAD_DIM, SSD_STATE), F32)],
        scratch_shapes=[pltpu.VMEM((SSD_HEADS, SSD_HEAD_DIM, SSD_STATE), F32),
                        pltpu.VMEM((SUBLANES, CONV_DIM), F32),
                        pltpu.VMEM((qb, D_INNER), F32)],
        compiler_params=_cparams(("parallel", "arbitrary")),
        name="ssd",
    )(zx3, zx3, zx3, qkvd3, conv_state8, ssm0, cw, cb, dtb, alog, dsk, nw)


def _rope_tables(pos):
    half = ROT_DIM // 2
    inv = ROPE_THETA ** (-jnp.arange(half, dtype=F32) * 2.0 / ROT_DIM)
    ang = pos.astype(F32)[:, None] * inv[None, :]
    cos, sin = jnp.cos(ang), jnp.sin(ang)
    n = pos.shape[0]
    pad = jnp.zeros((n, ATT_HEAD_DIM - ROT_DIM), F32)
    zero = jnp.zeros((n, half), F32)
    c_head = jnp.concatenate([cos, cos, pad + 1.0], axis=1)
    s1_head = jnp.concatenate([-sin, zero, pad], axis=1)
    s2_head = jnp.concatenate([zero, sin, pad], axis=1)
    rep = LANES // ATT_HEAD_DIM
    return jnp.tile(c_head, (1, rep)), jnp.tile(s1_head, (1, rep)), jnp.tile(s2_head, (1, rep))


def _rope_group(xg, c, s1, s2):
    half = ROT_DIM // 2
    return xg * c + pltpu.roll(xg, LANES - half, axis=1) * s1 + pltpu.roll(xg, half, axis=1) * s2


def _head_from_group(xg, odd, lane):
    if odd:
        xg = pltpu.roll(xg, ATT_HEAD_DIM, axis=1)
    return jnp.where(lane < ATT_HEAD_DIM, xg, 0.0)


def _prompt_prep_kernel(q_ref, k_ref, v_ref, c_ref, s1_ref, s2_ref,
                        krot_ref, vout_ref, qt_ref, kaug_ref, vt_ref, kmean_ref):
    i = pl.program_id(1)
    rows = q_ref.shape[1]
    c, s1, s2 = c_ref[...], s1_ref[...], s2_ref[...]
    lane = lax.broadcasted_iota(jnp.int32, (rows, LANES), 1)
    scale = ATT_HEAD_DIM ** -0.5 * LOG2_E
    hd = ATT_HEAD_DIM

    @pl.when(i == 0)
    def _():
        kmean_ref[...] = jnp.zeros_like(kmean_ref)

    for cg in range(ATT_WIDTH // LANES):
        qg = _rope_group(q_ref[0, :, cg * LANES:(cg + 1) * LANES], c, s1, s2) * scale
        qg_t = qg.T.astype(BF16)
        for odd in range(2):
            g, r = divmod(2 * cg + odd, GQA_REP)
            qt_ref[0, g, 0, :, r * rows:(r + 1) * rows] = qg_t[odd * hd:(odd + 1) * hd, :]

    onehot = jnp.where(lane == ATT_HEAD_DIM + i, 1.0, 0.0)
    for cg in range(KV_WIDTH // LANES):
        kg = _rope_group(k_ref[0, :, cg * LANES:(cg + 1) * LANES], c, s1, s2)
        krot_ref[0, :, cg * LANES:(cg + 1) * LANES] = kg
        vg = v_ref[0, :, cg * LANES:(cg + 1) * LANES]
        vout_ref[0, :, cg * LANES:(cg + 1) * LANES] = vg
        vg_t = vg.T.astype(BF16)
        for odd in range(2):
            g = 2 * cg + odd
            kh = _head_from_group(kg, odd, lane)
            kaug_ref[0, g] = (kh + onehot).astype(BF16)
            vt_ref[0, g, 0:hd, :] = vg_t[odd * hd:(odd + 1) * hd, :]
            vt_ref[0, g, hd:, :] = jnp.ones((V_ONES, rows), BF16)
            kmean_ref[0, g, pl.ds(i, 1), :] = jnp.mean(kh, axis=0, keepdims=True)


def _prompt_prep(qkvd3, tables):
    b, s, _ = qkvd3.shape
    nblk = s // MOBA_BLOCK
    assert s % MOBA_BLOCK == 0 and nblk <= SUBLANES
    blk = MOBA_BLOCK
    tab = pl.BlockSpec((blk, LANES), lambda bi, i: (i, 0))
    return pl.pallas_call(
        _prompt_prep_kernel,
        grid=(b, nblk),
        in_specs=[pl.BlockSpec((1, blk, ATT_WIDTH), lambda bi, i: (bi, i, 0)),
                  pl.BlockSpec((1, blk, KV_WIDTH), lambda bi, i: (bi, i, ATT_WIDTH // KV_WIDTH)),
                  pl.BlockSpec((1, blk, KV_WIDTH), lambda bi, i: (bi, i, ATT_WIDTH // KV_WIDTH + 1)),
                  tab, tab, tab],
        out_specs=[pl.BlockSpec((1, blk, KV_WIDTH), lambda bi, i: (bi, i, 0)),
                   pl.BlockSpec((1, blk, KV_WIDTH), lambda bi, i: (bi, i, 0)),
                   pl.BlockSpec((1, KV_HEADS, 1, ATT_HEAD_DIM, GQA_REP * blk), lambda bi, i: (bi, 0, i, 0, 0)),
                   pl.BlockSpec((1, KV_HEADS, blk, LANES), lambda bi, i: (bi, 0, i, 0)),
                   pl.BlockSpec((1, KV_HEADS, ATT_HEAD_DIM + V_ONES, blk), lambda bi, i: (bi, 0, 0, i)),
                   pl.BlockSpec((1, KV_HEADS, SUBLANES, LANES), lambda bi, i: (bi, 0, 0, 0))],
        out_shape=[jax.ShapeDtypeStruct((b, s, KV_WIDTH), F32),
                   jax.ShapeDtypeStruct((b, s, KV_WIDTH), F32),
                   jax.ShapeDtypeStruct((b, KV_HEADS, nblk, ATT_HEAD_DIM, GQA_REP * blk), BF16),
                   jax.ShapeDtypeStruct((b, KV_HEADS, s, LANES), BF16),
                   jax.ShapeDtypeStruct((b, KV_HEADS, ATT_HEAD_DIM + V_ONES, s), BF16),
                   jax.ShapeDtypeStruct((b, KV_HEADS, SUBLANES, LANES), F32)],
        compiler_params=_cparams(("parallel", "arbitrary")),
        name="prompt_prep",
    )(qkvd3, qkvd3, qkvd3, *tables)


def _topk_rows(gate, valid, row_f, rounds=MOBA_TOPK):
    g0 = jnp.where(valid, gate, -jnp.inf)
    sel = jnp.zeros(gate.shape, dtype=jnp.bool_)
    for _ in range(rounds):
        m = jnp.max(g0, axis=0, keepdims=True)
        idx = jnp.min(jnp.where(g0 == m, row_f, 1e9), axis=0, keepdims=True)
        pick = jnp.logical_and(row_f == idx, m > -jnp.inf)
        sel = jnp.logical_or(sel, pick)
        g0 = jnp.where(pick, -jnp.inf, g0)
    return sel


def _topk_lanes(gate, valid, lane_f, rounds=MOBA_TOPK):
    g0 = jnp.where(valid, gate, -jnp.inf)
    sel = jnp.zeros(gate.shape, dtype=jnp.bool_)
    for _ in range(rounds):
        m = jnp.max(g0, axis=1, keepdims=True)
        idx = jnp.min(jnp.where(g0 == m, lane_f, 1e9), axis=1, keepdims=True)
        pick = jnp.logical_and(lane_f == idx, m > -jnp.inf)
        sel = jnp.logical_or(sel, pick)
        g0 = jnp.where(pick, -jnp.inf, g0)
    return sel


def _moba_prompt_block(n_past, q_ref, k_ref, v_ref, km_ref, y_ref):
    blk = MOBA_BLOCK
    hd = ATT_HEAD_DIM
    nq = GQA_REP * blk
    qt = q_ref[0, 0, 0]
    q0 = jnp.concatenate([qt, jnp.zeros((LANES - hd, nq), BF16)], axis=0)
    k_own = k_ref[0, 0, n_past * blk:(n_past + 1) * blk, :]
    key_i = lax.broadcasted_iota(jnp.int32, (blk, nq), 0)
    q_i = lax.broadcasted_iota(jnp.int32, (blk, nq), 1) % blk
    s_own = jnp.where(key_i <= q_i, _dot(k_own, q0), NEG_INF)
    m = jnp.max(s_own, axis=0, keepdims=True)
    if n_past > 0:
        pad8 = jnp.zeros((SUBLANES, LANES), F32)
        gate = _dot(jnp.concatenate([km_ref[0, 0], pad8], axis=0).astype(BF16), q0)
        row = lax.broadcasted_iota(jnp.int32, gate.shape, 0)
        sel = _topk_rows(gate, row < n_past, row.astype(F32), rounds=min(MOBA_TOPK, n_past))
        selb = jnp.where(row < SUBLANES, jnp.where(sel, 0.0, NEG_INF), 0.0).astype(BF16)
        q_aug = jnp.concatenate([qt, selb, jnp.zeros((LANES - hd - 2 * SUBLANES, nq), BF16)], axis=0)
        s_past = _dot(k_ref[0, 0, 0:n_past * blk, :], q_aug)
        m = jnp.maximum(m, jnp.max(s_past, axis=0, keepdims=True))
    p_own = jnp.exp2(s_own - m)
    o = _dot(v_ref[0, 0, :, n_past * blk:(n_past + 1) * blk], p_own.astype(BF16))
    if n_past > 0:
        p_past = jnp.exp2(s_past - m)
        o = o + _dot(v_ref[0, 0, :, 0:n_past * blk], p_past.astype(BF16))
    o = o[0:hd] / o[hd:hd + 1]
    for pair in range(GQA_REP // 2):
        two = jnp.concatenate([o[:, 2 * pair * blk:(2 * pair + 1) * blk],
                               o[:, (2 * pair + 1) * blk:(2 * pair + 2) * blk]], axis=0)
        y_ref[:, pair * LANES:(pair + 1) * LANES] = two.T.astype(y_ref.dtype)


def _moba_prompt_kernel(q_ref, k_ref, v_ref, km_ref, y_ref, *, nblk):
    i = pl.program_id(2)
    for n_past in range(nblk):
        pl.when(i == n_past)(functools.partial(_moba_prompt_block, n_past, q_ref, k_ref, v_ref, km_ref, y_ref))


def _moba_prompt(q_t, k_aug, v_t, kmean):
    b, _, s, _ = k_aug.shape
    nblk = s // MOBA_BLOCK
    blk = MOBA_BLOCK
    return pl.pallas_call(
        functools.partial(_moba_prompt_kernel, nblk=nblk),
        grid=(b, KV_HEADS, nblk),
        in_specs=[pl.BlockSpec((1, 1, 1, ATT_HEAD_DIM, GQA_REP * blk), lambda bi, g, i: (bi, g, i, 0, 0)),
                  pl.BlockSpec((1, 1, s, LANES), lambda bi, g, i: (bi, g, 0, 0)),
                  pl.BlockSpec((1, 1, ATT_HEAD_DIM + V_ONES, s), lambda bi, g, i: (bi, g, 0, 0)),
                  pl.BlockSpec((1, 1, SUBLANES, LANES), lambda bi, g, i: (bi, g, 0, 0))],
        out_specs=pl.BlockSpec((blk, GQA_REP * ATT_HEAD_DIM), lambda bi, g, i: (bi * nblk + i, g)),
        out_shape=jax.ShapeDtypeStruct((b * s, ATT_WIDTH), BF16),
        compiler_params=_cparams(("parallel", "parallel", "arbitrary")),
        name="moba_prompt",
    )(q_t, k_aug, v_t, kmean)


def _rope_kernel(x_ref, c_ref, s1_ref, s2_ref, o_ref, *, q_groups):
    c, s1, s2 = c_ref[...], s1_ref[...], s2_ref[...]
    scale = ATT_HEAD_DIM ** -0.5
    for cg in range(x_ref.shape[1] // LANES):
        xg = _rope_group(x_ref[:, cg * LANES:(cg + 1) * LANES], c, s1, s2)
        if cg < q_groups:
            xg = xg * scale
        o_ref[:, cg * LANES:(cg + 1) * LANES] = xg


def _rope_sample(qkvd, tables):
    t = qkvd.shape[0]
    w = ATT_WIDTH + KV_WIDTH
    tab = pl.BlockSpec((t, LANES), lambda i: (0, 0))
    return pl.pallas_call(
        functools.partial(_rope_kernel, q_groups=ATT_WIDTH // LANES),
        grid=(1,),
        in_specs=[pl.BlockSpec((t, w), lambda i: (0, 0)), tab, tab, tab],
        out_specs=pl.BlockSpec((t, w), lambda i: (0, 0)),
        out_shape=jax.ShapeDtypeStruct((t, w), F32),
        compiler_params=_cparams(("arbitrary",)),
        name="rope_sample",
    )(qkvd, *tables)


def _moba_sample_kernel(pt_ref, qx_ref, kn_ref, vn_ref, e_ref, ck_ref, cv_ref, y_ref,
                        kbuf, vbuf, sem, *, npages, t):
    s = pl.program_id(0)
    ns = pl.num_programs(0)
    slot = s % 2

    def k_copy(seq, p, sl):
        return pltpu.make_async_copy(ck_ref.at[pt_ref[seq, p]], kbuf.at[sl, :, p * PAGE_SIZE:(p + 1) * PAGE_SIZE],
                                     sem.at[0, sl])

    def v_copy(seq, p, sl):
        return pltpu.make_async_copy(cv_ref.at[pt_ref[seq, p]], vbuf.at[sl, :, p * PAGE_SIZE:(p + 1) * PAGE_SIZE],
                                     sem.at[1, sl])

    def start_all(seq, sl):
        for p in range(npages):
            k_copy(seq, p, sl).start()
            v_copy(seq, p, sl).start()

    @pl.when(s == 0)
    def _():
        start_all(0, 0)

    @pl.when(s + 1 < ns)
    def _():
        start_all(s + 1, 1 - slot)

    for p in range(npages):
        k_copy(s, p, slot).wait()
        v_copy(s, p, slot).wait()

    past = npages * PAGE_SIZE
    nblk = past // MOBA_BLOCK
    rows = ATT_HEADS * t
    chunk = min(past, 2048)
    qx = qx_ref[0]

    s_raw = jnp.concatenate(
        [_dot(qx, kbuf[slot, :, c * chunk:(c + 1) * chunk].astype(BF16)) for c in range(past // chunk)], axis=1)
    lane = lax.broadcasted_iota(jnp.int32, (rows, LANES), 1)
    gate = jnp.zeros((rows, LANES), F32)
    for n in range(nblk):
        col = jnp.sum(s_raw[:, n * MOBA_BLOCK:(n + 1) * MOBA_BLOCK], axis=1, keepdims=True)
        gate = jnp.where(lane == n, col, gate)
    sel = _topk_lanes(gate, lane < nblk, lane.astype(F32))
    selb = jnp.where(sel, 0.0, NEG_INF).astype(BF16)
    s_past = s_raw + _dot(selb, e_ref[...])

    kn = jnp.concatenate([kn_ref[0], jnp.zeros((LANES - t, KV_WIDTH), F32)], axis=0).astype(BF16)
    vn = jnp.concatenate([vn_ref[0], jnp.zeros((LANES - t, KV_WIDTH), F32)], axis=0).astype(BF16)
    row = lax.broadcasted_iota(jnp.int32, (rows, LANES), 0)
    own_ok = lane <= (row % t)
    s_own = jnp.where(own_ok, _dot_nt(qx, kn), NEG_INF)

    m = jnp.maximum(jnp.max(s_past, axis=1, keepdims=True), jnp.max(s_own, axis=1, keepdims=True))
    p_past = jnp.exp(s_past - m)
    p_own = jnp.exp(s_own - m)
    den = jnp.sum(p_past, axis=1, keepdims=True) + jnp.sum(p_own, axis=1, keepdims=True)
    o = _dot(p_own.astype(BF16), vn)
    for c in range(past // chunk):
        o = o + _dot_nt(p_past[:, c * chunk:(c + 1) * chunk].astype(BF16),
                        vbuf[slot, :, c * chunk:(c + 1) * chunk].astype(BF16))
    o = o / den
    for h in range(ATT_HEADS):
        g = h // GQA_REP
        y_ref[0, :, h * ATT_HEAD_DIM:(h + 1) * ATT_HEAD_DIM] = (
            o[h * t:(h + 1) * t, g * ATT_HEAD_DIM:(g + 1) * ATT_HEAD_DIM].astype(y_ref.dtype))


def _moba_sample(page_table, q_exp, k_new, v_new, e_mat, cache_k, cache_v):
    ns, npages = page_table.shape
    t = k_new.shape[1]
    past = npages * PAGE_SIZE
    assert past % MOBA_BLOCK == 0 and past // MOBA_BLOCK <= LANES and t == SUBLANES
    rows = ATT_HEADS * t
    grid_spec = pltpu.PrefetchScalarGridSpec(
        num_scalar_prefetch=1,
        grid=(ns,),
        in_specs=[pl.BlockSpec((1, rows, KV_WIDTH), lambda s, pt: (s, 0, 0)),
                  pl.BlockSpec((1, t, KV_WIDTH), lambda s, pt: (s, 0, 0)),
                  pl.BlockSpec((1, t, KV_WIDTH), lambda s, pt: (s, 0, 0)),
                  pl.BlockSpec((LANES, past), lambda s, pt: (0, 0)),
                  pl.BlockSpec(memory_space=pl.ANY),
                  pl.BlockSpec(memory_space=pl.ANY)],
        out_specs=pl.BlockSpec((1, t, ATT_WIDTH), lambda s, pt: (s, 0, 0)),
        scratch_shapes=[pltpu.VMEM((2, KV_WIDTH, past), F32),
                        pltpu.VMEM((2, KV_WIDTH, past), F32),
                        pltpu.SemaphoreType.DMA((2, 2))],
    )
    return pl.pallas_call(
        functools.partial(_moba_sample_kernel, npages=npages, t=t),
        grid_spec=grid_spec,
        out_shape=jax.ShapeDtypeStruct((ns, t, ATT_WIDTH), BF16),
        compiler_params=_cparams(("arbitrary",)),
        name="moba_sample",
    )(page_table, q_exp, k_new, v_new, e_mat, cache_k, cache_v)


def _mem_head(q, mk, mv):
    q = (q.astype(F32) * MEM_HEAD_DIM ** -0.5).astype(BF16)
    s = _dot_nt(q, mk.astype(BF16))
    m = jnp.max(s, axis=1, keepdims=True)
    p = jnp.exp(s - m)
    den = jnp.sum(p, axis=1, keepdims=True)
    return _dot(p.astype(BF16), mv.astype(BF16)) / den


def _mem_attn_kernel(q_ref, mk_ref, mv_ref, y_ref):
    for h in range(MEM_HEADS):
        lo, hi = h * MEM_HEAD_DIM, (h + 1) * MEM_HEAD_DIM
        y_ref[0, :, lo:hi] = _mem_head(q_ref[0, :, lo:hi], mk_ref[0, :, lo:hi], mv_ref[0, :, lo:hi]).astype(y_ref.dtype)


def _mem_attn_paged_kernel(q_ref, mk_hbm, mv_hbm, y_ref, kbuf, vbuf, sem):
    s = pl.program_id(0)
    ns = pl.num_programs(0)
    slot = s % 2

    def copies(seq, sl):
        out = []
        for h in range(MEM_HEADS):
            out.append(pltpu.make_async_copy(mk_hbm.at[seq, :, h, :], kbuf.at[sl, h], sem.at[0, sl]))
            out.append(pltpu.make_async_copy(mv_hbm.at[seq, :, h, :], vbuf.at[sl, h], sem.at[1, sl]))
        return out

    @pl.when(s == 0)
    def _():
        for c in copies(0, 0):
            c.start()

    @pl.when(s + 1 < ns)
    def _():
        for c in copies(s + 1, 1 - slot):
            c.start()

    for c in copies(s, slot):
        c.wait()
    for h in range(MEM_HEADS):
        lo, hi = h * MEM_HEAD_DIM, (h + 1) * MEM_HEAD_DIM
        y_ref[0, :, lo:hi] = _mem_head(q_ref[0, :, lo:hi], kbuf[slot, h], vbuf[slot, h]).astype(y_ref.dtype)


def _mem_attn_paged(qm3, mk4, mv4):
    ns, l, _ = qm3.shape
    m = mk4.shape[1]
    return pl.pallas_call(
        _mem_attn_paged_kernel,
        grid=(ns,),
        in_specs=[pl.BlockSpec((1, l, MEM_WIDTH), lambda s: (s, 0, 0)),
                  pl.BlockSpec(memory_space=pl.ANY), pl.BlockSpec(memory_space=pl.ANY)],
        out_specs=pl.BlockSpec((1, l, MEM_WIDTH), lambda s: (s, 0, 0)),
        out_shape=jax.ShapeDtypeStruct((ns, l, MEM_WIDTH), BF16),
        scratch_shapes=[pltpu.VMEM((2, MEM_HEADS, m, MEM_HEAD_DIM), F32),
                        pltpu.VMEM((2, MEM_HEADS, m, MEM_HEAD_DIM), F32),
                        pltpu.SemaphoreType.DMA((2, 2))],
        compiler_params=_cparams(("arbitrary",)),
        name="mem_attn_paged",
    )(qm3, mk4, mv4)


def _mem_attn(qm3, mk, mv):
    nb, l, _ = qm3.shape
    if mk.ndim == 4:
        return _mem_attn_paged(qm3, mk, mv)
    m = mk.shape[1]
    tl = min(l, 512)
    assert l % tl == 0
    mem_spec = pl.BlockSpec((1, m, MEM_WIDTH), lambda b, i: (b, 0, 0))
    return pl.pallas_call(
        _mem_attn_kernel,
        grid=(nb, l // tl),
        in_specs=[pl.BlockSpec((1, tl, MEM_WIDTH), lambda b, i: (b, i, 0)), mem_spec, mem_spec],
        out_specs=pl.BlockSpec((1, tl, MEM_WIDTH), lambda b, i: (b, i, 0)),
        out_shape=jax.ShapeDtypeStruct((nb, l, MEM_WIDTH), BF16),
        compiler_params=_cparams(("parallel", "arbitrary")),
        name="mem_attn",
    )(qm3, mk, mv)


def _merge_kernel(x_ref, nw_ref, wg_ref, ys_ref, ya_ref, ym_ref, ws_ref, wa_ref, wm_ref, wo_ref, h_ref):
    d = D_MODEL
    x = x_ref[...]
    ms = jnp.mean(x * x, axis=-1, keepdims=True)
    xn = (x * lax.rsqrt(ms + RMS_EPS) * nw_ref[...]).astype(BF16)

    def gate(k):
        return jax.nn.sigmoid(_dot(xn, wg_ref[:, k * d:(k + 1) * d]))

    merged = gate(0) * _dot(ys_ref[...], ws_ref[...])
    merged = merged + gate(1) * _dot(ya_ref[...], wa_ref[...])
    merged = merged + gate(2) * _dot(ym_ref[...], wm_ref[...])
    h_ref[...] = x + _dot(merged.astype(BF16), wo_ref[...])


def _merge(x, nw, wg, y_ssd, y_att, y_mem, ws, wa, wm, wo):
    t = x.shape[0]
    tm = min(t, 512)
    assert t % tm == 0
    nw = nw.reshape(1, D_MODEL)
    rows = lambda w: pl.BlockSpec((tm, w), lambda i: (i, 0))
    const = lambda a: pl.BlockSpec(a.shape, lambda i: (0, 0), pipeline_mode=pl.Buffered(1))
    return pl.pallas_call(
        _merge_kernel,
        grid=(t // tm,),
        in_specs=[rows(D_MODEL), const(nw), const(wg), rows(D_INNER), rows(ATT_WIDTH), rows(MEM_WIDTH),
                  const(ws), const(wa), const(wm), const(wo)],
        out_specs=rows(D_MODEL),
        out_shape=jax.ShapeDtypeStruct((t, D_MODEL), F32),
        compiler_params=_cparams(("parallel",)),
        name="merge",
    )(x, nw, wg, y_ssd, y_att, y_mem, ws, wa, wm, wo)


def _ffn_gate_kernel(u_ref, st_ref, w_ref, b_ref, a_ref, carry_ref, *, tl, ntiles):
    i = pl.program_id(1)

    @pl.when(i == 0)
    def _():
        carry_ref[...] = st_ref[0]

    u = u_ref[0]
    halo = carry_ref[...]
    acc = _shift_rows(u, 2, halo) * w_ref[0:1, :]
    acc = acc + _shift_rows(u, 1, halo) * w_ref[1:2, :]
    acc = acc + u * w_ref[2:3, :]
    uc = acc + b_ref[...]
    if ntiles > 1:
        carry_ref[...] = u[tl - SUBLANES:tl, :]
    a_ref[0] = (_silu(uc[:, :D_FF]) * uc[:, D_FF:]).astype(a_ref.dtype)


def _ffn_gate(u3, state8, w, b):
    nb, l, c = u3.shape
    tl = min(l, 256)
    assert l % tl == 0 and tl % SUBLANES == 0
    ntiles = l // tl
    return pl.pallas_call(
        functools.partial(_ffn_gate_kernel, tl=tl, ntiles=ntiles),
        grid=(nb, ntiles),
        in_specs=[pl.BlockSpec((1, tl, c), lambda bi, i: (bi, i, 0)),
                  pl.BlockSpec((1, SUBLANES, c), lambda bi, i: (bi, 0, 0)),
                  pl.BlockSpec((FFN_CONV, c), lambda bi, i: (0, 0)),
                  pl.BlockSpec((1, c), lambda bi, i: (0, 0))],
        out_specs=pl.BlockSpec((1, tl, D_FF), lambda bi, i: (bi, i, 0)),
        out_shape=jax.ShapeDtypeStruct((nb, l, D_FF), BF16),
        scratch_shapes=[pltpu.VMEM((SUBLANES, c), F32)],
        compiler_params=_cparams(("parallel", "arbitrary")),
        name="ffn_gate",
    )(u3, state8, w, b)


def _ffn_fused_kernel(h_ref, st_ref, n2_ref, wu_ref, cw_ref, cb_ref, wd_ref, fw_ref, y_ref, last_ref, carry_ref,
                      *, tl, ntiles, ck):
    i = pl.program_id(1)

    @pl.when(i == 0)
    def _():
        carry_ref[...] = st_ref[0]

    h = h_ref[0]
    ms = jnp.mean(h * h, axis=-1, keepdims=True)
    hn = (h * lax.rsqrt(ms + RMS_EPS) * n2_ref[...]).astype(BF16)

    def conv(u, off):
        halo = carry_ref[:, off:off + ck]
        acc = _shift_rows(u, 2, halo) * cw_ref[0:1, off:off + ck]
        acc = acc + _shift_rows(u, 1, halo) * cw_ref[1:2, off:off + ck]
        acc = acc + u * cw_ref[2:3, off:off + ck]
        carry_ref[:, off:off + ck] = u[tl - SUBLANES:tl, :]
        return acc + cb_ref[:, off:off + ck]

    out = h
    for c in range(D_FF // ck):
        lo = c * ck
        ug = conv(_dot(hn, wu_ref[:, lo:lo + ck]), lo)
        uv = conv(_dot(hn, wu_ref[:, D_FF + lo:D_FF + lo + ck]), D_FF + lo)
        out = out + _dot((_silu(ug) * uv).astype(BF16), wd_ref[lo:lo + ck, :])
    ms = jnp.mean(out * out, axis=-1, keepdims=True)
    y_ref[0] = out * lax.rsqrt(ms + RMS_EPS) * fw_ref[...]

    @pl.when(i == ntiles - 1)
    def _():
        last_ref[0] = carry_ref[...]


def _ffn_fused(h3, state8, n2, wu, cw, cb, wd, fw):
    nb, l, d = h3.shape
    tl = 512
    ck = D_FF // 2
    assert l % tl == 0 and ck % LANES == 0
    ntiles = l // tl
    const = lambda a: pl.BlockSpec(a.shape, lambda bi, i: (0,) * a.ndim, pipeline_mode=pl.Buffered(1))
    n2, fw = n2.reshape(1, d), fw.reshape(1, d)
    return pl.pallas_call(
        functools.partial(_ffn_fused_kernel, tl=tl, ntiles=ntiles, ck=ck),
        grid=(nb, ntiles),
        in_specs=[pl.BlockSpec((1, tl, d), lambda bi, i: (bi, i, 0)),
                  pl.BlockSpec((1, SUBLANES, 2 * D_FF), lambda bi, i: (bi, 0, 0)),
                  const(n2), const(wu), const(cw), const(cb), const(wd), const(fw)],
        out_specs=[pl.BlockSpec((1, tl, d), lambda bi, i: (bi, i, 0)),
                   pl.BlockSpec((1, SUBLANES, 2 * D_FF), lambda bi, i: (bi, 0, 0))],
        out_shape=[jax.ShapeDtypeStruct((nb, l, d), F32),
                   jax.ShapeDtypeStruct((nb, SUBLANES, 2 * D_FF), F32)],
        scratch_shapes=[pltpu.VMEM((SUBLANES, 2 * D_FF), F32)],
        compiler_params=_cparams(("parallel", "arbitrary")),
        name="ffn_fused",
    )(h3, state8, n2, wu, cw, cb, wd, fw)


def _down_kernel(a_ref, w_ref, h_ref, nw_ref, y_ref):
    h = h_ref[...] + _dot(a_ref[...], w_ref[...])
    ms = jnp.mean(h * h, axis=-1, keepdims=True)
    y_ref[...] = h * lax.rsqrt(ms + RMS_EPS) * nw_ref[...]


def _down(act, w, h, nw):
    t = h.shape[0]
    tm = min(t, 512)
    assert t % tm == 0
    return pl.pallas_call(
        _down_kernel,
        grid=(t // tm,),
        in_specs=[pl.BlockSpec((tm, D_FF), lambda i: (i, 0)),
                  pl.BlockSpec((D_FF, D_MODEL), lambda i: (0, 0)),
                  pl.BlockSpec((tm, D_MODEL), lambda i: (i, 0)),
                  pl.BlockSpec((1, D_MODEL), lambda i: (0, 0))],
        out_specs=pl.BlockSpec((tm, D_MODEL), lambda i: (i, 0)),
        out_shape=jax.ShapeDtypeStruct((t, D_MODEL), F32),
        compiler_params=_cparams(("parallel",)),
        name="ffn_down",
    )(act, w, h, nw.reshape(1, D_MODEL))


def _pad_state_rows(state):
    nb, k, c = state.shape
    return jnp.concatenate([jnp.zeros((nb, SUBLANES - k, c), state.dtype), state], axis=1)


def _layer(x3, mem_k3, mem_v3, ssm0, conv_state, ffn_state, attend, p):
    nb, l, d = x3.shape
    t = nb * l
    x = x3.reshape(t, d)
    nw1 = p["norm1_w"]
    conv_state8 = _pad_state_rows(conv_state)
    conv_in_proj = l % 1024 == 0
    if conv_in_proj:
        zx = _proj_zx_conv(x, nw1, p["w_zx"], p["ssd_conv_w"], p["ssd_conv_b"], conv_state8, l)
        x_tail = x3[:, l - SUBLANES:, :].reshape(nb * SUBLANES, d)
        tail = _norm_matmul(x_tail, nw1, p["w_zx"][:, D_INNER:], tn=CONV_DIM // 2, name="proj_conv_tail")
        conv_new = tail.reshape(nb, SUBLANES, CONV_DIM)[:, SUBLANES - (SSD_CONV - 1):]
    else:
        zx = _norm_matmul(x, nw1, p["w_zx"], tn=2048, out_dtype=BF16, name="proj_zx")
        conv_new = zx.reshape(nb, l, D_INNER + CONV_DIM)[:, l - (SSD_CONV - 1):, D_INNER:].astype(F32)
    qkvd = _norm_matmul(x, nw1, p["w_qkvd"], tn=QKVD_WIDTH, name="proj_qkvd")
    qm = _norm_matmul(x, nw1, p["w_qm"], tn=MEM_WIDTH, out_dtype=BF16, name="proj_qm")

    zx3 = zx.reshape(nb, l, D_INNER + CONV_DIM)
    qkvd3 = qkvd.reshape(nb, l, QKVD_WIDTH)
    y_ssd, ssm_new = _ssd(zx3, qkvd3, conv_state8, ssm0, p["ssd_conv_w"], p["ssd_conv_b"],
                          p["dt_bias"], p["a_log"], p["d_skip"], p["ssd_norm_w"], conv_in_proj)

    y_att, k_new, v_new = attend(qkvd3)
    y_mem = _mem_attn(qm.reshape(nb, l, MEM_WIDTH), mem_k3, mem_v3)

    h = _merge(x, nw1, p["w_gates"], y_ssd.reshape(t, D_INNER), y_att.reshape(t, ATT_WIDTH), y_mem.reshape(t, MEM_WIDTH),
               p["w_ssd_out"], p["w_attn_out"], p["w_mem_out"], p["w_o"])

    if l % 512 == 0:
        y3, last8 = _ffn_fused(h.reshape(nb, l, d), _pad_state_rows(ffn_state), p["norm2_w"], p["w_up"],
                               p["ffn_conv_w"], p["ffn_conv_b"], p["w_down"], p["final_norm_w"])
        ffn_new = last8[:, SUBLANES - (FFN_CONV - 1):]
        return y3, k_new, v_new, ssm_new, conv_new, ffn_new

    u = _norm_matmul(h, p["norm2_w"], p["w_up"], tn=2 * D_FF // 11, name="ffn_up")
    u3 = u.reshape(nb, l, 2 * D_FF)
    ffn_new = u3[:, l - (FFN_CONV - 1):]
    act = _ffn_gate(u3, _pad_state_rows(ffn_state), p["ffn_conv_w"], p["ffn_conv_b"])
    y = _down(act.reshape(t, D_FF), p["w_down"], h, p["final_norm_w"])
    return y.reshape(nb, l, d), k_new, v_new, ssm_new, conv_new, ffn_new


def _attend_prompt(qkvd3):
    b, s, _ = qkvd3.shape
    tables = _rope_tables(jnp.arange(s, dtype=jnp.int32))
    k_rot, v, q_t, k_aug, v_t, kmean = _prompt_prep(qkvd3, tables)
    y_att = _moba_prompt(q_t, k_aug, v_t, kmean)
    return y_att, k_rot.reshape(b, s, KV_HEADS, ATT_HEAD_DIM), v.reshape(b, s, KV_HEADS, ATT_HEAD_DIM)


def _attend_sample(qkvd3, cache_k, cache_v, page_table):
    ns, t, _ = qkvd3.shape
    npages = page_table.shape[1]
    past = npages * PAGE_SIZE
    pos = past + jnp.arange(t, dtype=jnp.int32)
    tables = tuple(jnp.tile(tb, (ns, 1)) for tb in _rope_tables(pos))
    qk = _rope_sample(qkvd3.reshape(ns * t, QKVD_WIDTH), tables)
    q_rot = qk[:, :ATT_WIDTH].reshape(ns, t, KV_HEADS, GQA_REP, ATT_HEAD_DIM)
    k_rot = qk[:, ATT_WIDTH:].reshape(ns, t, KV_WIDTH)
    v = qkvd3[:, :, ATT_WIDTH + KV_WIDTH:ATT_WIDTH + 2 * KV_WIDTH]
    q_ht = q_rot.transpose(0, 2, 3, 1, 4)
    eye = jnp.eye(KV_HEADS, dtype=F32)
    q_exp = (q_ht[:, :, :, :, None, :] * eye[None, :, None, None, :, None]).reshape(ns, ATT_HEADS * t, KV_WIDTH)
    blk_of_key = jnp.arange(past, dtype=jnp.int32) // MOBA_BLOCK
    e_mat = (jnp.arange(LANES, dtype=jnp.int32)[:, None] == blk_of_key[None, :]).astype(BF16)
    n_phys = cache_k.shape[0]
    to_pages = lambda c: jnp.transpose(c, (0, 2, 3, 1)).reshape(n_phys, KV_WIDTH, PAGE_SIZE)
    y_att = _moba_sample(page_table, q_exp.astype(BF16), k_rot, v, e_mat, to_pages(cache_k), to_pages(cache_v))
    return y_att, k_rot.reshape(ns, t, KV_HEADS, ATT_HEAD_DIM), v.reshape(ns, t, KV_HEADS, ATT_HEAD_DIM)


def _layer_params(l, norm1_w, w_in, ssd_conv_w, ssd_conv_b, dt_bias, a_log, d_skip, ssd_norm_w, w_ssd_out,
                  w_attn_out, w_mem_out, w_o, norm2_w, w_up, ffn_conv_w, ffn_conv_b, w_down, final_norm_w):
    w = w_in[l]
    o_z, o_x, o_dt = 0, D_INNER, D_INNER + CONV_DIM
    o_q = o_dt + SSD_HEADS
    o_k, o_v = o_q + ATT_WIDTH, o_q + ATT_WIDTH + KV_WIDTH
    o_qm = o_v + KV_WIDTH
    o_g = o_qm + MEM_WIDTH
    pad_lanes = lambda a: jnp.pad(a, (0, LANES - a.shape[0])).reshape(1, LANES)
    w_dt = jnp.pad(w[:, o_dt:o_q], ((0, 0), (0, LANES - SSD_HEADS)))
    return {
        "norm1_w": norm1_w[l],
        "w_zx": w[:, o_z:o_dt].astype(BF16),
        "w_qkvd": jnp.concatenate([w[:, o_q:o_qm], w_dt], axis=1).astype(BF16),
        "w_qm": w[:, o_qm:o_g].astype(BF16),
        "w_gates": w[:, o_g:].astype(BF16),
        "ssd_conv_w": ssd_conv_w[l],
        "ssd_conv_b": ssd_conv_b[l].reshape(1, CONV_DIM),
        "dt_bias": pad_lanes(dt_bias[l]),
        "a_log": pad_lanes(a_log[l]),
        "d_skip": jnp.repeat(d_skip[l], SSD_HEAD_DIM).reshape(1, D_INNER),
        "ssd_norm_w": ssd_norm_w[l].reshape(1, D_INNER),
        "w_ssd_out": w_ssd_out[l].astype(BF16),
        "w_attn_out": w_attn_out[l].astype(BF16),
        "w_mem_out": w_mem_out[l].astype(BF16),
        "w_o": w_o[l].astype(BF16),
        "norm2_w": norm2_w[l],
        "w_up": w_up[l].astype(BF16),
        "ffn_conv_w": ffn_conv_w[l],
        "ffn_conv_b": ffn_conv_b[l].reshape(1, 2 * D_FF),
        "w_down": w_down[l].astype(BF16),
        "final_norm_w": final_norm_w,
    }


def kernel(x_prompt, x_sample, cache_k, cache_v, cache_mem_k, cache_mem_v, state_ssm, state_conv,
           state_ffn_conv, page_table, mem_prompt, norm1_w, w_in, ssd_conv_w, ssd_conv_b, dt_bias, a_log,
           d_skip, ssd_norm_w, mem_norm_w, w_mem_kv, w_ssd_out, w_attn_out, w_mem_out, w_o, norm2_w, w_up,
           ffn_conv_w, ffn_conv_b, w_down, final_norm_w):
    depth = w_in.shape[0]
    assert depth == 1, "the final RMSNorm is fused into the single layer's last kernel"
    b_p, s_p, _ = x_prompt.shape
    n_mem = mem_prompt.shape[1]
    ns = x_sample.shape[0]
    l = 0
    p = _layer_params(l, norm1_w, w_in, ssd_conv_w, ssd_conv_b, dt_bias, a_log, d_skip, ssd_norm_w, w_ssd_out,
                      w_attn_out, w_mem_out, w_o, norm2_w, w_up, ffn_conv_w, ffn_conv_b, w_down, final_norm_w)

    memx = mem_prompt.reshape(b_p * n_mem, D_MODEL)
    wkv = w_mem_kv[l].astype(BF16)
    mk_p = _norm_matmul(memx, mem_norm_w[l], wkv[:, :MEM_WIDTH], tn=MEM_WIDTH, name="mem_k")
    mv_p = _norm_matmul(memx, mem_norm_w[l], wkv[:, MEM_WIDTH:], tn=MEM_WIDTH, name="mem_v")
    mk_p3 = mk_p.reshape(b_p, n_mem, MEM_WIDTH)
    mv_p3 = mv_p.reshape(b_p, n_mem, MEM_WIDTH)
    ssm0 = jnp.zeros((b_p, SSD_HEADS, SSD_HEAD_DIM, SSD_STATE), F32)
    conv0 = jnp.zeros((b_p, SSD_CONV - 1, CONV_DIM), F32)
    ffn0 = jnp.zeros((b_p, FFN_CONV - 1, 2 * D_FF), F32)
    y_p, k_p, v_p, s_p_new, c_p, f_p = _layer(x_prompt, mk_p3, mv_p3, ssm0, conv0, ffn0, _attend_prompt, p)

    attend_s = functools.partial(_attend_sample, cache_k=cache_k[l], cache_v=cache_v[l], page_table=page_table)
    y_s, k_s, v_s, s_s_new, c_s, f_s = _layer(x_sample, cache_mem_k[l], cache_mem_v[l], state_ssm[l], state_conv[l],
                                              state_ffn_conv[l], attend_s, p)

    mem_shape = (1, b_p, n_mem, MEM_HEADS, MEM_HEAD_DIM)
    return (y_p, y_s, k_p[None], v_p[None], mk_p.reshape(mem_shape), mv_p.reshape(mem_shape),
            s_p_new[None], c_p[None], f_p[None], k_s[None], v_s[None], s_s_new[None], c_s[None], f_s[None])
```

```python
import functools
import math

import jax
import jax.numpy as jnp
from jax import lax
from jax.experimental import pallas as pl
from jax.experimental.pallas import tpu as pltpu

F32 = jnp.float32
BF16 = jnp.bfloat16

D_MODEL = 1024
D_INNER = 2048
SSD_HEAD_DIM = 64
SSD_HEADS = 32
SSD_GROUPS = 8
SSD_STATE = 128
SSD_CONV = 4
SSD_CHUNK = 128
CONV_DIM = 4096
ATT_HEADS = 16
ATT_HEAD_DIM = 64
KV_HEADS = 4
GQA_REP = 4
ATT_WIDTH = 1024
KV_WIDTH = 256
MOBA_BLOCK = 256
MOBA_TOPK = 3
ROT_DIM = 16
ROPE_THETA = 500000.0
MEM_HEADS = 4
MEM_HEAD_DIM = 256
MEM_WIDTH = 1024
D_FF = 2816
FFN_CONV = 3
RMS_EPS = 1e-6
NEG_INF = -1e30
LOG2_E = 1.4426950408889634
PAGE_SIZE = 128

LANES = 128
SUBLANES = 8
VMEM_LIMIT = 56 * 1024 * 1024
V_ONES = 16

QKVD_WIDTH = ATT_WIDTH + 2 * KV_WIDTH + LANES
DT_COL_BLOCK = (ATT_WIDTH + 2 * KV_WIDTH) // LANES


def _cparams(sem):
    return pltpu.CompilerParams(dimension_semantics=sem, vmem_limit_bytes=VMEM_LIMIT)


def _dot(a, b):
    return jnp.dot(a, b, preferred_element_type=F32)


def _dot_nt(a, b):
    return lax.dot_general(a, b, (((1,), (1,)), ((), ())), preferred_element_type=F32)


def _dot_f32(a, b):
    return jnp.dot(a, b, preferred_element_type=F32, precision=lax.Precision.HIGHEST)


def _silu(x):
    return x * jax.nn.sigmoid(x)


def _norm_matmul_kernel(x_ref, nw_ref, w_ref, o_ref, xn_ref, *, act):
    @pl.when(pl.program_id(1) == 0)
    def _():
        x = x_ref[...]
        ms = jnp.mean(x * x, axis=-1, keepdims=True)
        xn_ref[...] = (x * lax.rsqrt(ms + RMS_EPS) * nw_ref[...]).astype(BF16)

    y = _dot(xn_ref[...], w_ref[...])
    if act == "sigmoid":
        y = jax.nn.sigmoid(y)
    o_ref[...] = y.astype(o_ref.dtype)


def _norm_matmul(x, nw, w, *, tn, out_dtype=F32, act=None, name="norm_matmul"):
    t, d = x.shape
    n = w.shape[1]
    tm = min(t, 1024)
    assert t % tm == 0 and n % tn == 0
    return pl.pallas_call(
        functools.partial(_norm_matmul_kernel, act=act),
        grid=(t // tm, n // tn),
        in_specs=[pl.BlockSpec((tm, d), lambda i, j: (i, 0)),
                  pl.BlockSpec((1, d), lambda i, j: (0, 0)),
                  pl.BlockSpec((d, tn), lambda i, j: (0, j))],
        out_specs=pl.BlockSpec((tm, tn), lambda i, j: (i, j)),
        out_shape=jax.ShapeDtypeStruct((t, n), out_dtype),
        scratch_shapes=[pltpu.VMEM((tm, d), BF16)],
        compiler_params=_cparams(("parallel", "arbitrary")),
        name=name,
    )(x, nw.reshape(1, d), w)


def _shift_rows(x, k, halo):
    r = pltpu.roll(x, k, axis=0)
    row = lax.broadcasted_iota(jnp.int32, halo.shape, 0)
    top = jnp.where(row < k, pltpu.roll(halo, k, axis=0), r[:SUBLANES])
    if x.shape[0] == SUBLANES:
        return top
    return jnp.concatenate([top, r[SUBLANES:]], axis=0)


def _ssd_kernel(z_ref, xa_ref, xb_ref, dt_ref, cst_ref, s0_ref, cw_ref, cb_ref, dtb_ref, alog_ref,
                dsk_ref, nw_ref, y_ref, sf_ref, s_ref, carry_ref, ybuf_ref, *, qb, q, nchunks):
    c = pl.program_id(1)

    @pl.when(c == 0)
    def _():
        s_ref[...] = s0_ref[0]
        carry_ref[...] = cst_ref[0]

    pre = jnp.concatenate([xa_ref[0], xb_ref[0]], axis=1).astype(F32)
    halo = carry_ref[...]
    w0, w1, w2, w3 = (cw_ref[k:k + 1, :] for k in range(SSD_CONV))
    prev = _shift_rows(pre, 1, halo)
    halo_pair = halo * w1 + pltpu.roll(halo, 1, axis=0) * w0
    acc = _shift_rows(pre * w1 + prev * w0, 2, halo_pair) + (prev * w2 + pre * w3)
    xbc = _silu(acc + cb_ref[...])
    if nchunks > 1:
        carry_ref[...] = pre[qb - SUBLANES:qb, :]

    z = z_ref[0].astype(F32)
    dt_raw = dt_ref[0]
    if qb < q:
        xbc = jnp.concatenate([xbc, jnp.zeros((q - qb, CONV_DIM), F32)], axis=0)
        dt_raw = jnp.concatenate([dt_raw, jnp.zeros((q - qb, LANES), F32)], axis=0)

    xs = xbc[:, :D_INNER]
    xs_bf = xs.astype(BF16)
    xs_t = xs.T

    v = dt_raw + dtb_ref[...]
    dt = jnp.maximum(v, 0.0) + jnp.log1p(jnp.exp(-jnp.abs(v)))
    row_q = lax.broadcasted_iota(jnp.int32, (q, LANES), 0)
    if qb < q:
        dt = jnp.where(row_q < qb, dt, 0.0)
    a = -jnp.exp(alog_ref[...])
    ri = lax.broadcasted_iota(jnp.int32, (q, q), 0)
    ci = lax.broadcasted_iota(jnp.int32, (q, q), 1)
    causal = ci <= ri
    cs = _dot_f32(causal.astype(F32), dt * a)
    cs_t = cs.T
    dt_t = dt.T
    ecs = jnp.exp(cs)

    for g in range(SSD_GROUPS):
        b_g = xbc[:, D_INNER + g * SSD_STATE:D_INNER + (g + 1) * SSD_STATE].astype(BF16)
        c_g = xbc[:qb, D_INNER + SSD_GROUPS * SSD_STATE + g * SSD_STATE:
                  D_INNER + SSD_GROUPS * SSD_STATE + (g + 1) * SSD_STATE].astype(BF16)
        cb = _dot_nt(c_g, b_g)
        for r in range(SSD_HEADS // SSD_GROUPS):
            h = g * (SSD_HEADS // SSD_GROUPS) + r
            lo = h * SSD_HEAD_DIM
            cs_col = cs[:qb, h:h + 1]
            cs_row = cs_t[h:h + 1, :]
            dt_row = dt_t[h:h + 1, :]
            seg = jnp.where(causal[:qb], cs_col - cs_row, -jnp.inf)
            m_h = (cb * jnp.exp(seg) * dt_row).astype(BF16)
            s_h = s_ref[h]
            y_h = _dot(m_h, xs_bf[:, lo:lo + SSD_HEAD_DIM])
            y_h = y_h + _dot_nt(c_g, s_h.astype(BF16)) * ecs[:qb, h:h + 1]
            ybuf_ref[:, lo:lo + SSD_HEAD_DIM] = y_h
            cs_end = cs_row[:, q - 1:q]
            w_row = dt_row * jnp.exp(cs_end - cs_row)
            xw = (xs_t[lo:lo + SSD_HEAD_DIM, :] * w_row).astype(BF16)
            s_ref[h] = s_h * jnp.exp(cs_end) + _dot(xw, b_g)

    y = ybuf_ref[...] + dsk_ref[...] * xs[:qb]
    gt = y * _silu(z)
    gw = D_INNER // SSD_GROUPS
    parts = []
    for g in range(SSD_GROUPS):
        gg = gt[:, g * gw:(g + 1) * gw]
        parts.append(gg * lax.rsqrt(jnp.mean(gg * gg, axis=-1, keepdims=True) + RMS_EPS))
    yn = jnp.concatenate(parts, axis=1) * nw_ref[...]
    y_ref[0] = yn.astype(y_ref.dtype)

    @pl.when(c == nchunks - 1)
    def _():
        sf_ref[0] = s_ref[...]


def _ssd(zx3, qkvd3, conv_state8, ssm0, cw, cb, dtb, alog, dsk, nw):
    nb, l, _ = zx3.shape
    q = SSD_CHUNK
    qb = min(l, q)
    assert l % qb == 0 and qb % SUBLANES == 0
    nchunks = l // qb
    half = CONV_DIM // 2
    row_blk = lambda col: (lambda b, c: (b, c, col))
    full2 = lambda b, c: (0, 0)
    return pl.pallas_call(
        functools.partial(_ssd_kernel, qb=qb, q=q, nchunks=nchunks),
        grid=(nb, nchunks),
        in_specs=[pl.BlockSpec((1, qb, D_INNER), row_blk(0)),
                  pl.BlockSpec((1, qb, half), row_blk(1)),
                  pl.BlockSpec((1, qb, half), row_blk(2)),
                  pl.BlockSpec((1, qb, LANES), row_blk(DT_COL_BLOCK)),
                  pl.BlockSpec((1, SUBLANES, CONV_DIM), lambda b, c: (b, 0, 0)),
                  pl.BlockSpec((1, SSD_HEADS, SSD_HEAD_DIM, SSD_STATE), lambda b, c: (b, 0, 0, 0)),
                  pl.BlockSpec((SSD_CONV, CONV_DIM), full2),
                  pl.BlockSpec((1, CONV_DIM), full2),
                  pl.BlockSpec((1, LANES), full2),
                  pl.BlockSpec((1, LANES), full2),
                  pl.BlockSpec((1, D_INNER), full2),
                  pl.BlockSpec((1, D_INNER), full2)],
        out_specs=[pl.BlockSpec((1, qb, D_INNER), lambda b, c: (b, c, 0)),
                   pl.BlockSpec((1, SSD_HEADS, SSD_HEAD_DIM, SSD_STATE), lambda b, c: (b, 0, 0, 0))],
        out_shape=[jax.ShapeDtypeStruct((nb, l, D_INNER), BF16),
                   jax.ShapeDtypeStruct((nb, SSD_HEADS, SSD_HEAD_DIM, SSD_STATE), F32)],
        scratch_shapes=[pltpu.VMEM((SSD_HEADS, SSD_HEAD_DIM, SSD_STATE), F32),
                        pltpu.VMEM((SUBLANES, CONV_DIM), F32),
                        pltpu.VMEM((qb, D_INNER), F32)],
        compiler_params=_cparams(("parallel", "arbitrary")),
        name="ssd",
    )(zx3, zx3, zx3, qkvd3, conv_state8, ssm0, cw, cb, dtb, alog, dsk, nw)


def _rope_tables(pos):
    half = ROT_DIM // 2
    inv = ROPE_THETA ** (-jnp.arange(half, dtype=F32) * 2.0 / ROT_DIM)
    ang = pos.astype(F32)[:, None] * inv[None, :]
    cos, sin = jnp.cos(ang), jnp.sin(ang)
    n = pos.shape[0]
    pad = jnp.zeros((n, ATT_HEAD_DIM - ROT_DIM), F32)
    zero = jnp.zeros((n, half), F32)
    c_head = jnp.concatenate([cos, cos, pad + 1.0], axis=1)
    s1_head = jnp.concatenate([-sin, zero, pad], axis=1)
    s2_head = jnp.concatenate([zero, sin, pad], axis=1)
    rep = LANES // ATT_HEAD_DIM
    return jnp.tile(c_head, (1, rep)), jnp.tile(s1_head, (1, rep)), jnp.tile(s2_head, (1, rep))


def _rope_group(xg, c, s1, s2):
    half = ROT_DIM // 2
    return xg * c + pltpu.roll(xg, LANES - half, axis=1) * s1 + pltpu.roll(xg, half, axis=1) * s2


def _head_from_group(xg, odd, lane):
    if odd:
        xg = pltpu.roll(xg, ATT_HEAD_DIM, axis=1)
    return jnp.where(lane < ATT_HEAD_DIM, xg, 0.0)


def _prompt_prep_kernel(q_ref, k_ref, v_ref, c_ref, s1_ref, s2_ref,
                        krot_ref, vout_ref, qt_ref, kaug_ref, vt_ref, kmean_ref):
    i = pl.program_id(1)
    rows = q_ref.shape[1]
    c, s1, s2 = c_ref[...], s1_ref[...], s2_ref[...]
    lane = lax.broadcasted_iota(jnp.int32, (rows, LANES), 1)
    scale = ATT_HEAD_DIM ** -0.5 * LOG2_E
    hd = ATT_HEAD_DIM

    @pl.when(i == 0)
    def _():
        kmean_ref[...] = jnp.zeros_like(kmean_ref)

    for cg in range(ATT_WIDTH // LANES):
        qg = _rope_group(q_ref[0, :, cg * LANES:(cg + 1) * LANES], c, s1, s2) * scale
        qg_t = qg.T.astype(BF16)
        for odd in range(2):
            g, r = divmod(2 * cg + odd, GQA_REP)
            qt_ref[0, g, 0, :, r * rows:(r + 1) * rows] = qg_t[odd * hd:(odd + 1) * hd, :]

    onehot = jnp.where(lane == ATT_HEAD_DIM + i, 1.0, 0.0)
    for cg in range(KV_WIDTH // LANES):
        kg = _rope_group(k_ref[0, :, cg * LANES:(cg + 1) * LANES], c, s1, s2)
        krot_ref[0, :, cg * LANES:(cg + 1) * LANES] = kg
        vg = v_ref[0, :, cg * LANES:(cg + 1) * LANES]
        vout_ref[0, :, cg * LANES:(cg + 1) * LANES] = vg
        vg_t = vg.T.astype(BF16)
        for odd in range(2):
            g = 2 * cg + odd
            kh = _head_from_group(kg, odd, lane)
            kaug_ref[0, g] = (kh + onehot).astype(BF16)
            vt_ref[0, g, 0:hd, :] = vg_t[odd * hd:(odd + 1) * hd, :]
            vt_ref[0, g, hd:, :] = jnp.ones((V_ONES, rows), BF16)
            kmean_ref[0, g, pl.ds(i, 1), :] = jnp.mean(kh, axis=0, keepdims=True)


def _prompt_prep(qkvd3, tables):
    b, s, _ = qkvd3.shape
    nblk = s // MOBA_BLOCK
    assert s % MOBA_BLOCK == 0 and nblk <= SUBLANES
    blk = MOBA_BLOCK
    tab = pl.BlockSpec((blk, LANES), lambda bi, i: (i, 0))
    return pl.pallas_call(
        _prompt_prep_kernel,
        grid=(b, nblk),
        in_specs=[pl.BlockSpec((1, blk, ATT_WIDTH), lambda bi, i: (bi, i, 0)),
                  pl.BlockSpec((1, blk, KV_WIDTH), lambda bi, i: (bi, i, ATT_WIDTH // KV_WIDTH)),
                  pl.BlockSpec((1, blk, KV_WIDTH), lambda bi, i: (bi, i, ATT_WIDTH // KV_WIDTH + 1)),
                  tab, tab, tab],
        out_specs=[pl.BlockSpec((1, blk, KV_WIDTH), lambda bi, i: (bi, i, 0)),
                   pl.BlockSpec((1, blk, KV_WIDTH), lambda bi, i: (bi, i, 0)),
                   pl.BlockSpec((1, KV_HEADS, 1, ATT_HEAD_DIM, GQA_REP * blk), lambda bi, i: (bi, 0, i, 0, 0)),
                   pl.BlockSpec((1, KV_HEADS, blk, LANES), lambda bi, i: (bi, 0, i, 0)),
                   pl.BlockSpec((1, KV_HEADS, ATT_HEAD_DIM + V_ONES, blk), lambda bi, i: (bi, 0, 0, i)),
                   pl.BlockSpec((1, KV_HEADS, SUBLANES, LANES), lambda bi, i: (bi, 0, 0, 0))],
        out_shape=[jax.ShapeDtypeStruct((b, s, KV_WIDTH), F32),
                   jax.ShapeDtypeStruct((b, s, KV_WIDTH), F32),
                   jax.ShapeDtypeStruct((b, KV_HEADS, nblk, ATT_HEAD_DIM, GQA_REP * blk), BF16),
                   jax.ShapeDtypeStruct((b, KV_HEADS, s, LANES), BF16),
                   jax.ShapeDtypeStruct((b, KV_HEADS, ATT_HEAD_DIM + V_ONES, s), BF16),
                   jax.ShapeDtypeStruct((b, KV_HEADS, SUBLANES, LANES), F32)],
        compiler_params=_cparams(("parallel", "arbitrary")),
        name="prompt_prep",
    )(qkvd3, qkvd3, qkvd3, *tables)


def _topk_rows(gate, valid, row_f, rounds=MOBA_TOPK):
    g0 = jnp.where(valid, gate, -jnp.inf)
    sel = jnp.zeros(gate.shape, dtype=jnp.bool_)
    for _ in range(rounds):
        m = jnp.max(g0, axis=0, keepdims=True)
        idx = jnp.min(jnp.where(g0 == m, row_f, 1e9), axis=0, keepdims=True)
        pick = jnp.logical_and(row_f == idx, m > -jnp.inf)
        sel = jnp.logical_or(sel, pick)
        g0 = jnp.where(pick, -jnp.inf, g0)
    return sel


def _topk_lanes(gate, valid, lane_f, rounds=MOBA_TOPK):
    g0 = jnp.where(valid, gate, -jnp.inf)
    sel = jnp.zeros(gate.shape, dtype=jnp.bool_)
    for _ in range(rounds):
        m = jnp.max(g0, axis=1, keepdims=True)
        idx = jnp.min(jnp.where(g0 == m, lane_f, 1e9), axis=1, keepdims=True)
        pick = jnp.logical_and(lane_f == idx, m > -jnp.inf)
        sel = jnp.logical_or(sel, pick)
        g0 = jnp.where(pick, -jnp.inf, g0)
    return sel


def _moba_prompt_block(n_past, q_ref, k_ref, v_ref, km_ref, y_ref):
    blk = MOBA_BLOCK
    hd = ATT_HEAD_DIM
    nq = GQA_REP * blk
    qt = q_ref[0, 0, 0]
    q0 = jnp.concatenate([qt, jnp.zeros((LANES - hd, nq), BF16)], axis=0)
    k_own = k_ref[0, 0, n_past * blk:(n_past + 1) * blk, :]
    key_i = lax.broadcasted_iota(jnp.int32, (blk, nq), 0)
    q_i = lax.broadcasted_iota(jnp.int32, (blk, nq), 1) % blk
    s_own = jnp.where(key_i <= q_i, _dot(k_own, q0), NEG_INF)
    m = jnp.max(s_own, axis=0, keepdims=True)
    if n_past > 0:
        pad8 = jnp.zeros((SUBLANES, LANES), F32)
        gate = _dot(jnp.concatenate([km_ref[0, 0], pad8], axis=0).astype(BF16), q0)
        row = lax.broadcasted_iota(jnp.int32, gate.shape, 0)
        sel = _topk_rows(gate, row < n_past, row.astype(F32), rounds=min(MOBA_TOPK, n_past))
        selb = jnp.where(row < SUBLANES, jnp.where(sel, 0.0, NEG_INF), 0.0).astype(BF16)
        q_aug = jnp.concatenate([qt, selb, jnp.zeros((LANES - hd - 2 * SUBLANES, nq), BF16)], axis=0)
        s_past = _dot(k_ref[0, 0, 0:n_past * blk, :], q_aug)
        m = jnp.maximum(m, jnp.max(s_past, axis=0, keepdims=True))
    p_own = jnp.exp2(s_own - m)
    o = _dot(v_ref[0, 0, :, n_past * blk:(n_past + 1) * blk], p_own.astype(BF16))
    if n_past > 0:
        p_past = jnp.exp2(s_past - m)
        o = o + _dot(v_ref[0, 0, :, 0:n_past * blk], p_past.astype(BF16))
    o = o[0:hd] / o[hd:hd + 1]
    for pair in range(GQA_REP // 2):
        two = jnp.concatenate([o[:, 2 * pair * blk:(2 * pair + 1) * blk],
                               o[:, (2 * pair + 1) * blk:(2 * pair + 2) * blk]], axis=0)
        y_ref[:, pair * LANES:(pair + 1) * LANES] = two.T.astype(y_ref.dtype)


def _moba_prompt_kernel(q_ref, k_ref, v_ref, km_ref, y_ref, *, nblk):
    i = pl.program_id(2)
    for n_past in range(nblk):
        pl.when(i == n_past)(functools.partial(_moba_prompt_block, n_past, q_ref, k_ref, v_ref, km_ref, y_ref))


def _moba_prompt(q_t, k_aug, v_t, kmean):
    b, _, s, _ = k_aug.shape
    nblk = s // MOBA_BLOCK
    blk = MOBA_BLOCK
    return pl.pallas_call(
        functools.partial(_moba_prompt_kernel, nblk=nblk),
        grid=(b, KV_HEADS, nblk),
        in_specs=[pl.BlockSpec((1, 1, 1, ATT_HEAD_DIM, GQA_REP * blk), lambda bi, g, i: (bi, g, i, 0, 0)),
                  pl.BlockSpec((1, 1, s, LANES), lambda bi, g, i: (bi, g, 0, 0)),
                  pl.BlockSpec((1, 1, ATT_HEAD_DIM + V_ONES, s), lambda bi, g, i: (bi, g, 0, 0)),
                  pl.BlockSpec((1, 1, SUBLANES, LANES), lambda bi, g, i: (bi, g, 0, 0))],
        out_specs=pl.BlockSpec((blk, GQA_REP * ATT_HEAD_DIM), lambda bi, g, i: (bi * nblk + i, g)),
        out_shape=jax.ShapeDtypeStruct((b * s, ATT_WIDTH), BF16),
        compiler_params=_cparams(("parallel", "parallel", "arbitrary")),
        name="moba_prompt",
    )(q_t, k_aug, v_t, kmean)


def _rope_kernel(x_ref, c_ref, s1_ref, s2_ref, o_ref, *, q_groups):
    c, s1, s2 = c_ref[...], s1_ref[...], s2_ref[...]
    scale = ATT_HEAD_DIM ** -0.5
    for cg in range(x_ref.shape[1] // LANES):
        xg = _rope_group(x_ref[:, cg * LANES:(cg + 1) * LANES], c, s1, s2)
        if cg < q_groups:
            xg = xg * scale
        o_ref[:, cg * LANES:(cg + 1) * LANES] = xg


def _rope_sample(qkvd, tables):
    t = qkvd.shape[0]
    w = ATT_WIDTH + KV_WIDTH
    tab = pl.BlockSpec((t, LANES), lambda i: (0, 0))
    return pl.pallas_call(
        functools.partial(_rope_kernel, q_groups=ATT_WIDTH // LANES),
        grid=(1,),
        in_specs=[pl.BlockSpec((t, w), lambda i: (0, 0)), tab, tab, tab],
        out_specs=pl.BlockSpec((t, w), lambda i: (0, 0)),
        out_shape=jax.ShapeDtypeStruct((t, w), F32),
        compiler_params=_cparams(("arbitrary",)),
        name="rope_sample",
    )(qkvd, *tables)


def _moba_sample_kernel(pt_ref, qx_ref, kn_ref, vn_ref, e_ref, ck_ref, cv_ref, y_ref,
                        kbuf, vbuf, sem, *, npages, t):
    s = pl.program_id(0)
    ns = pl.num_programs(0)
    slot = s % 2

    def k_copy(seq, p, sl):
        return pltpu.make_async_copy(ck_ref.at[pt_ref[seq, p]], kbuf.at[sl, :, p * PAGE_SIZE:(p + 1) * PAGE_SIZE],
                                     sem.at[0, sl])

    def v_copy(seq, p, sl):
        return pltpu.make_async_copy(cv_ref.at[pt_ref[seq, p]], vbuf.at[sl, :, p * PAGE_SIZE:(p + 1) * PAGE_SIZE],
                                     sem.at[1, sl])

    def start_all(seq, sl):
        for p in range(npages):
            k_copy(seq, p, sl).start(priority=0)
            v_copy(seq, p, sl).start(priority=1)

    @pl.when(s == 0)
    def _():
        start_all(0, 0)

    @pl.when(s + 1 < ns)
    def _():
        start_all(s + 1, 1 - slot)

    for p in range(npages):
        k_copy(s, p, slot).wait()
        v_copy(s, p, slot).wait()

    past = npages * PAGE_SIZE
    nblk = past // MOBA_BLOCK
    rows = ATT_HEADS * t
    chunk = min(past, 2048)
    qx = qx_ref[0]

    s_raw = jnp.concatenate(
        [_dot(qx, kbuf[slot, :, c * chunk:(c + 1) * chunk].astype(BF16)) for c in range(past // chunk)], axis=1)
    lane = lax.broadcasted_iota(jnp.int32, (rows, LANES), 1)
    gate = jnp.zeros((rows, LANES), F32)
    for n in range(nblk):
        col = jnp.sum(s_raw[:, n * MOBA_BLOCK:(n + 1) * MOBA_BLOCK], axis=1, keepdims=True)
        gate = jnp.where(lane == n, col, gate)
    sel = _topk_lanes(gate, lane < nblk, lane.astype(F32))
    selb = jnp.where(sel, 0.0, NEG_INF).astype(BF16)
    s_past = s_raw + _dot(selb, e_ref[...])

    kn = jnp.concatenate([kn_ref[0], jnp.zeros((LANES - t, KV_WIDTH), F32)], axis=0).astype(BF16)
    vn = jnp.concatenate([vn_ref[0], jnp.zeros((LANES - t, KV_WIDTH), F32)], axis=0).astype(BF16)
    row = lax.broadcasted_iota(jnp.int32, (rows, LANES), 0)
    own_ok = lane <= (row % t)
    s_own = jnp.where(own_ok, _dot_nt(qx, kn), NEG_INF)

    m = jnp.maximum(jnp.max(s_past, axis=1, keepdims=True), jnp.max(s_own, axis=1, keepdims=True))
    p_past = jnp.exp(s_past - m)
    p_own = jnp.exp(s_own - m)
    den = jnp.sum(p_past, axis=1, keepdims=True) + jnp.sum(p_own, axis=1, keepdims=True)
    o = _dot(p_own.astype(BF16), vn)
    for c in range(past // chunk):
        o = o + _dot_nt(p_past[:, c * chunk:(c + 1) * chunk].astype(BF16),
                        vbuf[slot, :, c * chunk:(c + 1) * chunk].astype(BF16))
    o = o / den
    for h in range(ATT_HEADS):
        g = h // GQA_REP
        y_ref[0, :, h * ATT_HEAD_DIM:(h + 1) * ATT_HEAD_DIM] = (
            o[h * t:(h + 1) * t, g * ATT_HEAD_DIM:(g + 1) * ATT_HEAD_DIM].astype(y_ref.dtype))


def _moba_sample(page_table, q_exp, k_new, v_new, e_mat, cache_k, cache_v):
    ns, npages = page_table.shape
    t = k_new.shape[1]
    past = npages * PAGE_SIZE
    assert past % MOBA_BLOCK == 0 and past // MOBA_BLOCK <= LANES and t == SUBLANES
    rows = ATT_HEADS * t
    grid_spec = pltpu.PrefetchScalarGridSpec(
        num_scalar_prefetch=1,
        grid=(ns,),
        in_specs=[pl.BlockSpec((1, rows, KV_WIDTH), lambda s, pt: (s, 0, 0)),
                  pl.BlockSpec((1, t, KV_WIDTH), lambda s, pt: (s, 0, 0)),
                  pl.BlockSpec((1, t, KV_WIDTH), lambda s, pt: (s, 0, 0)),
                  pl.BlockSpec((LANES, past), lambda s, pt: (0, 0)),
                  pl.BlockSpec(memory_space=pl.ANY),
                  pl.BlockSpec(memory_space=pl.ANY)],
        out_specs=pl.BlockSpec((1, t, ATT_WIDTH), lambda s, pt: (s, 0, 0)),
        scratch_shapes=[pltpu.VMEM((2, KV_WIDTH, past), F32),
                        pltpu.VMEM((2, KV_WIDTH, past), F32),
                        pltpu.SemaphoreType.DMA((2, 2))],
    )
    return pl.pallas_call(
        functools.partial(_moba_sample_kernel, npages=npages, t=t),
        grid_spec=grid_spec,
        out_shape=jax.ShapeDtypeStruct((ns, t, ATT_WIDTH), BF16),
        compiler_params=_cparams(("arbitrary",)),
        name="moba_sample",
    )(page_table, q_exp, k_new, v_new, e_mat, cache_k, cache_v)


def _mem_head(q, mk, mv):
    q = (q.astype(F32) * MEM_HEAD_DIM ** -0.5).astype(BF16)
    s = _dot_nt(q, mk.astype(BF16))
    m = jnp.max(s, axis=1, keepdims=True)
    p = jnp.exp(s - m)
    den = jnp.sum(p, axis=1, keepdims=True)
    return _dot(p.astype(BF16), mv.astype(BF16)) / den


def _mem_attn_kernel(q_ref, mk_ref, mv_ref, y_ref):
    for h in range(MEM_HEADS):
        lo, hi = h * MEM_HEAD_DIM, (h + 1) * MEM_HEAD_DIM
        y_ref[0, :, lo:hi] = _mem_head(q_ref[0, :, lo:hi], mk_ref[0, :, lo:hi], mv_ref[0, :, lo:hi]).astype(y_ref.dtype)


def _mem_attn_paged_kernel(q_ref, mk_hbm, mv_hbm, y_ref, kbuf, vbuf, sem):
    s = pl.program_id(0)
    ns = pl.num_programs(0)
    slot = s % 2

    def copies(seq, sl):
        out = []
        for h in range(MEM_HEADS):
            out.append(pltpu.make_async_copy(mk_hbm.at[seq, :, h, :], kbuf.at[sl, h], sem.at[0, sl]))
            out.append(pltpu.make_async_copy(mv_hbm.at[seq, :, h, :], vbuf.at[sl, h], sem.at[1, sl]))
        return out

    @pl.when(s == 0)
    def _():
        for n, c in enumerate(copies(0, 0)):
            c.start(priority=n % 2)

    @pl.when(s + 1 < ns)
    def _():
        for n, c in enumerate(copies(s + 1, 1 - slot)):
            c.start(priority=n % 2)

    for c in copies(s, slot):
        c.wait()
    for h in range(MEM_HEADS):
        lo, hi = h * MEM_HEAD_DIM, (h + 1) * MEM_HEAD_DIM
        y_ref[0, :, lo:hi] = _mem_head(q_ref[0, :, lo:hi], kbuf[slot, h], vbuf[slot, h]).astype(y_ref.dtype)


def _mem_attn_paged(qm3, mk4, mv4):
    ns, l, _ = qm3.shape
    m = mk4.shape[1]
    return pl.pallas_call(
        _mem_attn_paged_kernel,
        grid=(ns,),
        in_specs=[pl.BlockSpec((1, l, MEM_WIDTH), lambda s: (s, 0, 0)),
                  pl.BlockSpec(memory_space=pl.ANY), pl.BlockSpec(memory_space=pl.ANY)],
        out_specs=pl.BlockSpec((1, l, MEM_WIDTH), lambda s: (s, 0, 0)),
        out_shape=jax.ShapeDtypeStruct((ns, l, MEM_WIDTH), BF16),
        scratch_shapes=[pltpu.VMEM((2, MEM_HEADS, m, MEM_HEAD_DIM), F32),
                        pltpu.VMEM((2, MEM_HEADS, m, MEM_HEAD_DIM), F32),
                        pltpu.SemaphoreType.DMA((2, 2))],
        compiler_params=_cparams(("arbitrary",)),
        name="mem_attn_paged",
    )(qm3, mk4, mv4)


def _mem_attn(qm3, mk, mv):
    nb, l, _ = qm3.shape
    if mk.ndim == 4:
        return _mem_attn_paged(qm3, mk, mv)
    m = mk.shape[1]
    tl = min(l, 512)
    assert l % tl == 0
    mem_spec = pl.BlockSpec((1, m, MEM_WIDTH), lambda b, i: (b, 0, 0))
    return pl.pallas_call(
        _mem_attn_kernel,
        grid=(nb, l // tl),
        in_specs=[pl.BlockSpec((1, tl, MEM_WIDTH), lambda b, i: (b, i, 0)), mem_spec, mem_spec],
        out_specs=pl.BlockSpec((1, tl, MEM_WIDTH), lambda b, i: (b, i, 0)),
        out_shape=jax.ShapeDtypeStruct((nb, l, MEM_WIDTH), BF16),
        compiler_params=_cparams(("parallel", "arbitrary")),
        name="mem_attn",
    )(qm3, mk, mv)


def _merge_kernel(x_ref, nw_ref, wg_ref, ys_ref, ya_ref, ym_ref, ws_ref, wa_ref, wm_ref, wo_ref, h_ref):
    d = D_MODEL
    x = x_ref[...]
    ms = jnp.mean(x * x, axis=-1, keepdims=True)
    xn = (x * lax.rsqrt(ms + RMS_EPS) * nw_ref[...]).astype(BF16)

    def gate(k):
        return jax.nn.sigmoid(_dot(xn, wg_ref[:, k * d:(k + 1) * d]))

    merged = gate(0) * _dot(ys_ref[...], ws_ref[...])
    merged = merged + gate(1) * _dot(ya_ref[...], wa_ref[...])
    merged = merged + gate(2) * _dot(ym_ref[...], wm_ref[...])
    h_ref[...] = x + _dot(merged.astype(BF16), wo_ref[...])


def _merge(x, nw, wg, y_ssd, y_att, y_mem, ws, wa, wm, wo):
    t = x.shape[0]
    tm = min(t, 512)
    assert t % tm == 0
    nw = nw.reshape(1, D_MODEL)
    rows = lambda w: pl.BlockSpec((tm, w), lambda i: (i, 0))
    const = lambda a: pl.BlockSpec(a.shape, lambda i: (0, 0), pipeline_mode=pl.Buffered(1))
    return pl.pallas_call(
        _merge_kernel,
        grid=(t // tm,),
        in_specs=[rows(D_MODEL), const(nw), const(wg), rows(D_INNER), rows(ATT_WIDTH), rows(MEM_WIDTH),
                  const(ws), const(wa), const(wm), const(wo)],
        out_specs=rows(D_MODEL),
        out_shape=jax.ShapeDtypeStruct((t, D_MODEL), F32),
        compiler_params=_cparams(("parallel",)),
        name="merge",
    )(x, nw, wg, y_ssd, y_att, y_mem, ws, wa, wm, wo)


def _ffn_gate_kernel(u_ref, st_ref, w_ref, b_ref, a_ref, carry_ref, *, tl, ntiles):
    i = pl.program_id(1)

    @pl.when(i == 0)
    def _():
        carry_ref[...] = st_ref[0]

    u = u_ref[0]
    halo = carry_ref[...]
    acc = _shift_rows(u, 2, halo) * w_ref[0:1, :]
    acc = acc + _shift_rows(u, 1, halo) * w_ref[1:2, :]
    acc = acc + u * w_ref[2:3, :]
    uc = acc + b_ref[...]
    if ntiles > 1:
        carry_ref[...] = u[tl - SUBLANES:tl, :]
    a_ref[0] = (_silu(uc[:, :D_FF]) * uc[:, D_FF:]).astype(a_ref.dtype)


def _ffn_gate(u3, state8, w, b):
    nb, l, c = u3.shape
    tl = min(l, 256)
    assert l % tl == 0 and tl % SUBLANES == 0
    ntiles = l // tl
    return pl.pallas_call(
        functools.partial(_ffn_gate_kernel, tl=tl, ntiles=ntiles),
        grid=(nb, ntiles),
        in_specs=[pl.BlockSpec((1, tl, c), lambda bi, i: (bi, i, 0)),
                  pl.BlockSpec((1, SUBLANES, c), lambda bi, i: (bi, 0, 0)),
                  pl.BlockSpec((FFN_CONV, c), lambda bi, i: (0, 0)),
                  pl.BlockSpec((1, c), lambda bi, i: (0, 0))],
        out_specs=pl.BlockSpec((1, tl, D_FF), lambda bi, i: (bi, i, 0)),
        out_shape=jax.ShapeDtypeStruct((nb, l, D_FF), BF16),
        scratch_shapes=[pltpu.VMEM((SUBLANES, c), F32)],
        compiler_params=_cparams(("parallel", "arbitrary")),
        name="ffn_gate",
    )(u3, state8, w, b)


def _ffn_fused_kernel(h_ref, st_ref, n2_ref, wu_ref, cw_ref, cb_ref, wd_ref, fw_ref, y_ref, last_ref, carry_ref,
                      *, tl, ntiles, ck):
    i = pl.program_id(1)

    @pl.when(i == 0)
    def _():
        carry_ref[...] = st_ref[0]

    h = h_ref[0]
    ms = jnp.mean(h * h, axis=-1, keepdims=True)
    hn = (h * lax.rsqrt(ms + RMS_EPS) * n2_ref[...]).astype(BF16)

    def conv(u, off):
        halo = carry_ref[:, off:off + ck]
        acc = _shift_rows(u, 2, halo) * cw_ref[0:1, off:off + ck]
        acc = acc + _shift_rows(u, 1, halo) * cw_ref[1:2, off:off + ck]
        acc = acc + u * cw_ref[2:3, off:off + ck]
        carry_ref[:, off:off + ck] = u[tl - SUBLANES:tl, :]
        return acc + cb_ref[:, off:off + ck]

    out = h
    for c in range(D_FF // ck):
        lo = c * ck
        ug = conv(_dot(hn, wu_ref[:, lo:lo + ck]), lo)
        uv = conv(_dot(hn, wu_ref[:, D_FF + lo:D_FF + lo + ck]), D_FF + lo)
        out = out + _dot((_silu(ug) * uv).astype(BF16), wd_ref[lo:lo + ck, :])
    ms = jnp.mean(out * out, axis=-1, keepdims=True)
    y_ref[0] = out * lax.rsqrt(ms + RMS_EPS) * fw_ref[...]

    @pl.when(i == ntiles - 1)
    def _():
        last_ref[0] = carry_ref[...]


def _ffn_fused(h3, state8, n2, wu, cw, cb, wd, fw):
    nb, l, d = h3.shape
    tl = 512
    ck = D_FF // 2
    assert l % tl == 0 and ck % LANES == 0
    ntiles = l // tl
    const = lambda a: pl.BlockSpec(a.shape, lambda bi, i: (0,) * a.ndim, pipeline_mode=pl.Buffered(1))
    n2, fw = n2.reshape(1, d), fw.reshape(1, d)
    return pl.pallas_call(
        functools.partial(_ffn_fused_kernel, tl=tl, ntiles=ntiles, ck=ck),
        grid=(nb, ntiles),
        in_specs=[pl.BlockSpec((1, tl, d), lambda bi, i: (bi, i, 0)),
                  pl.BlockSpec((1, SUBLANES, 2 * D_FF), lambda bi, i: (bi, 0, 0)),
                  const(n2), const(wu), const(cw), const(cb), const(wd), const(fw)],
        out_specs=[pl.BlockSpec((1, tl, d), lambda bi, i: (bi, i, 0)),
                   pl.BlockSpec((1, SUBLANES, 2 * D_FF), lambda bi, i: (bi, 0, 0))],
        out_shape=[jax.ShapeDtypeStruct((nb, l, d), F32),
                   jax.ShapeDtypeStruct((nb, SUBLANES, 2 * D_FF), F32)],
        scratch_shapes=[pltpu.VMEM((SUBLANES, 2 * D_FF), F32)],
        compiler_params=_cparams(("parallel", "arbitrary")),
        name="ffn_fused",
    )(h3, state8, n2, wu, cw, cb, wd, fw)


def _down_kernel(a_ref, w_ref, h_ref, nw_ref, y_ref):
    h = h_ref[...] + _dot(a_ref[...], w_ref[...])
    ms = jnp.mean(h * h, axis=-1, keepdims=True)
    y_ref[...] = h * lax.rsqrt(ms + RMS_EPS) * nw_ref[...]


def _down(act, w, h, nw):
    t = h.shape[0]
    tm = min(t, 512)
    assert t % tm == 0
    return pl.pallas_call(
        _down_kernel,
        grid=(t // tm,),
        in_specs=[pl.BlockSpec((tm, D_FF), lambda i: (i, 0)),
                  pl.BlockSpec((D_FF, D_MODEL), lambda i: (0, 0)),
                  pl.BlockSpec((tm, D_MODEL), lambda i: (i, 0)),
                  pl.BlockSpec((1, D_MODEL), lambda i: (0, 0))],
        out_specs=pl.BlockSpec((tm, D_MODEL), lambda i: (i, 0)),
        out_shape=jax.ShapeDtypeStruct((t, D_MODEL), F32),
        compiler_params=_cparams(("parallel",)),
        name="ffn_down",
    )(act, w, h, nw.reshape(1, D_MODEL))


def _pad_state_rows(state):
    nb, k, c = state.shape
    return jnp.concatenate([jnp.zeros((nb, SUBLANES - k, c), state.dtype), state], axis=1)


def _layer(x3, mem_k3, mem_v3, ssm0, conv_state, ffn_state, attend, p):
    nb, l, d = x3.shape
    t = nb * l
    x = x3.reshape(t, d)
    nw1 = p["norm1_w"]
    zx = _norm_matmul(x, nw1, p["w_zx"], tn=2048, out_dtype=BF16, name="proj_zx")
    qkvd = _norm_matmul(x, nw1, p["w_qkvd"], tn=QKVD_WIDTH, name="proj_qkvd")
    qm = _norm_matmul(x, nw1, p["w_qm"], tn=MEM_WIDTH, out_dtype=BF16, name="proj_qm")

    zx3 = zx.reshape(nb, l, D_INNER + CONV_DIM)
    qkvd3 = qkvd.reshape(nb, l, QKVD_WIDTH)
    y_ssd, ssm_new = _ssd(zx3, qkvd3, _pad_state_rows(conv_state), ssm0, p["ssd_conv_w"], p["ssd_conv_b"],
                          p["dt_bias"], p["a_log"], p["d_skip"], p["ssd_norm_w"])
    conv_new = zx3[:, l - (SSD_CONV - 1):, D_INNER:].astype(F32)

    y_att, k_new, v_new = attend(qkvd3)
    y_mem = _mem_attn(qm.reshape(nb, l, MEM_WIDTH), mem_k3, mem_v3)

    h = _merge(x, nw1, p["w_gates"], y_ssd.reshape(t, D_INNER), y_att.reshape(t, ATT_WIDTH), y_mem.reshape(t, MEM_WIDTH),
               p["w_ssd_out"], p["w_attn_out"], p["w_mem_out"], p["w_o"])

    if l % 512 == 0:
        y3, last8 = _ffn_fused(h.reshape(nb, l, d), _pad_state_rows(ffn_state), p["norm2_w"], p["w_up"],
                               p["ffn_conv_w"], p["ffn_conv_b"], p["w_down"], p["final_norm_w"])
        ffn_new = last8[:, SUBLANES - (FFN_CONV - 1):]
        return y3, k_new, v_new, ssm_new, conv_new, ffn_new

    u = _norm_matmul(h, p["norm2_w"], p["w_up"], tn=2 * D_FF // 11, name="ffn_up")
    u3 = u.reshape(nb, l, 2 * D_FF)
    ffn_new = u3[:, l - (FFN_CONV - 1):]
    act = _ffn_gate(u3, _pad_state_rows(ffn_state), p["ffn_conv_w"], p["ffn_conv_b"])
    y = _down(act.reshape(t, D_FF), p["w_down"], h, p["final_norm_w"])
    return y.reshape(nb, l, d), k_new, v_new, ssm_new, conv_new, ffn_new


def _attend_prompt(qkvd3):
    b, s, _ = qkvd3.shape
    tables = _rope_tables(jnp.arange(s, dtype=jnp.int32))
    k_rot, v, q_t, k_aug, v_t, kmean = _prompt_prep(qkvd3, tables)
    y_att = _moba_prompt(q_t, k_aug, v_t, kmean)
    return y_att, k_rot.reshape(b, s, KV_HEADS, ATT_HEAD_DIM), v.reshape(b, s, KV_HEADS, ATT_HEAD_DIM)


def _attend_sample(qkvd3, cache_k, cache_v, page_table):
    ns, t, _ = qkvd3.shape
    npages = page_table.shape[1]
    past = npages * PAGE_SIZE
    pos = past + jnp.arange(t, dtype=jnp.int32)
    tables = tuple(jnp.tile(tb, (ns, 1)) for tb in _rope_tables(pos))
    qk = _rope_sample(qkvd3.reshape(ns * t, QKVD_WIDTH), tables)
    q_rot = qk[:, :ATT_WIDTH].reshape(ns, t, KV_HEADS, GQA_REP, ATT_HEAD_DIM)
    k_rot = qk[:, ATT_WIDTH:].reshape(ns, t, KV_WIDTH)
    v = qkvd3[:, :, ATT_WIDTH + KV_WIDTH:ATT_WIDTH + 2 * KV_WIDTH]
    q_ht = q_rot.transpose(0, 2, 3, 1, 4)
    eye = jnp.eye(KV_HEADS, dtype=F32)
    q_exp = (q_ht[:, :, :, :, None, :] * eye[None, :, None, None, :, None]).reshape(ns, ATT_HEADS * t, KV_WIDTH)
    blk_of_key = jnp.arange(past, dtype=jnp.int32) // MOBA_BLOCK
    e_mat = (jnp.arange(LANES, dtype=jnp.int32)[:, None] == blk_of_key[None, :]).astype(BF16)
    n_phys = cache_k.shape[0]
    to_pages = lambda c: jnp.transpose(c, (0, 2, 3, 1)).reshape(n_phys, KV_WIDTH, PAGE_SIZE)
    y_att = _moba_sample(page_table, q_exp.astype(BF16), k_rot, v, e_mat, to_pages(cache_k), to_pages(cache_v))
    return y_att, k_rot.reshape(ns, t, KV_HEADS, ATT_HEAD_DIM), v.reshape(ns, t, KV_HEADS, ATT_HEAD_DIM)


def _layer_params(l, norm1_w, w_in, ssd_conv_w, ssd_conv_b, dt_bias, a_log, d_skip, ssd_norm_w, w_ssd_out,
                  w_attn_out, w_mem_out, w_o, norm2_w, w_up, ffn_conv_w, ffn_conv_b, w_down, final_norm_w):
    w = w_in[l]
    o_z, o_x, o_dt = 0, D_INNER, D_INNER + CONV_DIM
    o_q = o_dt + SSD_HEADS
    o_k, o_v = o_q + ATT_WIDTH, o_q + ATT_WIDTH + KV_WIDTH
    o_qm = o_v + KV_WIDTH
    o_g = o_qm + MEM_WIDTH
    pad_lanes = lambda a: jnp.pad(a, (0, LANES - a.shape[0])).reshape(1, LANES)
    w_dt = jnp.pad(w[:, o_dt:o_q], ((0, 0), (0, LANES - SSD_HEADS)))
    return {
        "norm1_w": norm1_w[l],
        "w_zx": w[:, o_z:o_dt].astype(BF16),
        "w_qkvd": jnp.concatenate([w[:, o_q:o_qm], w_dt], axis=1).astype(BF16),
        "w_qm": w[:, o_qm:o_g].astype(BF16),
        "w_gates": w[:, o_g:].astype(BF16),
        "ssd_conv_w": ssd_conv_w[l],
        "ssd_conv_b": ssd_conv_b[l].reshape(1, CONV_DIM),
        "dt_bias": pad_lanes(dt_bias[l]),
        "a_log": pad_lanes(a_log[l]),
        "d_skip": jnp.repeat(d_skip[l], SSD_HEAD_DIM).reshape(1, D_INNER),
        "ssd_norm_w": ssd_norm_w[l].reshape(1, D_INNER),
        "w_ssd_out": w_ssd_out[l].astype(BF16),
        "w_attn_out": w_attn_out[l].astype(BF16),
        "w_mem_out": w_mem_out[l].astype(BF16),
        "w_o": w_o[l].astype(BF16),
        "norm2_w": norm2_w[l],
        "w_up": w_up[l].astype(BF16),
        "ffn_conv_w": ffn_conv_w[l],
        "ffn_conv_b": ffn_conv_b[l].reshape(1, 2 * D_FF),
        "w_down": w_down[l].astype(BF16),
        "final_norm_w": final_norm_w,
    }


def kernel(x_prompt, x_sample, cache_k, cache_v, cache_mem_k, cache_mem_v, state_ssm, state_conv,
           state_ffn_conv, page_table, mem_prompt, norm1_w, w_in, ssd_conv_w, ssd_conv_b, dt_bias, a_log,
           d_skip, ssd_norm_w, mem_norm_w, w_mem_kv, w_ssd_out, w_attn_out, w_mem_out, w_o, norm2_w, w_up,
           ffn_conv_w, ffn_conv_b, w_down, final_norm_w):
    depth = w_in.shape[0]
    assert depth == 1, "the final RMSNorm is fused into the single layer's last kernel"
    b_p, s_p, _ = x_prompt.shape
    n_mem = mem_prompt.shape[1]
    ns = x_sample.shape[0]
    l = 0
    p = _layer_params(l, norm1_w, w_in, ssd_conv_w, ssd_conv_b, dt_bias, a_log, d_skip, ssd_norm_w, w_ssd_out,
                      w_attn_out, w_mem_out, w_o, norm2_w, w_up, ffn_conv_w, ffn_conv_b, w_down, final_norm_w)

    memx = mem_prompt.reshape(b_p * n_mem, D_MODEL)
    wkv = w_mem_kv[l].astype(BF16)
    mk_p = _norm_matmul(memx, mem_norm_w[l], wkv[:, :MEM_WIDTH], tn=MEM_WIDTH, name="mem_k")
    mv_p = _norm_matmul(memx, mem_norm_w[l], wkv[:, MEM_WIDTH:], tn=MEM_WIDTH, name="mem_v")
    mk_p3 = mk_p.reshape(b_p, n_mem, MEM_WIDTH)
    mv_p3 = mv_p.reshape(b_p, n_mem, MEM_WIDTH)
    ssm0 = jnp.zeros((b_p, SSD_HEADS, SSD_HEAD_DIM, SSD_STATE), F32)
    conv0 = jnp.zeros((b_p, SSD_CONV - 1, CONV_DIM), F32)
    ffn0 = jnp.zeros((b_p, FFN_CONV - 1, 2 * D_FF), F32)
    y_p, k_p, v_p, s_p_new, c_p, f_p = _layer(x_prompt, mk_p3, mv_p3, ssm0, conv0, ffn0, _attend_prompt, p)

    attend_s = functools.partial(_attend_sample, cache_k=cache_k[l], cache_v=cache_v[l], page_table=page_table)
    y_s, k_s, v_s, s_s_new, c_s, f_s = _layer(x_sample, cache_mem_k[l], cache_mem_v[l], state_ssm[l], state_conv[l],
                                              state_ffn_conv[l], attend_s, p)

    mem_shape = (1, b_p, n_mem, MEM_HEADS, MEM_HEAD_DIM)
    return (y_p, y_s, k_p[None], v_p[None], mk_p.reshape(mem_shape), mv_p.reshape(mem_shape),
            s_p_new[None], c_p[None], f_p[None], k_s[None], v_s[None], s_s_new[None], c_s[None], f_s[None])
```
